```python
import jax, jax.numpy as jnp
from jax import lax
import numpy as np

D_MODEL = 1024
BATCH = 8
SEQ = 2048
DEPTH = 2
DEC_BATCH = 128
DEC_SEQ = 1
PAST_LEN = 16384
PAGE_SIZE = 128

N_RET_LAYERS = (DEPTH + 1) // 2
N_CONV_LAYERS = DEPTH // 2
RET_HEADS = 4
RET_DK = D_MODEL // 8
RET_DV = D_MODEL // 4
RET_CHUNK = 128
ROPE_BASE = 10000.0
SGU_GROUPS = 4
SGU_GD = D_MODEL // 4
SGU_CHUNK = 128
CONV_WIDTH = 31
D_CONV = D_MODEL
D_FF = 4 * D_MODEL
RMS_EPS = 1e-6
LN_EPS = 1e-5

Q_W = RET_HEADS * RET_DK
V_W = RET_HEADS * RET_DV
U_W = SGU_GROUPS * SGU_GD
SPLIT_IDX = (Q_W, 2 * Q_W, 2 * Q_W + V_W, 2 * Q_W + 2 * V_W, 2 * Q_W + 2 * V_W + U_W)
D_IN = 2 * Q_W + 2 * V_W + 2 * U_W
D_MIX_OUT = V_W + U_W

kernel_name = "retention_sgu_conformer_hybrid_step"


def rmsnorm(x, g):
    x32 = x.astype(jnp.float32)
    y = x32 * lax.rsqrt(jnp.mean(x32 * x32, axis=-1, keepdims=True) + RMS_EPS)
    return (y * g.astype(jnp.float32)).astype(x.dtype)


def layernorm32(x32, g, b):
    mu = jnp.mean(x32, axis=-1, keepdims=True)
    var = jnp.mean(jnp.square(x32 - mu), axis=-1, keepdims=True)
    return (x32 - mu) * lax.rsqrt(var + LN_EPS) * g.astype(jnp.float32) + b.astype(jnp.float32)


def rotary(x, pos):
    half = x.shape[-1] // 2
    inv = ROPE_BASE ** (-jnp.arange(half, dtype=jnp.float32) / half)
    ang = pos.astype(jnp.float32)[:, None] * inv[None, :]
    cos = jnp.cos(ang)[None, :, None, :]
    sin = jnp.sin(ang)[None, :, None, :]
    x32 = x.astype(jnp.float32)
    x1, x2 = x32[..., :half], x32[..., half:]
    return jnp.concatenate([x1 * cos - x2 * sin, x1 * sin + x2 * cos], axis=-1)


def retention(q, k, v, s0):
    b, t, h, _ = q.shape
    c = min(t, RET_CHUNK)
    nc = t // c

    def to_chunks(a):
        return a.reshape(b, nc, c, h, a.shape[-1]).transpose(1, 0, 3, 2, 4)

    log_g = jnp.log1p(-(2.0 ** (-5.0 - jnp.arange(RET_HEADS, dtype=jnp.float32))))
    idx = jnp.arange(c, dtype=jnp.float32)
    diff = idx[:, None] - idx[None, :]
    dmask = jnp.where(diff >= 0, jnp.exp(jnp.maximum(diff, 0.0)[None] * log_g[:, None, None]), 0.0)
    xi = jnp.exp((idx[None, :] + 1.0) * log_g[:, None])
    zeta = jnp.exp((c - 1.0 - idx[None, :]) * log_g[:, None])
    chunk_decay = jnp.exp(c * log_g)[:, None, None]

    def step(s, inp):
        qc, kc, vc = inp
        inner = jnp.einsum('bhnd,bhmd->bhnm', qc, kc) * dmask
        o = jnp.einsum('bhnm,bhme->bhne', inner, vc) \
            + jnp.einsum('bhnd,bhde->bhne', qc, s) * xi[None, :, :, None]
        s_new = chunk_decay * s + jnp.einsum('bhmd,bhme->bhde', kc * zeta[None, :, :, None], vc)
        return s_new, o

    s_fin, o = lax.scan(step, s0, (to_chunks(q), to_chunks(k), to_chunks(v)))
    o = o.transpose(1, 0, 3, 2, 4).reshape(b, t, h, v.shape[-1])
    return o, s_fin


def mixer_ret_sgu(h, pos, s0, w_in, w_out, gn_g, ln_g, ln_b, sgu_w, sgu_b):
    b, t, _ = h.shape
    z = h @ w_in
    q, k, v, g, u, s = jnp.split(z, SPLIT_IDX, axis=-1)
    q = rotary(q.reshape(b, t, RET_HEADS, RET_DK), pos)
    k = rotary(k.reshape(b, t, RET_HEADS, RET_DK), pos) * (RET_DK ** -0.5)
    v = v.reshape(b, t, RET_HEADS, RET_DV).astype(jnp.float32)
    o, s_new = retention(q, k, v, s0.astype(jnp.float32))
    mu = jnp.mean(o, axis=-1, keepdims=True)
    var = jnp.mean(jnp.square(o - mu), axis=-1, keepdims=True)
    o_n = ((o - mu) * lax.rsqrt(var + LN_EPS)).reshape(b, t, V_W) * gn_g.astype(jnp.float32)
    ret_out = (jax.nn.silu(g.astype(jnp.float32)) * o_n).astype(h.dtype)
    u = jax.nn.gelu(u, approximate=False)
    s = jax.nn.gelu(s, approximate=False).reshape(b, t, SGU_GROUPS, SGU_GD)
    s_n = layernorm32(s.astype(jnp.float32), ln_g.reshape(SGU_GROUPS, SGU_GD),
                      ln_b.reshape(SGU_GROUPS, SGU_GD)).astype(h.dtype)
    c = min(t, SGU_CHUNK)
    nc = t // c
    tri = jnp.tril(jnp.ones((c, c), dtype=bool))
    w_s = jnp.where(tri[None], sgu_w[:, :c, :c], 0.0).astype(h.dtype)
    bias = sgu_b[:, :c].T.astype(h.dtype)
    mixed = jnp.einsum('gij,bnjgd->bnigd', w_s, s_n.reshape(b, nc, c, SGU_GROUPS, SGU_GD)) \
        + bias[:, :, None]
    sgu_out = u * mixed.reshape(b, t, U_W)
    y = jnp.concatenate([ret_out, sgu_out], axis=-1) @ w_out
    return y, s_new, s_n.reshape(b, t, U_W)


def mixer_conv(h, hist, w_in, b_in, dw_w, dw_b, ln_g, ln_b, w_out, b_out):
    a, gt = jnp.split(h @ w_in + b_in, 2, axis=-1)
    xg = a * jax.nn.sigmoid(gt)
    full = jnp.concatenate([hist.astype(xg.dtype), xg], axis=1)
    conv = lax.conv_general_dilated(full, dw_w[:, None, :].astype(xg.dtype), (1,), 'VALID',
                                    dimension_numbers=('NWC', 'WIO', 'NWC'),
                                    feature_group_count=D_CONV) + dw_b
    c = jax.nn.silu(layernorm32(conv.astype(jnp.float32), ln_g, ln_b)).astype(h.dtype)
    y = c @ w_out + b_out
    new_hist = full[:, full.shape[1] - (CONV_WIDTH - 1):]
    return y, new_hist


def trunk(x, pos, ret_s0, conv_h0, norm_mix_g, norm_ffn_g, final_norm_g,
          ab_w_in, ab_w_out, ret_gn_g, sgu_ln_g, sgu_ln_b, sgu_w, sgu_b,
          conv_w_in, conv_b_in, conv_dw_w, conv_dw_b, conv_ln_g, conv_ln_b,
          conv_w_out, conv_b_out, ffn_w1, ffn_w2):
    ret_states, sgu_rows, conv_states = [], [], []
    for l in range(DEPTH):
        i = l // 2
        h = rmsnorm(x, norm_mix_g[l])
        if l % 2 == 0:
            y, s_new, rows = mixer_ret_sgu(h, pos, ret_s0[i], ab_w_in[i], ab_w_out[i], ret_gn_g[i],
                                           sgu_ln_g[i], sgu_ln_b[i], sgu_w[i], sgu_b[i])
            ret_states.append(s_new)
            sgu_rows.append(rows)
        else:
            y, hist = mixer_conv(h, conv_h0[i], conv_w_in[i], conv_b_in[i], conv_dw_w[i], conv_dw_b[i],
                                 conv_ln_g[i], conv_ln_b[i], conv_w_out[i], conv_b_out[i])
            conv_states.append(hist)
        x = x + y
        f = rmsnorm(x, norm_ffn_g[l]) @ ffn_w1[l]
        x = x + jnp.square(jax.nn.relu(f)) @ ffn_w2[l]
    return (rmsnorm(x, final_norm_g), jnp.stack(ret_states), jnp.stack(sgu_rows), jnp.stack(conv_states))


def setup_inputs(seed: int = 0) -> dict:
    key = jax.random.key(seed)
    ks = jax.random.split(key, 32)

    def nrm(k, shape, scale):
        return jax.random.normal(k, shape, jnp.float32) * scale

    return {
        "x_prompt": nrm(ks[0], (BATCH, SEQ, D_MODEL), 1.0),
        "x_sample": nrm(ks[1], (DEC_BATCH, DEC_SEQ, D_MODEL), 1.0),
        "state_ret": nrm(ks[2], (N_RET_LAYERS, DEC_BATCH, RET_HEADS, RET_DK, RET_DV), 0.5),
        "state_conv": nrm(ks[3], (N_CONV_LAYERS, DEC_BATCH, CONV_WIDTH - 1, D_CONV), 0.5),
        "norm_mix_g": 1.0 + nrm(ks[4], (DEPTH, D_MODEL), 0.02),
        "norm_ffn_g": 1.0 + nrm(ks[5], (DEPTH, D_MODEL), 0.02),
        "final_norm_g": 1.0 + nrm(ks[6], (D_MODEL,), 0.02),
        "ab_w_in": nrm(ks[7], (N_RET_LAYERS, D_MODEL, D_IN), D_MODEL ** -0.5),
        "ab_w_out": nrm(ks[8], (N_RET_LAYERS, D_MIX_OUT, D_MODEL), D_MIX_OUT ** -0.5),
        "ret_gn_g": 1.0 + nrm(ks[9], (N_RET_LAYERS, V_W), 0.02),
        "sgu_ln_g": 1.0 + nrm(ks[10], (N_RET_LAYERS, U_W), 0.02),
        "sgu_ln_b": nrm(ks[11], (N_RET_LAYERS, U_W), 0.02),
        "sgu_w": nrm(ks[12], (N_RET_LAYERS, SGU_GROUPS, SGU_CHUNK, SGU_CHUNK), SGU_CHUNK ** -0.5),
        "sgu_b": 1.0 + nrm(ks[13], (N_RET_LAYERS, SGU_GROUPS, SGU_CHUNK), 0.1),
        "conv_w_in": nrm(ks[14], (N_CONV_LAYERS, D_MODEL, 2 * D_CONV), D_MODEL ** -0.5),
        "conv_b_in": nrm(ks[15], (N_CONV_LAYERS, 2 * D_CONV), 0.02),
        "conv_dw_w": nrm(ks[16], (N_CONV_LAYERS, CONV_WIDTH, D_CONV), CONV_WIDTH ** -0.5),
        "conv_dw_b": nrm(ks[17], (N_CONV_LAYERS, D_CONV), 0.02),
        "conv_ln_g": 1.0 + nrm(ks[18], (N_CONV_LAYERS, D_CONV), 0.02),
        "conv_ln_b": nrm(ks[19], (N_CONV_LAYERS, D_CONV), 0.02),
        "conv_w_out": nrm(ks[20], (N_CONV_LAYERS, D_CONV, D_MODEL), D_CONV ** -0.5),
        "conv_b_out": nrm(ks[21], (N_CONV_LAYERS, D_MODEL), 0.02),
        "ffn_w1": nrm(ks[22], (DEPTH, D_MODEL, D_FF), D_MODEL ** -0.5),
        "ffn_w2": nrm(ks[23], (DEPTH, D_FF, D_MODEL), D_FF ** -0.5),
    }


def reference(x_prompt, x_sample, state_ret, state_conv, norm_mix_g, norm_ffn_g, final_norm_g,
              ab_w_in, ab_w_out, ret_gn_g, sgu_ln_g, sgu_ln_b, sgu_w, sgu_b,
              conv_w_in, conv_b_in, conv_dw_w, conv_dw_b, conv_ln_g, conv_ln_b,
              conv_w_out, conv_b_out, ffn_w1, ffn_w2):
    weights = (norm_mix_g, norm_ffn_g, final_norm_g, ab_w_in, ab_w_out, ret_gn_g, sgu_ln_g, sgu_ln_b,
               sgu_w, sgu_b, conv_w_in, conv_b_in, conv_dw_w, conv_dw_b, conv_ln_g, conv_ln_b,
               conv_w_out, conv_b_out, ffn_w1, ffn_w2)
    pos_p = jnp.arange(SEQ)
    ret0_p = jnp.zeros((N_RET_LAYERS, x_prompt.shape[0], RET_HEADS, RET_DK, RET_DV), jnp.float32)
    conv0_p = jnp.zeros((N_CONV_LAYERS, x_prompt.shape[0], CONV_WIDTH - 1, D_CONV), x_prompt.dtype)
    y_prompt, ret_state_prompt, _, conv_state_prompt = trunk(x_prompt, pos_p, ret0_p, conv0_p, *weights)
    pos_s = PAST_LEN + jnp.arange(x_sample.shape[1])
    y_sample, ret_state_sample, sgu_rows_sample, conv_state_sample = trunk(
        x_sample, pos_s, state_ret, state_conv, *weights)
    return (y_prompt, y_sample, ret_state_prompt, ret_state_sample, sgu_rows_sample,
            conv_state_prompt, conv_state_sample)
```

```python
import functools

import jax
import jax.numpy as jnp
from jax import lax
from jax.experimental import pallas as pl
from jax.experimental.pallas import tpu as pltpu

F32 = jnp.float32
BF16 = jnp.bfloat16

D_MODEL = 1024
HEADS = 4
DK = 128
DV = 256
GROUPS = 4
GD = 256
CHUNK = 128
ROPE_BASE = 10000.0
CONV_WIDTH = 31
HIST = CONV_WIDTH - 1
D_FF = 4 * D_MODEL
RMS_EPS = 1e-6
LN_EPS = 1e-5
PAST_LEN = 16384

Q_W = HEADS * DK
V_W = HEADS * DV
U_W = GROUPS * GD
OFF_V = 2 * Q_W
OFF_G = OFF_V + V_W
OFF_U = OFF_G + V_W
OFF_S = OFF_U + U_W
D_IN = OFF_S + U_W

SUBLANES = 8
VMEM_LIMIT = 56 * 1024 * 1024

TM_MIX = 512
TM_FFN = 512
TM_CONV = 512
CONV_RB = 32
CONV_LC = 512
HIST_PAD = 32
BB_RET = 8
BB_CONV = 16


def _params(*sem):
    return pltpu.CompilerParams(dimension_semantics=sem, vmem_limit_bytes=VMEM_LIMIT)


def _const_spec(shape):
    n = len(shape)
    return pl.BlockSpec(shape, lambda *_: (0,) * n, pipeline_mode=pl.Buffered(1))


def _smem_spec():
    return pl.BlockSpec(memory_space=pltpu.SMEM)


def _rms(x, g):
    ms = jnp.mean(x * x, axis=-1, keepdims=True)
    return x * lax.rsqrt(ms + RMS_EPS) * g


def _layernorm(x, g, b):
    mu = jnp.mean(x, axis=-1, keepdims=True)
    xc = x - mu
    var = jnp.mean(xc * xc, axis=-1, keepdims=True)
    return xc * lax.rsqrt(var + LN_EPS) * g + b


def _groupnorm(x, g):
    mu = jnp.mean(x, axis=-1, keepdims=True)
    xc = x - mu
    var = jnp.mean(xc * xc, axis=-1, keepdims=True)
    return xc * lax.rsqrt(var + LN_EPS) * g


def _sigmoid(x):
    return 1.0 / (1.0 + jnp.exp(-x))


def _silu(x):
    return x * _sigmoid(x)


def _gelu(x):
    return 0.5 * x * (1.0 + lax.erf(x * 0.7071067811865476))


def _dot(a, b):
    return jnp.dot(a, b, preferred_element_type=F32)


def _dot_nt(a, b):
    return lax.dot_general(a, b, (((1,), (1,)), ((), ())), preferred_element_type=F32)


def _dot_tn(a, b):
    return lax.dot_general(a, b, (((0,), (0,)), ((), ())), preferred_element_type=F32)


def _rotary(blk, cos2, sin2):
    return blk * cos2 + pltpu.roll(blk, DK // 2, 1) * sin2


def _mix_prompt_kernel(cd_ref, x_ref, gmix_ref, win_ref, wout_ref, cos_ref, sin_ref,
                       dmask_ref, xi_ref, zeta_ref, gn_ref, lng_ref, lnb_ref,
                       wsgu_ref, bsgu_ref,
                       x1_ref, state_ref,
                       q_s, k_s, v_s, gate_s, u_s, sn_s, mix_s):
    t = pl.program_id(1)
    tm = x_ref.shape[1]

    @pl.when(t == 0)
    def _():
        state_ref[...] = jnp.zeros_like(state_ref)

    x = x_ref[0]
    h = _rms(x, gmix_ref[...]).astype(BF16)

    qk = _dot(h, win_ref[:, 0:OFF_V])
    cos2 = cos_ref[...]
    sin2 = sin_ref[...]
    for j in range(HEADS):
        q_s[:, j * DK:(j + 1) * DK] = _rotary(qk[:, j * DK:(j + 1) * DK], cos2, sin2).astype(BF16)
        kj = qk[:, Q_W + j * DK:Q_W + (j + 1) * DK]
        k_s[:, j * DK:(j + 1) * DK] = _rotary(kj, cos2, sin2) * (DK ** -0.5)
    v_s[...] = _dot(h, win_ref[:, OFF_V:OFF_G]).astype(BF16)
    gate_s[...] = _silu(_dot(h, win_ref[:, OFF_G:OFF_U]))
    u_s[...] = _gelu(_dot(h, win_ref[:, OFF_U:OFF_S]))
    s = _gelu(_dot(h, win_ref[:, OFF_S:D_IN]))
    for g in range(GROUPS):
        gs = slice(g * GD, (g + 1) * GD)
        sn_s[:, gs] = _layernorm(s[:, gs], lng_ref[:, gs], lnb_ref[:, gs]).astype(BF16)

    row = lax.broadcasted_iota(jnp.int32, (CHUNK, CHUNK), 0)
    col = lax.broadcasted_iota(jnp.int32, (CHUNK, CHUNK), 1)
    tril = row >= col

    for c in range(tm // CHUNK):
        rs = slice(c * CHUNK, (c + 1) * CHUNK)
        for hh in range(HEADS):
            ks = slice(hh * DK, (hh + 1) * DK)
            vs = slice(hh * DV, (hh + 1) * DV)
            qc = q_s[rs, ks]
            kf = k_s[rs, ks]
            vc = v_s[rs, vs]
            st = state_ref[0, 0, hh]
            inner = _dot_nt(qc, kf.astype(BF16)) * dmask_ref[hh]
            o = _dot(inner.astype(BF16), vc) + _dot(qc, st.astype(BF16)) * xi_ref[hh]
            kz = (kf * zeta_ref[hh]).astype(BF16)
            state_ref[0, 0, hh] = cd_ref[hh] * st + _dot_tn(kz, vc)
            on = _groupnorm(o, gn_ref[:, vs])
            mix_s[rs, vs] = (gate_s[rs, vs] * on).astype(BF16)
        for g in range(GROUPS):
            gs = slice(g * GD, (g + 1) * GD)
            wg = jnp.where(tril, wsgu_ref[g], 0.0).astype(BF16)
            mixed = _dot(wg, sn_s[rs, gs]) + bsgu_ref[g]
            mix_s[rs, V_W + g * GD:V_W + (g + 1) * GD] = (u_s[rs, gs] * mixed).astype(BF16)

    x1_ref[0] = x + _dot(mix_s[...], wout_ref[...])


def _mix_prompt(x, gmix, win, wout, cos2, sin2, dmask, xi_b, zeta_b, cd, gn, lng, lnb, wsgu, bsgu_b):
    b, t, d = x.shape
    tm = TM_MIX
    grid = (b, t // tm)
    in_specs = [
        _smem_spec(),
        pl.BlockSpec((1, tm, d), lambda i, j: (i, j, 0)),
        _const_spec((1, d)),
        _const_spec((d, D_IN)),
        _const_spec((V_W + U_W, d)),
        pl.BlockSpec((tm, DK), lambda i, j: (j, 0)),
        pl.BlockSpec((tm, DK), lambda i, j: (j, 0)),
        _const_spec((HEADS, CHUNK, CHUNK)),
        _const_spec((HEADS, CHUNK, DV)),
        _const_spec((HEADS, CHUNK, DK)),
        _const_spec((1, V_W)),
        _const_spec((1, U_W)),
        _const_spec((1, U_W)),
        _const_spec((GROUPS, CHUNK, CHUNK)),
        _const_spec((GROUPS, CHUNK, GD)),
    ]
    out_specs = [
        pl.BlockSpec((1, tm, d), lambda i, j: (i, j, 0)),
        pl.BlockSpec((1, 1, HEADS, DK, DV), lambda i, j: (0, i, 0, 0, 0)),
    ]
    out_shape = [
        jax.ShapeDtypeStruct((b, t, d), F32),
        jax.ShapeDtypeStruct((1, b, HEADS, DK, DV), F32),
    ]
    scratch = [
        pltpu.VMEM((tm, Q_W), BF16),
        pltpu.VMEM((tm, Q_W), F32),
        pltpu.VMEM((tm, V_W), BF16),
        pltpu.VMEM((tm, V_W), F32),
        pltpu.VMEM((tm, U_W), F32),
        pltpu.VMEM((tm, U_W), BF16),
        pltpu.VMEM((tm, V_W + U_W), BF16),
    ]
    return pl.pallas_call(
        _mix_prompt_kernel, grid=grid, in_specs=in_specs, out_specs=out_specs,
        out_shape=out_shape, scratch_shapes=scratch,
        compiler_params=_params("arbitrary", "arbitrary"), name="mix_prompt",
    )(cd, x, gmix, win, wout, cos2, sin2, dmask, xi_b, zeta_b, gn, lng, lnb, wsgu, bsgu_b)


def _ffn_kernel(x_ref, g_ref, w1_ref, w2_ref, gfin_ref, o_ref, *, final):
    x = x_ref[...]
    h = _rms(x, g_ref[...]).astype(BF16)
    acc = x
    for j in range(D_FF // D_MODEL):
        js = slice(j * D_MODEL, (j + 1) * D_MODEL)
        f = _dot(h, w1_ref[:, js])
        a = jnp.square(jnp.maximum(f, 0.0)).astype(BF16)
        acc = acc + _dot(a, w2_ref[js, :])
    if final:
        acc = _rms(acc, gfin_ref[...])
    o_ref[...] = acc


def _ffn(x, g, w1, w2, gfin, final, tm):
    m, d = x.shape
    return pl.pallas_call(
        functools.partial(_ffn_kernel, final=final),
        grid=(m // tm,),
        in_specs=[pl.BlockSpec((tm, d), lambda i: (i, 0)),
                  _const_spec((1, d)), _const_spec((d, D_FF)), _const_spec((D_FF, d)),
                  _const_spec((1, d))],
        out_specs=pl.BlockSpec((tm, d), lambda i: (i, 0)),
        out_shape=jax.ShapeDtypeStruct((m, d), F32),
        compiler_params=_params("arbitrary"), name="ffn_final" if final else "ffn",
    )(x, g, w1, w2, gfin)


def _glu_in(x, g, win_ref, bin_ref):
    h = _rms(x, g).astype(BF16)
    z = _dot(h, win_ref[...]) + bin_ref[...]
    return z[:, :D_MODEL] * _sigmoid(z[:, D_MODEL:])


def _conv_prompt_kernel(x_ref, g_ref, win_ref, bin_ref, dww_ref, dwb_ref, lng_ref, lnb_ref,
                        wout_ref, bout_ref, o_ref, hist_ref, full_s, conv_s):
    t = pl.program_id(1)
    nt = pl.num_programs(1)
    tm = x_ref.shape[1]

    @pl.when(t == 0)
    def _():
        full_s[0:HIST_PAD, :] = jnp.zeros((HIST_PAD, D_MODEL), F32)

    x = x_ref[0]
    full_s[HIST_PAD:HIST_PAD + tm, :] = _glu_in(x, g_ref[...], win_ref, bin_ref)

    off0 = HIST_PAD - HIST

    def body(r, carry):
        base = pl.multiple_of(r * CONV_RB, CONV_RB)
        for lc in range(D_MODEL // CONV_LC):
            ls = slice(lc * CONV_LC, (lc + 1) * CONV_LC)
            out = dwb_ref[:, ls]
            for p in range(SUBLANES):
                rows = CONV_RB + (SUBLANES if p else 0)
                z = None
                for a in range(HIST_PAD // SUBLANES + 1):
                    m = SUBLANES * a + p
                    if m < off0 or m > HIST_PAD:
                        continue
                    term = dww_ref[m - off0:m - off0 + 1, ls] * full_s[pl.ds(base + SUBLANES * a, rows), ls]
                    z = term if z is None else z + term
                out = out + z[p:p + CONV_RB, :]
            conv_s[pl.ds(base, CONV_RB), ls] = out
        return carry

    lax.fori_loop(0, tm // CONV_RB, body, 0)

    c = _silu(_layernorm(conv_s[...], lng_ref[...], lnb_ref[...])).astype(BF16)
    o_ref[0] = x + _dot(c, wout_ref[...]) + bout_ref[...]

    @pl.when(t == nt - 1)
    def _():
        hist_ref[0, 0] = full_s[tm + off0:tm + HIST_PAD, :]

    full_s[0:HIST_PAD, :] = full_s[tm:tm + HIST_PAD, :]


def _conv_prompt(x, g, win, bin_, dww, dwb, lng, lnb, wout, bout):
    b, t, d = x.shape
    tm = TM_CONV
    in_specs = [
        pl.BlockSpec((1, tm, d), lambda i, j: (i, j, 0)),
        _const_spec((1, d)), _const_spec((d, 2 * d)), _const_spec((1, 2 * d)),
        _const_spec((CONV_WIDTH, d)), _const_spec((1, d)), _const_spec((1, d)), _const_spec((1, d)),
        _const_spec((d, d)), _const_spec((1, d)),
    ]
    out_specs = [
        pl.BlockSpec((1, tm, d), lambda i, j: (i, j, 0)),
        pl.BlockSpec((1, 1, HIST, d), lambda i, j: (0, i, 0, 0)),
    ]
    out_shape = [jax.ShapeDtypeStruct((b, t, d), F32),
                 jax.ShapeDtypeStruct((1, b, HIST, d), F32)]
    return pl.pallas_call(
        _conv_prompt_kernel, grid=(b, t // tm), in_specs=in_specs, out_specs=out_specs,
        out_shape=out_shape,
        scratch_shapes=[pltpu.VMEM((tm + HIST_PAD, d), F32), pltpu.VMEM((tm, d), F32)],
        compiler_params=_params("arbitrary", "arbitrary"), name="conv_prompt",
    )(x, g, win, bin_, dww, dwb, lng, lnb, wout, bout)


def _mix_sample_in_kernel(sgu_ref, x_ref, gmix_ref, win_ref, cos_ref, sin_ref, lng_ref, lnb_ref,
                          q_ref, k_ref, v_ref, gate_ref, sgu_out_ref, rows_ref):
    h = _rms(x_ref[...], gmix_ref[...]).astype(BF16)
    qk = _dot(h, win_ref[:, 0:OFF_V])
    cos2 = cos_ref[...]
    sin2 = sin_ref[...]
    for j in range(HEADS):
        q_ref[:, j * DK:(j + 1) * DK] = _rotary(qk[:, j * DK:(j + 1) * DK], cos2, sin2)
        kj = qk[:, Q_W + j * DK:Q_W + (j + 1) * DK]
        k_ref[:, j * DK:(j + 1) * DK] = _rotary(kj, cos2, sin2) * (DK ** -0.5)
    v_ref[...] = _dot(h, win_ref[:, OFF_V:OFF_G])
    gate_ref[...] = _silu(_dot(h, win_ref[:, OFF_G:OFF_U]))
    u = _gelu(_dot(h, win_ref[:, OFF_U:OFF_S]))
    s = _gelu(_dot(h, win_ref[:, OFF_S:D_IN]))
    for g in range(GROUPS):
        gs = slice(g * GD, (g + 1) * GD)
        sn = _layernorm(s[:, gs], lng_ref[:, gs], lnb_ref[:, gs])
        rows_ref[:, gs] = sn
        sgu_out_ref[:, gs] = u[:, gs] * (sgu_ref[0, g] * sn + sgu_ref[1, g])


def _mix_sample_in(x, gmix, win, cos2, sin2, lng, lnb, sgu_wb):
    m, d = x.shape
    shp = lambda n: jax.ShapeDtypeStruct((m, n), F32)
    vm = lambda: pl.BlockSpec(memory_space=pltpu.VMEM)
    return pl.pallas_call(
        _mix_sample_in_kernel,
        in_specs=[_smem_spec()] + [vm() for _ in range(7)],
        out_specs=[vm() for _ in range(6)],
        out_shape=[shp(Q_W), shp(Q_W), shp(V_W), shp(V_W), shp(U_W), shp(U_W)],
        compiler_params=pltpu.CompilerParams(vmem_limit_bytes=VMEM_LIMIT), name="mix_sample_in",
    )(sgu_wb, x, gmix, win, cos2, sin2, lng, lnb)


def _ret_sample_kernel(decay_ref, q_ref, k_ref, v_ref, s_ref, snew_ref, o_ref):
    bb = q_ref.shape[0]
    for hh in range(HEADS):
        ks = slice(hh * DK, (hh + 1) * DK)
        vs = slice(hh * DV, (hh + 1) * DV)
        qt = q_ref[:, ks].T
        kt = k_ref[:, ks].T
        gam = decay_ref[hh]
        for b in range(bb):
            st = s_ref[0, b, hh]
            sn = gam * st + kt[:, b:b + 1] * v_ref[b:b + 1, vs]
            snew_ref[0, b, hh] = sn
            o_ref[b:b + 1, vs] = jnp.sum(qt[:, b:b + 1] * sn, axis=0, keepdims=True)


def _ret_sample(decay, q, k, v, state):
    m = q.shape[0]
    bb = BB_RET
    return pl.pallas_call(
        _ret_sample_kernel, grid=(m // bb,),
        in_specs=[_smem_spec(),
                  pl.BlockSpec((bb, Q_W), lambda i: (i, 0)),
                  pl.BlockSpec((bb, Q_W), lambda i: (i, 0)),
                  pl.BlockSpec((bb, V_W), lambda i: (i, 0)),
                  pl.BlockSpec((1, bb, HEADS, DK, DV), lambda i: (0, i, 0, 0, 0))],
        out_specs=[pl.BlockSpec((1, bb, HEADS, DK, DV), lambda i: (0, i, 0, 0, 0)),
                   pl.BlockSpec((bb, V_W), lambda i: (i, 0))],
        out_shape=[jax.ShapeDtypeStruct(state.shape, F32), jax.ShapeDtypeStruct((m, V_W), F32)],
        compiler_params=_params("arbitrary"), name="ret_sample",
    )(decay, q, k, v, state)


def _mix_sample_out_kernel(x_ref, o_ref, gate_ref, sgu_ref, gn_ref, wout_ref, x1_ref):
    o = o_ref[...]
    parts = []
    for hh in range(HEADS):
        vs = slice(hh * DV, (hh + 1) * DV)
        parts.append((gate_ref[:, vs] * _groupnorm(o[:, vs], gn_ref[:, vs])).astype(BF16))
    y = _dot(jnp.concatenate(parts, axis=-1), wout_ref[0:V_W, :])
    y = y + _dot(sgu_ref[...].astype(BF16), wout_ref[V_W:V_W + U_W, :])
    x1_ref[...] = x_ref[...] + y


def _mix_sample_out(x, o, gate, sgu_out, gn, wout):
    m, d = x.shape
    vm = lambda: pl.BlockSpec(memory_space=pltpu.VMEM)
    return pl.pallas_call(
        _mix_sample_out_kernel, in_specs=[vm() for _ in range(6)], out_specs=vm(),
        out_shape=jax.ShapeDtypeStruct((m, d), F32),
        compiler_params=pltpu.CompilerParams(vmem_limit_bytes=VMEM_LIMIT), name="mix_sample_out",
    )(x, o, gate, sgu_out, gn, wout)


def _conv_sample_in_kernel(x_ref, g_ref, win_ref, bin_ref, xg_ref):
    xg_ref[...] = _glu_in(x_ref[...], g_ref[...], win_ref, bin_ref)


def _conv_sample_in(x, g, win, bin_):
    m, d = x.shape
    vm = lambda: pl.BlockSpec(memory_space=pltpu.VMEM)
    return pl.pallas_call(
        _conv_sample_in_kernel, in_specs=[vm() for _ in range(4)], out_specs=vm(),
        out_shape=jax.ShapeDtypeStruct((m, d), F32),
        compiler_params=pltpu.CompilerParams(vmem_limit_bytes=VMEM_LIMIT), name="conv_sample_in",
    )(x, g, win, bin_)


def _conv_sample_state_kernel(hist_ref, xg_ref, dww_ref, dwb_ref, newhist_ref, conv_ref):
    hist = hist_ref[0]
    xg = xg_ref[...]
    acc = jnp.sum(hist * dww_ref[0:HIST, :][None], axis=1, keepdims=True)
    conv_ref[...] = acc + xg * dww_ref[HIST:CONV_WIDTH, :][None] + dwb_ref[...][None]
    newhist_ref[0, :, 0:HIST - 1, :] = hist_ref[0, :, 1:HIST, :]
    newhist_ref[0, :, HIST - 1:HIST, :] = xg


def _conv_sample_state(hist, xg3, dww, dwb):
    _, m, _, d = hist.shape
    bb = BB_CONV
    return pl.pallas_call(
        _conv_sample_state_kernel, grid=(m // bb,),
        in_specs=[pl.BlockSpec((1, bb, HIST, d), lambda i: (0, i, 0, 0)),
                  pl.BlockSpec((bb, 1, d), lambda i: (i, 0, 0)),
                  _const_spec((CONV_WIDTH, d)), _const_spec((1, d))],
        out_specs=[pl.BlockSpec((1, bb, HIST, d), lambda i: (0, i, 0, 0)),
                   pl.BlockSpec((bb, 1, d), lambda i: (i, 0, 0))],
        out_shape=[jax.ShapeDtypeStruct(hist.shape, F32), jax.ShapeDtypeStruct((m, 1, d), F32)],
        compiler_params=_params("arbitrary"), name="conv_sample_state",
    )(hist, xg3, dww, dwb)


def _conv_sample_out_kernel(x_ref, conv_ref, lng_ref, lnb_ref, wout_ref, bout_ref, o_ref):
    c = _silu(_layernorm(conv_ref[...], lng_ref[...], lnb_ref[...])).astype(BF16)
    o_ref[...] = x_ref[...] + _dot(c, wout_ref[...]) + bout_ref[...]


def _conv_sample_out(x, conv, lng, lnb, wout, bout):
    m, d = x.shape
    vm = lambda: pl.BlockSpec(memory_space=pltpu.VMEM)
    return pl.pallas_call(
        _conv_sample_out_kernel, in_specs=[vm() for _ in range(6)], out_specs=vm(),
        out_shape=jax.ShapeDtypeStruct((m, d), F32),
        compiler_params=pltpu.CompilerParams(vmem_limit_bytes=VMEM_LIMIT), name="conv_sample_out",
    )(x, conv, lng, lnb, wout, bout)


def _rope_tables(pos):
    half = DK // 2
    inv = ROPE_BASE ** (-jnp.arange(half, dtype=F32) / half)
    ang = pos.astype(F32)[:, None] * inv[None, :]
    cos = jnp.cos(ang)
    sin = jnp.sin(ang)
    return jnp.concatenate([cos, cos], axis=-1), jnp.concatenate([-sin, sin], axis=-1)


def _decay_tables(c):
    log_g = jnp.log1p(-(2.0 ** (-5.0 - jnp.arange(HEADS, dtype=F32))))
    idx = jnp.arange(c, dtype=F32)
    diff = idx[:, None] - idx[None, :]
    dmask = jnp.where(diff >= 0, jnp.exp(jnp.maximum(diff, 0.0)[None] * log_g[:, None, None]), 0.0)
    xi = jnp.exp((idx[None, :] + 1.0) * log_g[:, None])
    zeta = jnp.exp((c - 1.0 - idx[None, :]) * log_g[:, None])
    chunk_decay = jnp.exp(c * log_g)
    return dmask, xi, zeta, chunk_decay


def kernel(x_prompt, x_sample, state_ret, state_conv, norm_mix_g, norm_ffn_g, final_norm_g, ab_w_in, ab_w_out, ret_gn_g, sgu_ln_g, sgu_ln_b, sgu_w, sgu_b, conv_w_in, conv_b_in, conv_dw_w, conv_dw_b, conv_ln_g, conv_ln_b, conv_w_out, conv_b_out, ffn_w1, ffn_w2):
    b, t, d = x_prompt.shape
    m_s = x_sample.shape[0]
    row = lambda a: a.reshape(1, -1)

    win = ab_w_in[0].astype(BF16)
    wout = ab_w_out[0].astype(BF16)
    w1 = [ffn_w1[l].astype(BF16) for l in range(2)]
    w2 = [ffn_w2[l].astype(BF16) for l in range(2)]
    cwin = conv_w_in[0].astype(BF16)
    cwout = conv_w_out[0].astype(BF16)
    gmix = [row(norm_mix_g[l]) for l in range(2)]
    gffn = [row(norm_ffn_g[l]) for l in range(2)]
    gfin = row(final_norm_g)
    gn, lng, lnb = row(ret_gn_g[0]), row(sgu_ln_g[0]), row(sgu_ln_b[0])
    cbin, cdwb, clng, clnb, cbout = (row(conv_b_in[0]), row(conv_dw_b[0]), row(conv_ln_g[0]),
                                     row(conv_ln_b[0]), row(conv_b_out[0]))

    cos_p, sin_p = _rope_tables(jnp.arange(t))
    dmask, xi, zeta, cd = _decay_tables(CHUNK)
    xi_b = jnp.broadcast_to(xi[:, :, None], (HEADS, CHUNK, DV))
    zeta_b = jnp.broadcast_to(zeta[:, :, None], (HEADS, CHUNK, DK))
    bsgu_b = jnp.broadcast_to(sgu_b[0][:, :CHUNK, None], (GROUPS, CHUNK, GD))
    x1, ret_state_p = _mix_prompt(x_prompt, gmix[0], win, wout, cos_p, sin_p, dmask, xi_b, zeta_b, cd,
                                  gn, lng, lnb, sgu_w[0], bsgu_b)
    x2 = _ffn(x1.reshape(b * t, d), gffn[0], w1[0], w2[0], gfin, False, TM_FFN).reshape(b, t, d)
    x3, conv_state_p = _conv_prompt(x2, gmix[1], cwin, cbin, conv_dw_w[0], cdwb, clng, clnb, cwout, cbout)
    y_prompt = _ffn(x3.reshape(b * t, d), gffn[1], w1[1], w2[1], gfin, True, TM_FFN).reshape(b, t, d)

    xs = x_sample.reshape(m_s, d)
    cos_s, sin_s = _rope_tables(PAST_LEN + jnp.arange(1))
    _, _, _, decay1 = _decay_tables(1)
    sgu_wb = jnp.stack([sgu_w[0][:, 0, 0], sgu_b[0][:, 0]])
    q, k, v, gate, sgu_out, rows = _mix_sample_in(xs, gmix[0], win, cos_s, sin_s, lng, lnb, sgu_wb)
    ret_state_s, o = _ret_sample(decay1, q, k, v, state_ret)
    xs1 = _mix_sample_out(xs, o, gate, sgu_out, gn, wout)
    xs2 = _ffn(xs1, gffn[0], w1[0], w2[0], gfin, False, m_s)
    xg = _conv_sample_in(xs2, gmix[1], cwin, cbin)
    conv_state_s, conv = _conv_sample_state(state_conv, xg.reshape(m_s, 1, d), conv_dw_w[0], cdwb)
    xs3 = _conv_sample_out(xs2, conv.reshape(m_s, d), clng, clnb, cwout, cbout)
    y_sample = _ffn(xs3, gffn[1], w1[1], w2[1], gfin, True, m_s).reshape(m_s, 1, d)

    return (y_prompt, y_sample, ret_state_p, ret_state_s, rows.reshape(1, m_s, 1, U_W),
            conv_state_p, conv_state_s)
```

```python
import functools

import jax
import jax.numpy as jnp
from jax import lax
from jax.experimental import pallas as pl
from jax.experimental.pallas import tpu as pltpu

F32 = jnp.float32
BF16 = jnp.bfloat16

D_MODEL = 1024
HEADS = 4
DK = 128
DV = 256
GROUPS = 4
GD = 256
CHUNK = 128
ROPE_BASE = 10000.0
CONV_WIDTH = 31
HIST = CONV_WIDTH - 1
D_FF = 4 * D_MODEL
RMS_EPS = 1e-6
LN_EPS = 1e-5
PAST_LEN = 16384

Q_W = HEADS * DK
V_W = HEADS * DV
U_W = GROUPS * GD
OFF_V = 2 * Q_W
OFF_G = OFF_V + V_W
OFF_U = OFF_G + V_W
OFF_S = OFF_U + U_W
D_IN = OFF_S + U_W

SUBLANES = 8
VMEM_LIMIT = 56 * 1024 * 1024

TM_MIX = 512
TM_FFN = 512
TM_CONV = 512
CONV_RB = 32
CONV_LC = 256
HIST_PAD = 32
FFN_SLICES = 8
BB_RET = 8
BB_CONV = 16


def _params(*sem):
    return pltpu.CompilerParams(dimension_semantics=sem, vmem_limit_bytes=VMEM_LIMIT)


def _const_spec(shape):
    n = len(shape)
    return pl.BlockSpec(shape, lambda *_: (0,) * n, pipeline_mode=pl.Buffered(1))


def _smem_spec():
    return pl.BlockSpec(memory_space=pltpu.SMEM)


def _rms(x, g):
    ms = jnp.mean(x * x, axis=-1, keepdims=True)
    return x * lax.rsqrt(ms + RMS_EPS) * g


def _layernorm(x, g, b):
    mu = jnp.mean(x, axis=-1, keepdims=True)
    xc = x - mu
    var = jnp.mean(xc * xc, axis=-1, keepdims=True)
    return xc * lax.rsqrt(var + LN_EPS) * g + b


def _groupnorm(x, g):
    mu = jnp.mean(x, axis=-1, keepdims=True)
    xc = x - mu
    var = jnp.mean(xc * xc, axis=-1, keepdims=True)
    return xc * lax.rsqrt(var + LN_EPS) * g


def _sigmoid(x):
    return 1.0 / (1.0 + jnp.exp(-x))


def _silu(x):
    return x * _sigmoid(x)


def _gelu(x):
    return 0.5 * x * (1.0 + lax.erf(x * 0.7071067811865476))


def _dot(a, b):
    return jnp.dot(a, b, preferred_element_type=F32)


def _dot_nt(a, b):
    return lax.dot_general(a, b, (((1,), (1,)), ((), ())), preferred_element_type=F32)


def _dot_tn(a, b):
    return lax.dot_general(a, b, (((0,), (0,)), ((), ())), preferred_element_type=F32)


def _rotary(blk, cos2, sin2):
    return blk * cos2 + pltpu.roll(blk, DK // 2, 1) * sin2


def _mix_prompt_kernel(cd_ref, x_ref, gmix_ref, win_ref, wout_ref, cos_ref, sin_ref,
                       dmask_ref, xi_ref, zeta_ref, gn_ref, lng_ref, lnb_ref,
                       wsgu_ref, bsgu_ref,
                       x1_ref, state_ref,
                       q_s, k_s, v_s, gate_s, u_s, sn_s, mix_s):
    t = pl.program_id(1)
    tm = x_ref.shape[1]

    @pl.when(t == 0)
    def _():
        state_ref[...] = jnp.zeros_like(state_ref)

    x = x_ref[0]
    h = _rms(x, gmix_ref[...]).astype(BF16)

    qk = _dot(h, win_ref[:, 0:OFF_V])
    cos2 = cos_ref[...]
    sin2 = sin_ref[...]
    for j in range(HEADS):
        q_s[:, j * DK:(j + 1) * DK] = _rotary(qk[:, j * DK:(j + 1) * DK], cos2, sin2).astype(BF16)
        kj = qk[:, Q_W + j * DK:Q_W + (j + 1) * DK]
        k_s[:, j * DK:(j + 1) * DK] = _rotary(kj, cos2, sin2) * (DK ** -0.5)
    v_s[...] = _dot(h, win_ref[:, OFF_V:OFF_G]).astype(BF16)
    gate_s[...] = _silu(_dot(h, win_ref[:, OFF_G:OFF_U]))
    u_s[...] = _gelu(_dot(h, win_ref[:, OFF_U:OFF_S]))
    s = _gelu(_dot(h, win_ref[:, OFF_S:D_IN]))
    for g in range(GROUPS):
        gs = slice(g * GD, (g + 1) * GD)
        sn_s[:, gs] = _layernorm(s[:, gs], lng_ref[:, gs], lnb_ref[:, gs]).astype(BF16)

    row = lax.broadcasted_iota(jnp.int32, (CHUNK, CHUNK), 0)
    col = lax.broadcasted_iota(jnp.int32, (CHUNK, CHUNK), 1)
    tril = row >= col

    for c in range(tm // CHUNK):
        rs = slice(c * CHUNK, (c + 1) * CHUNK)
        for hh in range(HEADS):
            ks = slice(hh * DK, (hh + 1) * DK)
            vs = slice(hh * DV, (hh + 1) * DV)
            qc = q_s[rs, ks]
            kf = k_s[rs, ks]
            vc = v_s[rs, vs]
            st = state_ref[0, 0, hh]
            inner = _dot_nt(qc, kf.astype(BF16)) * dmask_ref[hh]
            o = _dot(inner.astype(BF16), vc) + _dot(qc, st.astype(BF16)) * xi_ref[hh]
            kz = (kf * zeta_ref[hh]).astype(BF16)
            state_ref[0, 0, hh] = cd_ref[hh] * st + _dot_tn(kz, vc)
            on = _groupnorm(o, gn_ref[:, vs])
            mix_s[rs, vs] = (gate_s[rs, vs] * on).astype(BF16)
        for g in range(GROUPS):
            gs = slice(g * GD, (g + 1) * GD)
            wg = jnp.where(tril, wsgu_ref[g], 0.0).astype(BF16)
            mixed = _dot(wg, sn_s[rs, gs]) + bsgu_ref[g]
            mix_s[rs, V_W + g * GD:V_W + (g + 1) * GD] = (u_s[rs, gs] * mixed).astype(BF16)

    x1_ref[0] = x + _dot(mix_s[...], wout_ref[...])


def _mix_prompt(x, gmix, win, wout, cos2, sin2, dmask, xi_b, zeta_b, cd, gn, lng, lnb, wsgu, bsgu_b):
    b, t, d = x.shape
    tm = TM_MIX
    grid = (b, t // tm)
    in_specs = [
        _smem_spec(),
        pl.BlockSpec((1, tm, d), lambda i, j: (i, j, 0)),
        _const_spec((1, d)),
        _const_spec((d, D_IN)),
        _const_spec((V_W + U_W, d)),
        pl.BlockSpec((tm, DK), lambda i, j: (j, 0)),
        pl.BlockSpec((tm, DK), lambda i, j: (j, 0)),
        _const_spec((HEADS, CHUNK, CHUNK)),
        _const_spec((HEADS, CHUNK, DV)),
        _const_spec((HEADS, CHUNK, DK)),
        _const_spec((1, V_W)),
        _const_spec((1, U_W)),
        _const_spec((1, U_W)),
        _const_spec((GROUPS, CHUNK, CHUNK)),
        _const_spec((GROUPS, CHUNK, GD)),
    ]
    out_specs = [
        pl.BlockSpec((1, tm, d), lambda i, j: (i, j, 0)),
        pl.BlockSpec((1, 1, HEADS, DK, DV), lambda i, j: (0, i, 0, 0, 0)),
    ]
    out_shape = [
        jax.ShapeDtypeStruct((b, t, d), F32),
        jax.ShapeDtypeStruct((1, b, HEADS, DK, DV), F32),
    ]
    scratch = [
        pltpu.VMEM((tm, Q_W), BF16),
        pltpu.VMEM((tm, Q_W), F32),
        pltpu.VMEM((tm, V_W), BF16),
        pltpu.VMEM((tm, V_W), F32),
        pltpu.VMEM((tm, U_W), F32),
        pltpu.VMEM((tm, U_W), BF16),
        pltpu.VMEM((tm, V_W + U_W), BF16),
    ]
    return pl.pallas_call(
        _mix_prompt_kernel, grid=grid, in_specs=in_specs, out_specs=out_specs,
        out_shape=out_shape, scratch_shapes=scratch,
        compiler_params=_params("arbitrary", "arbitrary"), name="mix_prompt",
    )(cd, x, gmix, win, wout, cos2, sin2, dmask, xi_b, zeta_b, gn, lng, lnb, wsgu, bsgu_b)


def _ffn_kernel(x_ref, g_ref, w1_ref, w2_ref, gfin_ref, o_ref, *, final):
    x = x_ref[...]
    h = _rms(x, g_ref[...]).astype(BF16)
    acc = x
    for j in range(w1_ref.shape[0]):
        f = _dot(h, w1_ref[j])
        a = jnp.square(jnp.maximum(f, 0.0)).astype(BF16)
        acc = acc + _dot(a, w2_ref[j])
    if final:
        acc = _rms(acc, gfin_ref[...])
    o_ref[...] = acc


def _ffn(x, g, w1s, w2s, gfin, final, tm):
    m, d = x.shape
    return pl.pallas_call(
        functools.partial(_ffn_kernel, final=final),
        grid=(m // tm,),
        in_specs=[pl.BlockSpec((tm, d), lambda i: (i, 0)),
                  _const_spec((1, d)), _const_spec(w1s.shape), _const_spec(w2s.shape),
                  _const_spec((1, d))],
        out_specs=pl.BlockSpec((tm, d), lambda i: (i, 0)),
        out_shape=jax.ShapeDtypeStruct((m, d), F32),
        compiler_params=_params("arbitrary"), name="ffn_final" if final else "ffn",
    )(x, g, w1s, w2s, gfin)


def _glu_in(x, g, win_ref, bin_ref):
    h = _rms(x, g).astype(BF16)
    z = _dot(h, win_ref[...]) + bin_ref[...]
    return z[:, :D_MODEL] * _sigmoid(z[:, D_MODEL:])


def _conv_block(r, full_s, conv_s, dww_ref, dwb_ref):
    off0 = HIST_PAD - HIST
    base = pl.multiple_of(r * CONV_RB, CONV_RB)
    for lc in range(D_MODEL // CONV_LC):
        ls = slice(lc * CONV_LC, (lc + 1) * CONV_LC)
        out = dwb_ref[:, ls]
        for p in range(SUBLANES):
            rows = CONV_RB + (SUBLANES if p else 0)
            z = None
            for a in range(HIST_PAD // SUBLANES + 1):
                m = SUBLANES * a + p
                if m < off0 or m > HIST_PAD:
                    continue
                term = dww_ref[m - off0:m - off0 + 1, ls] * full_s[pl.ds(base + SUBLANES * a, rows), ls]
                z = term if z is None else z + term
            out = out + z[p:p + CONV_RB, :]
        conv_s[pl.ds(base, CONV_RB), ls] = out


def _layer1_prompt_kernel(x_ref, gmix_ref, win_ref, bin_ref, dww_ref, dwb_ref, lng_ref, lnb_ref,
                          wout_ref, bout_ref, gffn_ref, w1_ref, w2_ref, gfin_ref,
                          o_ref, hist_ref,
                          full_s, conv_s, xres_s, acc_s, hffn_s, *, n_tiles, tiles_per_seq):
    s = pl.program_id(0)
    tm = x_ref.shape[0]
    off0 = HIST_PAD - HIST
    n_slices = w1_ref.shape[0]
    blocks_per_slice = (tm // CONV_RB) // n_slices

    @pl.when(s > 0)
    def _():
        c = _silu(_layernorm(conv_s[...], lng_ref[...], lnb_ref[...])).astype(BF16)
        x3 = xres_s[...] + _dot(c, wout_ref[...]) + bout_ref[...]
        acc_s[...] = x3
        hffn_s[...] = _rms(x3, gffn_ref[...]).astype(BF16)

    @pl.when(s < n_tiles)
    def _():
        @pl.when(s % tiles_per_seq == 0)
        def _():
            full_s[0:HIST_PAD, :] = jnp.zeros((HIST_PAD, D_MODEL), F32)

        x = x_ref[...]
        xres_s[...] = x
        full_s[HIST_PAD:HIST_PAD + tm, :] = _glu_in(x, gmix_ref[...], win_ref, bin_ref)

    def conv_part(i):
        for q in range(blocks_per_slice):
            _conv_block(i * blocks_per_slice + q, full_s, conv_s, dww_ref, dwb_ref)

    def ffn_part(i):
        f = _dot(hffn_s[...], w1_ref[i])
        a = jnp.square(jnp.maximum(f, 0.0)).astype(BF16)
        acc_s[...] += _dot(a, w2_ref[i])

    def loop(parts):
        def body(i, carry):
            for part in parts:
                part(i)
            return carry
        lax.fori_loop(0, n_slices, body, 0)

    @pl.when(s == 0)
    def _():
        loop((conv_part,))

    @pl.when(jnp.logical_and(s > 0, s < n_tiles))
    def _():
        loop((ffn_part, conv_part))

    @pl.when(s == n_tiles)
    def _():
        loop((ffn_part,))

    @pl.when(s > 0)
    def _():
        o_ref[...] = _rms(acc_s[...], gfin_ref[...])

    @pl.when(s < n_tiles)
    def _():
        @pl.when(s % tiles_per_seq == tiles_per_seq - 1)
        def _():
            hist_ref[0, 0] = full_s[tm + off0:tm + HIST_PAD, :]

        full_s[0:HIST_PAD, :] = full_s[tm:tm + HIST_PAD, :]


def _layer1_prompt(x, gmix, win, bin_, dww, dwb, lng, lnb, wout, bout, gffn, w1s, w2s, gfin):
    b, t, d = x.shape
    tm = TM_CONV
    n_tiles = (b * t) // tm
    tiles_per_seq = t // tm
    n_slices, _, hs = w1s.shape
    in_specs = [
        pl.BlockSpec((tm, d), lambda s: (jnp.minimum(s, n_tiles - 1), 0)),
        _const_spec((1, d)), _const_spec((d, 2 * d)), _const_spec((1, 2 * d)),
        _const_spec((CONV_WIDTH, d)), _const_spec((1, d)), _const_spec((1, d)), _const_spec((1, d)),
        _const_spec((d, d)), _const_spec((1, d)),
        _const_spec((1, d)), _const_spec((n_slices, d, hs)), _const_spec((n_slices, hs, d)),
        _const_spec((1, d)),
    ]
    out_specs = [
        pl.BlockSpec((tm, d), lambda s: (jnp.maximum(s - 1, 0), 0)),
        pl.BlockSpec((1, 1, HIST, d), lambda s: (0, jnp.minimum(s, n_tiles - 1) // tiles_per_seq, 0, 0)),
    ]
    out_shape = [jax.ShapeDtypeStruct((b * t, d), F32),
                 jax.ShapeDtypeStruct((1, b, HIST, d), F32)]
    scratch = [
        pltpu.VMEM((tm + HIST_PAD, d), F32),
        pltpu.VMEM((tm, d), F32),
        pltpu.VMEM((tm, d), F32),
        pltpu.VMEM((tm, d), F32),
        pltpu.VMEM((tm, d), BF16),
    ]
    y, hist = pl.pallas_call(
        functools.partial(_layer1_prompt_kernel, n_tiles=n_tiles, tiles_per_seq=tiles_per_seq),
        grid=(n_tiles + 1,), in_specs=in_specs, out_specs=out_specs,
        out_shape=out_shape, scratch_shapes=scratch,
        compiler_params=_params("arbitrary"), name="layer1_prompt",
    )(x.reshape(b * t, d), gmix, win, bin_, dww, dwb, lng, lnb, wout, bout, gffn, w1s, w2s, gfin)
    return y.reshape(b, t, d), hist


def _mix_sample_in_kernel(sgu_ref, x_ref, gmix_ref, win_ref, cos_ref, sin_ref, lng_ref, lnb_ref,
                          q_ref, k_ref, v_ref, gate_ref, sgu_out_ref, rows_ref):
    h = _rms(x_ref[...], gmix_ref[...]).astype(BF16)
    qk = _dot(h, win_ref[:, 0:OFF_V])
    cos2 = cos_ref[...]
    sin2 = sin_ref[...]
    for j in range(HEADS):
        q_ref[:, j * DK:(j + 1) * DK] = _rotary(qk[:, j * DK:(j + 1) * DK], cos2, sin2)
        kj = qk[:, Q_W + j * DK:Q_W + (j + 1) * DK]
        k_ref[:, j * DK:(j + 1) * DK] = _rotary(kj, cos2, sin2) * (DK ** -0.5)
    v_ref[...] = _dot(h, win_ref[:, OFF_V:OFF_G])
    gate_ref[...] = _silu(_dot(h, win_ref[:, OFF_G:OFF_U]))
    u = _gelu(_dot(h, win_ref[:, OFF_U:OFF_S]))
    s = _gelu(_dot(h, win_ref[:, OFF_S:D_IN]))
    for g in range(GROUPS):
        gs = slice(g * GD, (g + 1) * GD)
        sn = _layernorm(s[:, gs], lng_ref[:, gs], lnb_ref[:, gs])
        rows_ref[:, gs] = sn
        sgu_out_ref[:, gs] = u[:, gs] * (sgu_ref[0, g] * sn + sgu_ref[1, g])


def _mix_sample_in(x, gmix, win, cos2, sin2, lng, lnb, sgu_wb):
    m, d = x.shape
    shp = lambda n: jax.ShapeDtypeStruct((m, n), F32)
    vm = lambda: pl.BlockSpec(memory_space=pltpu.VMEM)
    return pl.pallas_call(
        _mix_sample_in_kernel,
        in_specs=[_smem_spec()] + [vm() for _ in range(7)],
        out_specs=[vm() for _ in range(6)],
        out_shape=[shp(Q_W), shp(Q_W), shp(V_W), shp(V_W), shp(U_W), shp(U_W)],
        compiler_params=pltpu.CompilerParams(vmem_limit_bytes=VMEM_LIMIT), name="mix_sample_in",
    )(sgu_wb, x, gmix, win, cos2, sin2, lng, lnb)


def _ret_sample_kernel(decay_ref, q_ref, k_ref, v_ref, s_ref, snew_ref, o_ref):
    bb = q_ref.shape[0]
    for hh in range(HEADS):
        ks = slice(hh * DK, (hh + 1) * DK)
        vs = slice(hh * DV, (hh + 1) * DV)
        qt = q_ref[:, ks].T
        kt = k_ref[:, ks].T
        gam = decay_ref[hh]
        for b in range(bb):
            st = s_ref[0, b, hh]
            sn = gam * st + kt[:, b:b + 1] * v_ref[b:b + 1, vs]
            snew_ref[0, b, hh] = sn
            o_ref[b:b + 1, vs] = jnp.sum(qt[:, b:b + 1] * sn, axis=0, keepdims=True)


def _ret_sample(decay, q, k, v, state):
    m = q.shape[0]
    bb = BB_RET
    return pl.pallas_call(
        _ret_sample_kernel, grid=(m // bb,),
        in_specs=[_smem_spec(),
                  pl.BlockSpec((bb, Q_W), lambda i: (i, 0)),
                  pl.BlockSpec((bb, Q_W), lambda i: (i, 0)),
                  pl.BlockSpec((bb, V_W), lambda i: (i, 0)),
                  pl.BlockSpec((1, bb, HEADS, DK, DV), lambda i: (0, i, 0, 0, 0))],
        out_specs=[pl.BlockSpec((1, bb, HEADS, DK, DV), lambda i: (0, i, 0, 0, 0)),
                   pl.BlockSpec((bb, V_W), lambda i: (i, 0))],
        out_shape=[jax.ShapeDtypeStruct(state.shape, F32), jax.ShapeDtypeStruct((m, V_W), F32)],
        compiler_params=_params("arbitrary"), name="ret_sample",
    )(decay, q, k, v, state)


def _mix_sample_out_kernel(x_ref, o_ref, gate_ref, sgu_ref, gn_ref, wout_ref, x1_ref):
    o = o_ref[...]
    parts = []
    for hh in range(HEADS):
        vs = slice(hh * DV, (hh + 1) * DV)
        parts.append((gate_ref[:, vs] * _groupnorm(o[:, vs], gn_ref[:, vs])).astype(BF16))
    y = _dot(jnp.concatenate(parts, axis=-1), wout_ref[0:V_W, :])
    y = y + _dot(sgu_ref[...].astype(BF16), wout_ref[V_W:V_W + U_W, :])
    x1_ref[...] = x_ref[...] + y


def _mix_sample_out(x, o, gate, sgu_out, gn, wout):
    m, d = x.shape
    vm = lambda: pl.BlockSpec(memory_space=pltpu.VMEM)
    return pl.pallas_call(
        _mix_sample_out_kernel, in_specs=[vm() for _ in range(6)], out_specs=vm(),
        out_shape=jax.ShapeDtypeStruct((m, d), F32),
        compiler_params=pltpu.CompilerParams(vmem_limit_bytes=VMEM_LIMIT), name="mix_sample_out",
    )(x, o, gate, sgu_out, gn, wout)


def _conv_sample_in_kernel(x_ref, g_ref, win_ref, bin_ref, xg_ref):
    xg_ref[...] = _glu_in(x_ref[...], g_ref[...], win_ref, bin_ref)


def _conv_sample_in(x, g, win, bin_):
    m, d = x.shape
    vm = lambda: pl.BlockSpec(memory_space=pltpu.VMEM)
    return pl.pallas_call(
        _conv_sample_in_kernel, in_specs=[vm() for _ in range(4)], out_specs=vm(),
        out_shape=jax.ShapeDtypeStruct((m, d), F32),
        compiler_params=pltpu.CompilerParams(vmem_limit_bytes=VMEM_LIMIT), name="conv_sample_in",
    )(x, g, win, bin_)


def _conv_sample_state_kernel(hist_ref, xg_ref, dww_ref, dwb_ref, newhist_ref, conv_ref):
    hist = hist_ref[0]
    xg = xg_ref[...]
    acc = jnp.sum(hist * dww_ref[0:HIST, :][None], axis=1, keepdims=True)
    conv_ref[...] = acc + xg * dww_ref[HIST:CONV_WIDTH, :][None] + dwb_ref[...][None]
    newhist_ref[0, :, 0:HIST - 1, :] = hist_ref[0, :, 1:HIST, :]
    newhist_ref[0, :, HIST - 1:HIST, :] = xg


def _conv_sample_state(hist, xg3, dww, dwb):
    _, m, _, d = hist.shape
    bb = BB_CONV
    return pl.pallas_call(
        _conv_sample_state_kernel, grid=(m // bb,),
        in_specs=[pl.BlockSpec((1, bb, HIST, d), lambda i: (0, i, 0, 0)),
                  pl.BlockSpec((bb, 1, d), lambda i: (i, 0, 0)),
                  _const_spec((CONV_WIDTH, d)), _const_spec((1, d))],
        out_specs=[pl.BlockSpec((1, bb, HIST, d), lambda i: (0, i, 0, 0)),
                   pl.BlockSpec((bb, 1, d), lambda i: (i, 0, 0))],
        out_shape=[jax.ShapeDtypeStruct(hist.shape, F32), jax.ShapeDtypeStruct((m, 1, d), F32)],
        compiler_params=_params("arbitrary"), name="conv_sample_state",
    )(hist, xg3, dww, dwb)


def _conv_sample_out_kernel(x_ref, conv_ref, lng_ref, lnb_ref, wout_ref, bout_ref, o_ref):
    c = _silu(_layernorm(conv_ref[...], lng_ref[...], lnb_ref[...])).astype(BF16)
    o_ref[...] = x_ref[...] + _dot(c, wout_ref[...]) + bout_ref[...]


def _conv_sample_out(x, conv, lng, lnb, wout, bout):
    m, d = x.shape
    vm = lambda: pl.BlockSpec(memory_space=pltpu.VMEM)
    return pl.pallas_call(
        _conv_sample_out_kernel, in_specs=[vm() for _ in range(6)], out_specs=vm(),
        out_shape=jax.ShapeDtypeStruct((m, d), F32),
        compiler_params=pltpu.CompilerParams(vmem_limit_bytes=VMEM_LIMIT), name="conv_sample_out",
    )(x, conv, lng, lnb, wout, bout)


def _rope_tables(pos):
    half = DK // 2
    inv = ROPE_BASE ** (-jnp.arange(half, dtype=F32) / half)
    ang = pos.astype(F32)[:, None] * inv[None, :]
    cos = jnp.cos(ang)
    sin = jnp.sin(ang)
    return jnp.concatenate([cos, cos], axis=-1), jnp.concatenate([-sin, sin], axis=-1)


def _decay_tables(c):
    log_g = jnp.log1p(-(2.0 ** (-5.0 - jnp.arange(HEADS, dtype=F32))))
    idx = jnp.arange(c, dtype=F32)
    diff = idx[:, None] - idx[None, :]
    dmask = jnp.where(diff >= 0, jnp.exp(jnp.maximum(diff, 0.0)[None] * log_g[:, None, None]), 0.0)
    xi = jnp.exp((idx[None, :] + 1.0) * log_g[:, None])
    zeta = jnp.exp((c - 1.0 - idx[None, :]) * log_g[:, None])
    chunk_decay = jnp.exp(c * log_g)
    return dmask, xi, zeta, chunk_decay


def kernel(x_prompt, x_sample, state_ret, state_conv, norm_mix_g, norm_ffn_g, final_norm_g, ab_w_in, ab_w_out, ret_gn_g, sgu_ln_g, sgu_ln_b, sgu_w, sgu_b, conv_w_in, conv_b_in, conv_dw_w, conv_dw_b, conv_ln_g, conv_ln_b, conv_w_out, conv_b_out, ffn_w1, ffn_w2):
    b, t, d = x_prompt.shape
    m_s = x_sample.shape[0]
    row = lambda a: a.reshape(1, -1)

    win = ab_w_in[0].astype(BF16)
    wout = ab_w_out[0].astype(BF16)
    hs = D_FF // FFN_SLICES
    w1 = [ffn_w1[l].astype(BF16).reshape(d, FFN_SLICES, hs).transpose(1, 0, 2) for l in range(2)]
    w2 = [ffn_w2[l].astype(BF16).reshape(FFN_SLICES, hs, d) for l in range(2)]
    cwin = conv_w_in[0].astype(BF16)
    cwout = conv_w_out[0].astype(BF16)
    gmix = [row(norm_mix_g[l]) for l in range(2)]
    gffn = [row(norm_ffn_g[l]) for l in range(2)]
    gfin = row(final_norm_g)
    gn, lng, lnb = row(ret_gn_g[0]), row(sgu_ln_g[0]), row(sgu_ln_b[0])
    cbin, cdwb, clng, clnb, cbout = (row(conv_b_in[0]), row(conv_dw_b[0]), row(conv_ln_g[0]),
                                     row(conv_ln_b[0]), row(conv_b_out[0]))

    cos_p, sin_p = _rope_tables(jnp.arange(t))
    dmask, xi, zeta, cd = _decay_tables(CHUNK)
    xi_b = jnp.broadcast_to(xi[:, :, None], (HEADS, CHUNK, DV))
    zeta_b = jnp.broadcast_to(zeta[:, :, None], (HEADS, CHUNK, DK))
    bsgu_b = jnp.broadcast_to(sgu_b[0][:, :CHUNK, None], (GROUPS, CHUNK, GD))
    x1, ret_state_p = _mix_prompt(x_prompt, gmix[0], win, wout, cos_p, sin_p, dmask, xi_b, zeta_b, cd,
                                  gn, lng, lnb, sgu_w[0], bsgu_b)
    x2 = _ffn(x1.reshape(b * t, d), gffn[0], w1[0], w2[0], gfin, False, TM_FFN).reshape(b, t, d)
    y_prompt, conv_state_p = _layer1_prompt(x2, gmix[1], cwin, cbin, conv_dw_w[0], cdwb, clng, clnb, cwout,
                                            cbout, gffn[1], w1[1], w2[1], gfin)

    xs = x_sample.reshape(m_s, d)
    cos_s, sin_s = _rope_tables(PAST_LEN + jnp.arange(1))
    _, _, _, decay1 = _decay_tables(1)
    sgu_wb = jnp.stack([sgu_w[0][:, 0, 0], sgu_b[0][:, 0]])
    q, k, v, gate, sgu_out, rows = _mix_sample_in(xs, gmix[0], win, cos_s, sin_s, lng, lnb, sgu_wb)
    ret_state_s, o = _ret_sample(decay1, q, k, v, state_ret)
    xs1 = _mix_sample_out(xs, o, gate, sgu_out, gn, wout)
    xs2 = _ffn(xs1, gffn[0], w1[0], w2[0], gfin, False, m_s)
    xg = _conv_sample_in(xs2, gmix[1], cwin, cbin)
    conv_state_s, conv = _conv_sample_state(state_conv, xg.reshape(m_s, 1, d), conv_dw_w[0], cdwb)
    xs3 = _conv_sample_out(xs2, conv.reshape(m_s, d), clng, clnb, cwout, cbout)
    y_sample = _ffn(xs3, gffn[1], w1[1], w2[1], gfin, True, m_s).reshape(m_s, 1, d)

    return (y_prompt, y_sample, ret_state_p, ret_state_s, rows.reshape(1, m_s, 1, U_W),
            conv_state_p, conv_state_s)
```

```python
import functools

import jax
import jax.numpy as jnp
from jax import lax
from jax.experimental import pallas as pl
from jax.experimental.pallas import tpu as pltpu

F32 = jnp.float32
BF16 = jnp.bfloat16

D_MODEL = 1024
HEADS = 4
DK = 128
DV = 256
GROUPS = 4
GD = 256
CHUNK = 128
ROPE_BASE = 10000.0
CONV_WIDTH = 31
HIST = CONV_WIDTH - 1
D_FF = 4 * D_MODEL
RMS_EPS = 1e-6
LN_EPS = 1e-5
PAST_LEN = 16384

Q_W = HEADS * DK
V_W = HEADS * DV
U_W = GROUPS * GD
OFF_V = 2 * Q_W
OFF_G = OFF_V + V_W
OFF_U = OFF_G + V_W
OFF_S = OFF_U + U_W
D_IN = OFF_S + U_W

SUBLANES = 8
VMEM_LIMIT = 56 * 1024 * 1024

TM_MIX = 512
TM_FFN = 512
TM_CONV = 512
CONV_RB = 64
CONV_LC = 128
HIST_PAD = 32
FFN_SLICES = 4
BB_RET = 8
BB_CONV = 16


def _params(*sem):
    return pltpu.CompilerParams(dimension_semantics=sem, vmem_limit_bytes=VMEM_LIMIT)


def _const_spec(shape):
    n = len(shape)
    return pl.BlockSpec(shape, lambda *_: (0,) * n, pipeline_mode=pl.Buffered(1))


def _smem_spec():
    return pl.BlockSpec(memory_space=pltpu.SMEM)


def _rms(x, g):
    ms = jnp.mean(x * x, axis=-1, keepdims=True)
    return x * lax.rsqrt(ms + RMS_EPS) * g


def _layernorm(x, g, b):
    mu = jnp.mean(x, axis=-1, keepdims=True)
    xc = x - mu
    var = jnp.mean(xc * xc, axis=-1, keepdims=True)
    return xc * lax.rsqrt(var + LN_EPS) * g + b


def _groupnorm(x, g):
    mu = jnp.mean(x, axis=-1, keepdims=True)
    xc = x - mu
    var = jnp.mean(xc * xc, axis=-1, keepdims=True)
    return xc * lax.rsqrt(var + LN_EPS) * g


def _sigmoid(x):
    return 1.0 / (1.0 + jnp.exp(-x))


def _silu(x):
    return x * _sigmoid(x)


def _gelu(x):
    return 0.5 * x * (1.0 + lax.erf(x * 0.7071067811865476))


def _dot(a, b):
    return jnp.dot(a, b, preferred_element_type=F32)


def _dot_nt(a, b):
    return lax.dot_general(a, b, (((1,), (1,)), ((), ())), preferred_element_type=F32)


def _dot_tn(a, b):
    return lax.dot_general(a, b, (((0,), (0,)), ((), ())), preferred_element_type=F32)


def _rotary(blk, cos2, sin2):
    return blk * cos2 + pltpu.roll(blk, DK // 2, 1) * sin2


def _mix_prompt_kernel(cd_ref, x_ref, gmix_ref, win_ref, wout_ref, cos_ref, sin_ref,
                       dmask_ref, xi_ref, zeta_ref, gn_ref, lng_ref, lnb_ref,
                       wsgu_ref, bsgu_ref,
                       x1_ref, state_ref,
                       q_s, k_s, v_s, gate_s, u_s, sn_s, mix_s):
    t = pl.program_id(1)
    tm = x_ref.shape[1]

    @pl.when(t == 0)
    def _():
        state_ref[...] = jnp.zeros_like(state_ref)

    x = x_ref[0]
    h = _rms(x, gmix_ref[...]).astype(BF16)

    qk = _dot(h, win_ref[:, 0:OFF_V])
    cos2 = cos_ref[...]
    sin2 = sin_ref[...]
    for j in range(HEADS):
        q_s[:, j * DK:(j + 1) * DK] = _rotary(qk[:, j * DK:(j + 1) * DK], cos2, sin2).astype(BF16)
        kj = qk[:, Q_W + j * DK:Q_W + (j + 1) * DK]
        k_s[:, j * DK:(j + 1) * DK] = _rotary(kj, cos2, sin2) * (DK ** -0.5)
    v_s[...] = _dot(h, win_ref[:, OFF_V:OFF_G]).astype(BF16)
    gate_s[...] = _silu(_dot(h, win_ref[:, OFF_G:OFF_U]))
    u_s[...] = _gelu(_dot(h, win_ref[:, OFF_U:OFF_S]))
    s = _gelu(_dot(h, win_ref[:, OFF_S:D_IN]))
    for g in range(GROUPS):
        gs = slice(g * GD, (g + 1) * GD)
        sn_s[:, gs] = _layernorm(s[:, gs], lng_ref[:, gs], lnb_ref[:, gs]).astype(BF16)

    row = lax.broadcasted_iota(jnp.int32, (CHUNK, CHUNK), 0)
    col = lax.broadcasted_iota(jnp.int32, (CHUNK, CHUNK), 1)
    tril = row >= col

    for c in range(tm // CHUNK):
        rs = slice(c * CHUNK, (c + 1) * CHUNK)
        for hh in range(HEADS):
            ks = slice(hh * DK, (hh + 1) * DK)
            vs = slice(hh * DV, (hh + 1) * DV)
            qc = q_s[rs, ks]
            kf = k_s[rs, ks]
            vc = v_s[rs, vs]
            st = state_ref[0, 0, hh]
            inner = _dot_nt(qc, kf.astype(BF16)) * dmask_ref[hh]
            o = _dot(inner.astype(BF16), vc) + _dot(qc, st.astype(BF16)) * xi_ref[hh]
            kz = (kf * zeta_ref[hh]).astype(BF16)
            state_ref[0, 0, hh] = cd_ref[hh] * st + _dot_tn(kz, vc)
            on = _groupnorm(o, gn_ref[:, vs])
            mix_s[rs, vs] = (gate_s[rs, vs] * on).astype(BF16)
        for g in range(GROUPS):
            gs = slice(g * GD, (g + 1) * GD)
            wg = jnp.where(tril, wsgu_ref[g], 0.0).astype(BF16)
            mixed = _dot(wg, sn_s[rs, gs]) + bsgu_ref[g]
            mix_s[rs, V_W + g * GD:V_W + (g + 1) * GD] = (u_s[rs, gs] * mixed).astype(BF16)

    x1_ref[0] = x + _dot(mix_s[...], wout_ref[...])


def _mix_prompt(x, gmix, win, wout, cos2, sin2, dmask, xi_b, zeta_b, cd, gn, lng, lnb, wsgu, bsgu_b):
    b, t, d = x.shape
    tm = TM_MIX
    grid = (b, t // tm)
    in_specs = [
        _smem_spec(),
        pl.BlockSpec((1, tm, d), lambda i, j: (i, j, 0)),
        _const_spec((1, d)),
        _const_spec((d, D_IN)),
        _const_spec((V_W + U_W, d)),
        pl.BlockSpec((tm, DK), lambda i, j: (j, 0)),
        pl.BlockSpec((tm, DK), lambda i, j: (j, 0)),
        _const_spec((HEADS, CHUNK, CHUNK)),
        _const_spec((HEADS, CHUNK, DV)),
        _const_spec((HEADS, CHUNK, DK)),
        _const_spec((1, V_W)),
        _const_spec((1, U_W)),
        _const_spec((1, U_W)),
        _const_spec((GROUPS, CHUNK, CHUNK)),
        _const_spec((GROUPS, CHUNK, GD)),
    ]
    out_specs = [
        pl.BlockSpec((1, tm, d), lambda i, j: (i, j, 0)),
        pl.BlockSpec((1, 1, HEADS, DK, DV), lambda i, j: (0, i, 0, 0, 0)),
    ]
    out_shape = [
        jax.ShapeDtypeStruct((b, t, d), F32),
        jax.ShapeDtypeStruct((1, b, HEADS, DK, DV), F32),
    ]
    scratch = [
        pltpu.VMEM((tm, Q_W), BF16),
        pltpu.VMEM((tm, Q_W), F32),
        pltpu.VMEM((tm, V_W), BF16),
        pltpu.VMEM((tm, V_W), F32),
        pltpu.VMEM((tm, U_W), F32),
        pltpu.VMEM((tm, U_W), BF16),
        pltpu.VMEM((tm, V_W + U_W), BF16),
    ]
    return pl.pallas_call(
        _mix_prompt_kernel, grid=grid, in_specs=in_specs, out_specs=out_specs,
        out_shape=out_shape, scratch_shapes=scratch,
        compiler_params=_params("arbitrary", "arbitrary"), name="mix_prompt",
    )(cd, x, gmix, win, wout, cos2, sin2, dmask, xi_b, zeta_b, gn, lng, lnb, wsgu, bsgu_b)


def _ffn_kernel(x_ref, g_ref, w1_ref, w2_ref, gfin_ref, o_ref, *, final):
    x = x_ref[...]
    h = _rms(x, g_ref[...]).astype(BF16)
    acc = x
    for j in range(w1_ref.shape[0]):
        f = _dot(h, w1_ref[j])
        a = jnp.square(jnp.maximum(f, 0.0)).astype(BF16)
        acc = acc + _dot(a, w2_ref[j])
    if final:
        acc = _rms(acc, gfin_ref[...])
    o_ref[...] = acc


def _ffn(x, g, w1s, w2s, gfin, final, tm):
    m, d = x.shape
    return pl.pallas_call(
        functools.partial(_ffn_kernel, final=final),
        grid=(m // tm,),
        in_specs=[pl.BlockSpec((tm, d), lambda i: (i, 0)),
                  _const_spec((1, d)), _const_spec(w1s.shape), _const_spec(w2s.shape),
                  _const_spec((1, d))],
        out_specs=pl.BlockSpec((tm, d), lambda i: (i, 0)),
        out_shape=jax.ShapeDtypeStruct((m, d), F32),
        compiler_params=_params("arbitrary"), name="ffn_final" if final else "ffn",
    )(x, g, w1s, w2s, gfin)


def _glu_in(x, g, win_ref, bin_ref):
    h = _rms(x, g).astype(BF16)
    z = _dot(h, win_ref[...]) + bin_ref[...]
    return z[:, :D_MODEL] * _sigmoid(z[:, D_MODEL:])


def _conv_block(r, full_s, conv_s, dww_ref, dwb_ref):
    off0 = HIST_PAD - HIST
    base = pl.multiple_of(r * CONV_RB, CONV_RB)
    for lc in range(D_MODEL // CONV_LC):
        ls = slice(lc * CONV_LC, (lc + 1) * CONV_LC)
        win = full_s[pl.ds(base, CONV_RB + HIST_PAD), ls]
        out = dwb_ref[:, ls]
        for p in range(SUBLANES):
            rows = CONV_RB + (SUBLANES if p else 0)
            z = None
            for a in range(HIST_PAD // SUBLANES + 1):
                m = SUBLANES * a + p
                if m < off0 or m > HIST_PAD:
                    continue
                term = dww_ref[m - off0:m - off0 + 1, ls] * win[SUBLANES * a:SUBLANES * a + rows, :]
                z = term if z is None else z + term
            out = out + z[p:p + CONV_RB, :]
        conv_s[pl.ds(base, CONV_RB), ls] = out


def _conv_prompt_kernel(x_ref, xprev_ref, gmix_ref, win_ref, bin_ref, dww_ref, dwb_ref, lng_ref, lnb_ref,
                        wout_ref, bout_ref, o_ref, hist_ref, full_s, conv_s, *, n_tiles, tiles_per_seq):
    s = pl.program_id(0)
    tm = x_ref.shape[0]
    off0 = HIST_PAD - HIST

    def front():
        full_s[HIST_PAD:HIST_PAD + tm, :] = _glu_in(x_ref[...], gmix_ref[...], win_ref, bin_ref)

    def back():
        c = _silu(_layernorm(conv_s[...], lng_ref[...], lnb_ref[...])).astype(BF16)
        o_ref[...] = xprev_ref[...] + _dot(c, wout_ref[...]) + bout_ref[...]

    @pl.when(s % tiles_per_seq == 0)
    def _():
        full_s[0:HIST_PAD, :] = jnp.zeros((HIST_PAD, D_MODEL), F32)

    @pl.when(s == 0)
    def _():
        front()

    @pl.when(jnp.logical_and(s > 0, s < n_tiles))
    def _():
        front()
        back()

    @pl.when(s == n_tiles)
    def _():
        back()

    @pl.when(s < n_tiles)
    def _():
        def body(r, carry):
            _conv_block(r, full_s, conv_s, dww_ref, dwb_ref)
            return carry
        lax.fori_loop(0, tm // CONV_RB, body, 0)

        @pl.when(s % tiles_per_seq == tiles_per_seq - 1)
        def _():
            hist_ref[0, 0] = full_s[tm + off0:tm + HIST_PAD, :]

        full_s[0:HIST_PAD, :] = full_s[tm:tm + HIST_PAD, :]


def _conv_prompt(x, gmix, win, bin_, dww, dwb, lng, lnb, wout, bout):
    b, t, d = x.shape
    tm = TM_CONV
    n_tiles = (b * t) // tm
    tiles_per_seq = t // tm
    in_specs = [
        pl.BlockSpec((tm, d), lambda s: (jnp.minimum(s, n_tiles - 1), 0)),
        pl.BlockSpec((tm, d), lambda s: (jnp.maximum(s - 1, 0), 0)),
        _const_spec((1, d)), _const_spec((d, 2 * d)), _const_spec((1, 2 * d)),
        _const_spec((CONV_WIDTH, d)), _const_spec((1, d)), _const_spec((1, d)), _const_spec((1, d)),
        _const_spec((d, d)), _const_spec((1, d)),
    ]
    out_specs = [
        pl.BlockSpec((tm, d), lambda s: (jnp.maximum(s - 1, 0), 0)),
        pl.BlockSpec((1, 1, HIST, d), lambda s: (0, jnp.minimum(s, n_tiles - 1) // tiles_per_seq, 0, 0)),
    ]
    out_shape = [jax.ShapeDtypeStruct((b * t, d), F32),
                 jax.ShapeDtypeStruct((1, b, HIST, d), F32)]
    scratch = [
        pltpu.VMEM((tm + HIST_PAD, d), F32),
        pltpu.VMEM((tm, d), F32),
    ]
    x2d = x.reshape(b * t, d)
    y, hist = pl.pallas_call(
        functools.partial(_conv_prompt_kernel, n_tiles=n_tiles, tiles_per_seq=tiles_per_seq),
        grid=(n_tiles + 1,), in_specs=in_specs, out_specs=out_specs,
        out_shape=out_shape, scratch_shapes=scratch,
        compiler_params=_params("arbitrary"), name="conv_prompt",
    )(x2d, x2d, gmix, win, bin_, dww, dwb, lng, lnb, wout, bout)
    return y, hist


def _mix_sample_in_kernel(sgu_ref, x_ref, gmix_ref, win_ref, cos_ref, sin_ref, lng_ref, lnb_ref,
                          q_ref, k_ref, v_ref, gate_ref, sgu_out_ref, rows_ref):
    h = _rms(x_ref[...], gmix_ref[...]).astype(BF16)
    qk = _dot(h, win_ref[:, 0:OFF_V])
    cos2 = cos_ref[...]
    sin2 = sin_ref[...]
    for j in range(HEADS):
        q_ref[:, j * DK:(j + 1) * DK] = _rotary(qk[:, j * DK:(j + 1) * DK], cos2, sin2)
        kj = qk[:, Q_W + j * DK:Q_W + (j + 1) * DK]
        k_ref[:, j * DK:(j + 1) * DK] = _rotary(kj, cos2, sin2) * (DK ** -0.5)
    v_ref[...] = _dot(h, win_ref[:, OFF_V:OFF_G])
    gate_ref[...] = _silu(_dot(h, win_ref[:, OFF_G:OFF_U]))
    u = _gelu(_dot(h, win_ref[:, OFF_U:OFF_S]))
    s = _gelu(_dot(h, win_ref[:, OFF_S:D_IN]))
    for g in range(GROUPS):
        gs = slice(g * GD, (g + 1) * GD)
        sn = _layernorm(s[:, gs], lng_ref[:, gs], lnb_ref[:, gs])
        rows_ref[:, gs] = sn
        sgu_out_ref[:, gs] = u[:, gs] * (sgu_ref[0, g] * sn + sgu_ref[1, g])


def _mix_sample_in(x, gmix, win, cos2, sin2, lng, lnb, sgu_wb):
    m, d = x.shape
    shp = lambda n: jax.ShapeDtypeStruct((m, n), F32)
    vm = lambda: pl.BlockSpec(memory_space=pltpu.VMEM)
    return pl.pallas_call(
        _mix_sample_in_kernel,
        in_specs=[_smem_spec()] + [vm() for _ in range(7)],
        out_specs=[vm() for _ in range(6)],
        out_shape=[shp(Q_W), shp(Q_W), shp(V_W), shp(V_W), shp(U_W), shp(U_W)],
        compiler_params=pltpu.CompilerParams(vmem_limit_bytes=VMEM_LIMIT), name="mix_sample_in",
    )(sgu_wb, x, gmix, win, cos2, sin2, lng, lnb)


def _ret_sample_kernel(decay_ref, q_ref, k_ref, v_ref, s_ref, snew_ref, o_ref):
    bb = q_ref.shape[0]
    for hh in range(HEADS):
        ks = slice(hh * DK, (hh + 1) * DK)
        vs = slice(hh * DV, (hh + 1) * DV)
        qt = q_ref[:, ks].T
        kt = k_ref[:, ks].T
        gam = decay_ref[hh]
        for b in range(bb):
            st = s_ref[0, b, hh]
            sn = gam * st + kt[:, b:b + 1] * v_ref[b:b + 1, vs]
            snew_ref[0, b, hh] = sn
            o_ref[b:b + 1, vs] = jnp.sum(qt[:, b:b + 1] * sn, axis=0, keepdims=True)


def _ret_sample(decay, q, k, v, state):
    m = q.shape[0]
    bb = BB_RET
    return pl.pallas_call(
        _ret_sample_kernel, grid=(m // bb,),
        in_specs=[_smem_spec(),
                  pl.BlockSpec((bb, Q_W), lambda i: (i, 0)),
                  pl.BlockSpec((bb, Q_W), lambda i: (i, 0)),
                  pl.BlockSpec((bb, V_W), lambda i: (i, 0)),
                  pl.BlockSpec((1, bb, HEADS, DK, DV), lambda i: (0, i, 0, 0, 0))],
        out_specs=[pl.BlockSpec((1, bb, HEADS, DK, DV), lambda i: (0, i, 0, 0, 0)),
                   pl.BlockSpec((bb, V_W), lambda i: (i, 0))],
        out_shape=[jax.ShapeDtypeStruct(state.shape, F32), jax.ShapeDtypeStruct((m, V_W), F32)],
        compiler_params=_params("arbitrary"), name="ret_sample",
    )(decay, q, k, v, state)


def _mix_sample_out_kernel(x_ref, o_ref, gate_ref, sgu_ref, gn_ref, wout_ref, x1_ref):
    o = o_ref[...]
    parts = []
    for hh in range(HEADS):
        vs = slice(hh * DV, (hh + 1) * DV)
        parts.append((gate_ref[:, vs] * _groupnorm(o[:, vs], gn_ref[:, vs])).astype(BF16))
    y = _dot(jnp.concatenate(parts, axis=-1), wout_ref[0:V_W, :])
    y = y + _dot(sgu_ref[...].astype(BF16), wout_ref[V_W:V_W + U_W, :])
    x1_ref[...] = x_ref[...] + y


def _mix_sample_out(x, o, gate, sgu_out, gn, wout):
    m, d = x.shape
    vm = lambda: pl.BlockSpec(memory_space=pltpu.VMEM)
    return pl.pallas_call(
        _mix_sample_out_kernel, in_specs=[vm() for _ in range(6)], out_specs=vm(),
        out_shape=jax.ShapeDtypeStruct((m, d), F32),
        compiler_params=pltpu.CompilerParams(vmem_limit_bytes=VMEM_LIMIT), name="mix_sample_out",
    )(x, o, gate, sgu_out, gn, wout)


def _conv_sample_in_kernel(x_ref, g_ref, win_ref, bin_ref, xg_ref):
    xg_ref[...] = _glu_in(x_ref[...], g_ref[...], win_ref, bin_ref)


def _conv_sample_in(x, g, win, bin_):
    m, d = x.shape
    vm = lambda: pl.BlockSpec(memory_space=pltpu.VMEM)
    return pl.pallas_call(
        _conv_sample_in_kernel, in_specs=[vm() for _ in range(4)], out_specs=vm(),
        out_shape=jax.ShapeDtypeStruct((m, d), F32),
        compiler_params=pltpu.CompilerParams(vmem_limit_bytes=VMEM_LIMIT), name="conv_sample_in",
    )(x, g, win, bin_)


def _conv_sample_state_kernel(hist_ref, xg_ref, dww_ref, dwb_ref, newhist_ref, conv_ref):
    xg = xg_ref[...]
    acc = dwb_ref[...] + xg * dww_ref[HIST:CONV_WIDTH, :]
    for j in range(HIST):
        acc = acc + hist_ref[j] * dww_ref[j:j + 1, :]
    conv_ref[...] = acc
    for j in range(HIST - 1):
        newhist_ref[j] = hist_ref[j + 1]
    newhist_ref[HIST - 1] = xg


def _conv_sample_state(hist_t, xg, dww, dwb):
    _, m, d = hist_t.shape
    bb = BB_CONV
    return pl.pallas_call(
        _conv_sample_state_kernel, grid=(m // bb,),
        in_specs=[pl.BlockSpec((HIST, bb, d), lambda i: (0, i, 0)),
                  pl.BlockSpec((bb, d), lambda i: (i, 0)),
                  _const_spec((CONV_WIDTH, d)), _const_spec((1, d))],
        out_specs=[pl.BlockSpec((HIST, bb, d), lambda i: (0, i, 0)),
                   pl.BlockSpec((bb, d), lambda i: (i, 0))],
        out_shape=[jax.ShapeDtypeStruct(hist_t.shape, F32), jax.ShapeDtypeStruct((m, d), F32)],
        compiler_params=_params("arbitrary"), name="conv_sample_state",
    )(hist_t, xg, dww, dwb)


def _conv_sample_out_kernel(x_ref, conv_ref, lng_ref, lnb_ref, wout_ref, bout_ref, o_ref):
    c = _silu(_layernorm(conv_ref[...], lng_ref[...], lnb_ref[...])).astype(BF16)
    o_ref[...] = x_ref[...] + _dot(c, wout_ref[...]) + bout_ref[...]


def _conv_sample_out(x, conv, lng, lnb, wout, bout):
    m, d = x.shape
    vm = lambda: pl.BlockSpec(memory_space=pltpu.VMEM)
    return pl.pallas_call(
        _conv_sample_out_kernel, in_specs=[vm() for _ in range(6)], out_specs=vm(),
        out_shape=jax.ShapeDtypeStruct((m, d), F32),
        compiler_params=pltpu.CompilerParams(vmem_limit_bytes=VMEM_LIMIT), name="conv_sample_out",
    )(x, conv, lng, lnb, wout, bout)


def _rope_tables(pos):
    half = DK // 2
    inv = ROPE_BASE ** (-jnp.arange(half, dtype=F32) / half)
    ang = pos.astype(F32)[:, None] * inv[None, :]
    cos = jnp.cos(ang)
    sin = jnp.sin(ang)
    return jnp.concatenate([cos, cos], axis=-1), jnp.concatenate([-sin, sin], axis=-1)


def _decay_tables(c):
    log_g = jnp.log1p(-(2.0 ** (-5.0 - jnp.arange(HEADS, dtype=F32))))
    idx = jnp.arange(c, dtype=F32)
    diff = idx[:, None] - idx[None, :]
    dmask = jnp.where(diff >= 0, jnp.exp(jnp.maximum(diff, 0.0)[None] * log_g[:, None, None]), 0.0)
    xi = jnp.exp((idx[None, :] + 1.0) * log_g[:, None])
    zeta = jnp.exp((c - 1.0 - idx[None, :]) * log_g[:, None])
    chunk_decay = jnp.exp(c * log_g)
    return dmask, xi, zeta, chunk_decay


def kernel(x_prompt, x_sample, state_ret, state_conv, norm_mix_g, norm_ffn_g, final_norm_g, ab_w_in, ab_w_out, ret_gn_g, sgu_ln_g, sgu_ln_b, sgu_w, sgu_b, conv_w_in, conv_b_in, conv_dw_w, conv_dw_b, conv_ln_g, conv_ln_b, conv_w_out, conv_b_out, ffn_w1, ffn_w2):
    b, t, d = x_prompt.shape
    m_s = x_sample.shape[0]
    row = lambda a: a.reshape(1, -1)

    win = ab_w_in[0].astype(BF16)
    wout = ab_w_out[0].astype(BF16)
    hs = D_FF // FFN_SLICES
    w1 = [ffn_w1[l].astype(BF16).reshape(d, FFN_SLICES, hs).transpose(1, 0, 2) for l in range(2)]
    w2 = [ffn_w2[l].astype(BF16).reshape(FFN_SLICES, hs, d) for l in range(2)]
    cwin = conv_w_in[0].astype(BF16)
    cwout = conv_w_out[0].astype(BF16)
    gmix = [row(norm_mix_g[l]) for l in range(2)]
    gffn = [row(norm_ffn_g[l]) for l in range(2)]
    gfin = row(final_norm_g)
    gn, lng, lnb = row(ret_gn_g[0]), row(sgu_ln_g[0]), row(sgu_ln_b[0])
    cbin, cdwb, clng, clnb, cbout = (row(conv_b_in[0]), row(conv_dw_b[0]), row(conv_ln_g[0]),
                                     row(conv_ln_b[0]), row(conv_b_out[0]))

    cos_p, sin_p = _rope_tables(jnp.arange(t))
    dmask, xi, zeta, cd = _decay_tables(CHUNK)
    xi_b = jnp.broadcast_to(xi[:, :, None], (HEADS, CHUNK, DV))
    zeta_b = jnp.broadcast_to(zeta[:, :, None], (HEADS, CHUNK, DK))
    bsgu_b = jnp.broadcast_to(sgu_b[0][:, :CHUNK, None], (GROUPS, CHUNK, GD))
    x1, ret_state_p = _mix_prompt(x_prompt, gmix[0], win, wout, cos_p, sin_p, dmask, xi_b, zeta_b, cd,
                                  gn, lng, lnb, sgu_w[0], bsgu_b)
    x2 = _ffn(x1.reshape(b * t, d), gffn[0], w1[0], w2[0], gfin, False, TM_FFN).reshape(b, t, d)
    x3, conv_state_p = _conv_prompt(x2, gmix[1], cwin, cbin, conv_dw_w[0], cdwb, clng, clnb, cwout, cbout)
    y_prompt = _ffn(x3, gffn[1], w1[1], w2[1], gfin, True, TM_FFN).reshape(b, t, d)

    xs = x_sample.reshape(m_s, d)
    cos_s, sin_s = _rope_tables(PAST_LEN + jnp.arange(1))
    _, _, _, decay1 = _decay_tables(1)
    sgu_wb = jnp.stack([sgu_w[0][:, 0, 0], sgu_b[0][:, 0]])
    q, k, v, gate, sgu_out, rows = _mix_sample_in(xs, gmix[0], win, cos_s, sin_s, lng, lnb, sgu_wb)
    ret_state_s, o = _ret_sample(decay1, q, k, v, state_ret)
    xs1 = _mix_sample_out(xs, o, gate, sgu_out, gn, wout)
    xs2 = _ffn(xs1, gffn[0], w1[0], w2[0], gfin, False, m_s)
    xg = _conv_sample_in(xs2, gmix[1], cwin, cbin)
    hist_t = jnp.transpose(state_conv[0], (1, 0, 2))
    newhist_t, conv = _conv_sample_state(hist_t, xg, conv_dw_w[0], cdwb)
    conv_state_s = jnp.transpose(newhist_t, (1, 0, 2))[None]
    xs3 = _conv_sample_out(xs2, conv, clng, clnb, cwout, cbout)
    y_sample = _ffn(xs3, gffn[1], w1[1], w2[1], gfin, True, m_s).reshape(m_s, 1, d)

    return (y_prompt, y_sample, ret_state_p, ret_state_s, rows.reshape(1, m_s, 1, U_W),
            conv_state_p, conv_state_s)
```

```python
import functools

import jax
import jax.numpy as jnp
from jax import lax
from jax.experimental import pallas as pl
from jax.experimental.pallas import tpu as pltpu

F32 = jnp.float32
BF16 = jnp.bfloat16

D_MODEL = 1024
HEADS = 4
DK = 128
DV = 256
GROUPS = 4
GD = 256
CHUNK = 128
ROPE_BASE = 10000.0
CONV_WIDTH = 31
HIST = CONV_WIDTH - 1
D_FF = 4 * D_MODEL
RMS_EPS = 1e-6
LN_EPS = 1e-5
PAST_LEN = 16384

Q_W = HEADS * DK
V_W = HEADS * DV
U_W = GROUPS * GD
OFF_V = 2 * Q_W
OFF_G = OFF_V + V_W
OFF_U = OFF_G + V_W
OFF_S = OFF_U + U_W
D_IN = OFF_S + U_W

SUBLANES = 8
VMEM_LIMIT = 56 * 1024 * 1024

TM_MIX = 512
TM_FFN = 512
TM_CONV = 512
CONV_RB = 64
CONV_LC = 128
HIST_PAD = 32
FFN_SLICES = 4
FFN_HS = D_FF // FFN_SLICES
BB_RET = 8
BB_CONV = 16


def _params(*sem):
    return pltpu.CompilerParams(dimension_semantics=sem, vmem_limit_bytes=VMEM_LIMIT)


def _const_spec(shape):
    n = len(shape)
    return pl.BlockSpec(shape, lambda *_: (0,) * n, pipeline_mode=pl.Buffered(1))


def _smem_spec():
    return pl.BlockSpec(memory_space=pltpu.SMEM)


def _rms(x, g):
    ms = jnp.mean(x * x, axis=-1, keepdims=True)
    return x * lax.rsqrt(ms + RMS_EPS) * g


def _layernorm(x, g, b):
    mu = jnp.mean(x, axis=-1, keepdims=True)
    xc = x - mu
    var = jnp.mean(xc * xc, axis=-1, keepdims=True)
    return xc * lax.rsqrt(var + LN_EPS) * g + b


def _groupnorm(x, g):
    mu = jnp.mean(x, axis=-1, keepdims=True)
    xc = x - mu
    var = jnp.mean(xc * xc, axis=-1, keepdims=True)
    return xc * lax.rsqrt(var + LN_EPS) * g


def _sigmoid(x):
    return 1.0 / (1.0 + jnp.exp(-x))


def _silu(x):
    return x * _sigmoid(x)


def _gelu(x):
    return 0.5 * x * (1.0 + lax.erf(x * 0.7071067811865476))


def _dot(a, b):
    return jnp.dot(a, b, preferred_element_type=F32)


def _dot_nt(a, b):
    return lax.dot_general(a, b, (((1,), (1,)), ((), ())), preferred_element_type=F32)


def _dot_tn(a, b):
    return lax.dot_general(a, b, (((0,), (0,)), ((), ())), preferred_element_type=F32)


def _rotary(blk, cos2, sin2):
    return blk * cos2 + pltpu.roll(blk, DK // 2, 1) * sin2


def _mix_prompt_kernel(cd_ref, x_ref, gmix_ref, win_ref, wout_ref, cos_ref, sin_ref,
                       dmask_ref, xi_ref, zeta_ref, gn_ref, lng_ref, lnb_ref,
                       wsgu_ref, bsgu_ref,
                       x1_ref, state_ref,
                       q_s, k_s, v_s, gate_s, u_s, sn_s, mix_s):
    t = pl.program_id(1)
    tm = x_ref.shape[1]

    @pl.when(t == 0)
    def _():
        state_ref[...] = jnp.zeros_like(state_ref)

    x = x_ref[0]
    h = _rms(x, gmix_ref[...]).astype(BF16)

    qk = _dot(h, win_ref[:, 0:OFF_V])
    cos2 = cos_ref[...]
    sin2 = sin_ref[...]
    for j in range(HEADS):
        q_s[:, j * DK:(j + 1) * DK] = _rotary(qk[:, j * DK:(j + 1) * DK], cos2, sin2).astype(BF16)
        kj = qk[:, Q_W + j * DK:Q_W + (j + 1) * DK]
        k_s[:, j * DK:(j + 1) * DK] = _rotary(kj, cos2, sin2) * (DK ** -0.5)
    v_s[...] = _dot(h, win_ref[:, OFF_V:OFF_G]).astype(BF16)
    gate_s[...] = _silu(_dot(h, win_ref[:, OFF_G:OFF_U]))
    u_s[...] = _gelu(_dot(h, win_ref[:, OFF_U:OFF_S]))
    s = _gelu(_dot(h, win_ref[:, OFF_S:D_IN]))
    for g in range(GROUPS):
        gs = slice(g * GD, (g + 1) * GD)
        sn_s[:, gs] = _layernorm(s[:, gs], lng_ref[:, gs], lnb_ref[:, gs]).astype(BF16)

    row = lax.broadcasted_iota(jnp.int32, (CHUNK, CHUNK), 0)
    col = lax.broadcasted_iota(jnp.int32, (CHUNK, CHUNK), 1)
    tril = row >= col

    for c in range(tm // CHUNK):
        rs = slice(c * CHUNK, (c + 1) * CHUNK)
        for hh in range(HEADS):
            ks = slice(hh * DK, (hh + 1) * DK)
            vs = slice(hh * DV, (hh + 1) * DV)
            qc = q_s[rs, ks]
            kf = k_s[rs, ks]
            vc = v_s[rs, vs]
            st = state_ref[0, 0, hh]
            inner = _dot_nt(qc, kf.astype(BF16)) * dmask_ref[hh]
            o = _dot(inner.astype(BF16), vc) + _dot(qc, st.astype(BF16)) * xi_ref[hh]
            kz = (kf * zeta_ref[hh]).astype(BF16)
            state_ref[0, 0, hh] = cd_ref[hh] * st + _dot_tn(kz, vc)
            on = _groupnorm(o, gn_ref[:, vs])
            mix_s[rs, vs] = (gate_s[rs, vs] * on).astype(BF16)
        for g in range(GROUPS):
            gs = slice(g * GD, (g + 1) * GD)
            wg = jnp.where(tril, wsgu_ref[g], 0.0).astype(BF16)
            mixed = _dot(wg, sn_s[rs, gs]) + bsgu_ref[g]
            mix_s[rs, V_W + g * GD:V_W + (g + 1) * GD] = (u_s[rs, gs] * mixed).astype(BF16)

    x1_ref[0] = x + _dot(mix_s[...], wout_ref[...])


def _mix_prompt(x, gmix, win, wout, cos2, sin2, dmask, xi_b, zeta_b, cd, gn, lng, lnb, wsgu, bsgu_b):
    b, t, d = x.shape
    tm = TM_MIX
    grid = (b, t // tm)
    in_specs = [
        _smem_spec(),
        pl.BlockSpec((1, tm, d), lambda i, j: (i, j, 0)),
        _const_spec((1, d)),
        _const_spec((d, D_IN)),
        _const_spec((V_W + U_W, d)),
        pl.BlockSpec((tm, DK), lambda i, j: (j, 0)),
        pl.BlockSpec((tm, DK), lambda i, j: (j, 0)),
        _const_spec((HEADS, CHUNK, CHUNK)),
        _const_spec((HEADS, CHUNK, DV)),
        _const_spec((HEADS, CHUNK, DK)),
        _const_spec((1, V_W)),
        _const_spec((1, U_W)),
        _const_spec((1, U_W)),
        _const_spec((GROUPS, CHUNK, CHUNK)),
        _const_spec((GROUPS, CHUNK, GD)),
    ]
    out_specs = [
        pl.BlockSpec((1, tm, d), lambda i, j: (i, j, 0)),
        pl.BlockSpec((1, 1, HEADS, DK, DV), lambda i, j: (0, i, 0, 0, 0)),
    ]
    out_shape = [
        jax.ShapeDtypeStruct((b, t, d), F32),
        jax.ShapeDtypeStruct((1, b, HEADS, DK, DV), F32),
    ]
    scratch = [
        pltpu.VMEM((tm, Q_W), BF16),
        pltpu.VMEM((tm, Q_W), F32),
        pltpu.VMEM((tm, V_W), BF16),
        pltpu.VMEM((tm, V_W), F32),
        pltpu.VMEM((tm, U_W), F32),
        pltpu.VMEM((tm, U_W), BF16),
        pltpu.VMEM((tm, V_W + U_W), BF16),
    ]
    return pl.pallas_call(
        _mix_prompt_kernel, grid=grid, in_specs=in_specs, out_specs=out_specs,
        out_shape=out_shape, scratch_shapes=scratch,
        compiler_params=_params("arbitrary", "arbitrary"), name="mix_prompt",
    )(cd, x, gmix, win, wout, cos2, sin2, dmask, xi_b, zeta_b, gn, lng, lnb, wsgu, bsgu_b)


def _ffn_tile(x, g, gfin, w1_s, w2_s, final, before_slice=None):
    h = _rms(x, g).astype(BF16)
    acc = x
    for j in range(FFN_SLICES):
        if before_slice is not None:
            before_slice(j)
        js = slice(j * FFN_HS, (j + 1) * FFN_HS)
        f = _dot(h, w1_s[:, js])
        a = jnp.square(jnp.maximum(f, 0.0)).astype(BF16)
        acc = acc + _dot(a, w2_s[js, :])
    if final:
        acc = _rms(acc, gfin)
    return acc


def _ffn_kernel(xp_ref, xs_ref, g_ref, gfin_ref, w1_hbm, w2_hbm, yp_ref, ys_ref,
                w1_s, w2_s, st1, st2, sem, *, layer, final, n_tiles):
    i = pl.program_id(0)

    def slice_copies(j):
        slot = j % 2
        js = pl.ds(j * FFN_HS, FFN_HS)
        return (pltpu.make_async_copy(w1_hbm.at[layer, :, js], st1.at[slot], sem.at[0, slot]),
                pltpu.make_async_copy(w2_hbm.at[layer, js, :], st2.at[slot], sem.at[1, slot]))

    def fetch_slice(j):
        js = slice(j * FFN_HS, (j + 1) * FFN_HS)
        for c in slice_copies(j):
            c.wait()
        w1_s[:, js] = st1[j % 2].astype(BF16)
        w2_s[js, :] = st2[j % 2].astype(BF16)
        if j + 2 < FFN_SLICES:
            for c in slice_copies(j + 2):
                c.start()

    @pl.when(i == 0)
    def _():
        for j in range(min(2, FFN_SLICES)):
            for c in slice_copies(j):
                c.start()
        yp_ref[...] = _ffn_tile(xp_ref[...], g_ref[...], gfin_ref[...], w1_s, w2_s, final, fetch_slice)

    @pl.when(jnp.logical_and(i > 0, i < n_tiles))
    def _():
        yp_ref[...] = _ffn_tile(xp_ref[...], g_ref[...], gfin_ref[...], w1_s, w2_s, final)

    @pl.when(i == n_tiles)
    def _():
        ys_ref[...] = _ffn_tile(xs_ref[...], g_ref[...], gfin_ref[...], w1_s, w2_s, final)


def _ffn(xp, xs, g, w1, w2, gfin, layer, final):
    m, d = xp.shape
    ms = xs.shape[0]
    tm = TM_FFN
    n_tiles = m // tm
    tile = lambda i: (jnp.minimum(i, n_tiles - 1), 0)
    return pl.pallas_call(
        functools.partial(_ffn_kernel, layer=layer, final=final, n_tiles=n_tiles),
        grid=(n_tiles + 1,),
        in_specs=[pl.BlockSpec((tm, d), tile), _const_spec((ms, d)),
                  _const_spec((1, d)), _const_spec((1, d)),
                  pl.BlockSpec(memory_space=pl.ANY), pl.BlockSpec(memory_space=pl.ANY)],
        out_specs=[pl.BlockSpec((tm, d), tile), pl.BlockSpec((ms, d), lambda i: (0, 0))],
        out_shape=[jax.ShapeDtypeStruct((m, d), F32), jax.ShapeDtypeStruct((ms, d), F32)],
        scratch_shapes=[pltpu.VMEM((d, D_FF), BF16), pltpu.VMEM((D_FF, d), BF16),
                        pltpu.VMEM((2, d, FFN_HS), F32), pltpu.VMEM((2, FFN_HS, d), F32),
                        pltpu.SemaphoreType.DMA((2, 2))],
        compiler_params=_params("arbitrary"), name="ffn_final" if final else "ffn",
    )(xp, xs, g, gfin, w1, w2)


def _glu_in(x, g, win_ref, bin_ref):
    h = _rms(x, g).astype(BF16)
    z = _dot(h, win_ref[...]) + bin_ref[...]
    return z[:, :D_MODEL] * _sigmoid(z[:, D_MODEL:])


def _conv_block(r, full_s, conv_s, dww_ref, dwb_ref):
    off0 = HIST_PAD - HIST
    base = pl.multiple_of(r * CONV_RB, CONV_RB)
    for lc in range(D_MODEL // CONV_LC):
        ls = slice(lc * CONV_LC, (lc + 1) * CONV_LC)
        win = full_s[pl.ds(base, CONV_RB + HIST_PAD), ls]
        out = dwb_ref[:, ls]
        for p in range(SUBLANES):
            rows = CONV_RB + (SUBLANES if p else 0)
            z = None
            for a in range(HIST_PAD // SUBLANES + 1):
                m = SUBLANES * a + p
                if m < off0 or m > HIST_PAD:
                    continue
                term = dww_ref[m - off0:m - off0 + 1, ls] * win[SUBLANES * a:SUBLANES * a + rows, :]
                z = term if z is None else z + term
            out = out + z[p:p + CONV_RB, :]
        conv_s[pl.ds(base, CONV_RB), ls] = out


def _conv_prompt_kernel(x_ref, xprev_ref, gmix_ref, win_ref, bin_ref, dww_ref, dwb_ref, lng_ref, lnb_ref,
                        wout_ref, bout_ref, o_ref, hist_ref, full_s, conv_s, *, n_tiles, tiles_per_seq):
    s = pl.program_id(0)
    tm = x_ref.shape[0]
    off0 = HIST_PAD - HIST

    def front():
        full_s[HIST_PAD:HIST_PAD + tm, :] = _glu_in(x_ref[...], gmix_ref[...], win_ref, bin_ref)

    def back():
        c = _silu(_layernorm(conv_s[...], lng_ref[...], lnb_ref[...])).astype(BF16)
        o_ref[...] = xprev_ref[...] + _dot(c, wout_ref[...]) + bout_ref[...]

    @pl.when(s % tiles_per_seq == 0)
    def _():
        full_s[0:HIST_PAD, :] = jnp.zeros((HIST_PAD, D_MODEL), F32)

    @pl.when(s == 0)
    def _():
        front()

    @pl.when(jnp.logical_and(s > 0, s < n_tiles))
    def _():
        front()
        back()

    @pl.when(s == n_tiles)
    def _():
        back()

    @pl.when(s < n_tiles)
    def _():
        def body(r, carry):
            _conv_block(r, full_s, conv_s, dww_ref, dwb_ref)
            return carry
        lax.fori_loop(0, tm // CONV_RB, body, 0)

        @pl.when(s % tiles_per_seq == tiles_per_seq - 1)
        def _():
            hist_ref[0, 0] = full_s[tm + off0:tm + HIST_PAD, :]

        full_s[0:HIST_PAD, :] = full_s[tm:tm + HIST_PAD, :]


def _conv_prompt(x, gmix, win, bin_, dww, dwb, lng, lnb, wout, bout):
    b, t, d = x.shape
    tm = TM_CONV
    n_tiles = (b * t) // tm
    tiles_per_seq = t // tm
    in_specs = [
        pl.BlockSpec((tm, d), lambda s: (jnp.minimum(s, n_tiles - 1), 0)),
        pl.BlockSpec((tm, d), lambda s: (jnp.maximum(s - 1, 0), 0)),
        _const_spec((1, d)), _const_spec((d, 2 * d)), _const_spec((1, 2 * d)),
        _const_spec((CONV_WIDTH, d)), _const_spec((1, d)), _const_spec((1, d)), _const_spec((1, d)),
        _const_spec((d, d)), _const_spec((1, d)),
    ]
    out_specs = [
        pl.BlockSpec((tm, d), lambda s: (jnp.maximum(s - 1, 0), 0)),
        pl.BlockSpec((1, 1, HIST, d), lambda s: (0, jnp.minimum(s, n_tiles - 1) // tiles_per_seq, 0, 0)),
    ]
    out_shape = [jax.ShapeDtypeStruct((b * t, d), F32),
                 jax.ShapeDtypeStruct((1, b, HIST, d), F32)]
    scratch = [
        pltpu.VMEM((tm + HIST_PAD, d), F32),
        pltpu.VMEM((tm, d), F32),
    ]
    x2d = x.reshape(b * t, d)
    y, hist = pl.pallas_call(
        functools.partial(_conv_prompt_kernel, n_tiles=n_tiles, tiles_per_seq=tiles_per_seq),
        grid=(n_tiles + 1,), in_specs=in_specs, out_specs=out_specs,
        out_shape=out_shape, scratch_shapes=scratch,
        compiler_params=_params("arbitrary"), name="conv_prompt",
    )(x2d, x2d, gmix, win, bin_, dww, dwb, lng, lnb, wout, bout)
    return y, hist


def _mix_sample_in_kernel(sgu_ref, x_ref, gmix_ref, win_ref, cos_ref, sin_ref, lng_ref, lnb_ref,
                          q_ref, k_ref, v_ref, gate_ref, sgu_out_ref, rows_ref):
    h = _rms(x_ref[...], gmix_ref[...]).astype(BF16)
    qk = _dot(h, win_ref[:, 0:OFF_V])
    cos2 = cos_ref[...]
    sin2 = sin_ref[...]
    for j in range(HEADS):
        q_ref[:, j * DK:(j + 1) * DK] = _rotary(qk[:, j * DK:(j + 1) * DK], cos2, sin2)
        kj = qk[:, Q_W + j * DK:Q_W + (j + 1) * DK]
        k_ref[:, j * DK:(j + 1) * DK] = _rotary(kj, cos2, sin2) * (DK ** -0.5)
    v_ref[...] = _dot(h, win_ref[:, OFF_V:OFF_G])
    gate_ref[...] = _silu(_dot(h, win_ref[:, OFF_G:OFF_U]))
    u = _gelu(_dot(h, win_ref[:, OFF_U:OFF_S]))
    s = _gelu(_dot(h, win_ref[:, OFF_S:D_IN]))
    for g in range(GROUPS):
        gs = slice(g * GD, (g + 1) * GD)
        sn = _layernorm(s[:, gs], lng_ref[:, gs], lnb_ref[:, gs])
        rows_ref[:, gs] = sn
        sgu_out_ref[:, gs] = u[:, gs] * (sgu_ref[0, g] * sn + sgu_ref[1, g])


def _mix_sample_in(x, gmix, win, cos2, sin2, lng, lnb, sgu_wb):
    m, d = x.shape
    shp = lambda n: jax.ShapeDtypeStruct((m, n), F32)
    vm = lambda: pl.BlockSpec(memory_space=pltpu.VMEM)
    return pl.pallas_call(
        _mix_sample_in_kernel,
        in_specs=[_smem_spec()] + [vm() for _ in range(7)],
        out_specs=[vm() for _ in range(6)],
        out_shape=[shp(Q_W), shp(Q_W), shp(V_W), shp(V_W), shp(U_W), shp(U_W)],
        compiler_params=pltpu.CompilerParams(vmem_limit_bytes=VMEM_LIMIT), name="mix_sample_in",
    )(sgu_wb, x, gmix, win, cos2, sin2, lng, lnb)


def _ret_sample_kernel(decay_ref, q_ref, k_ref, v_ref, s_ref, snew_ref, o_ref):
    bb = q_ref.shape[0]
    for hh in range(HEADS):
        ks = slice(hh * DK, (hh + 1) * DK)
        vs = slice(hh * DV, (hh + 1) * DV)
        qt = q_ref[:, ks].T
        kt = k_ref[:, ks].T
        gam = decay_ref[hh]
        for b in range(bb):
            st = s_ref[0, b, hh]
            sn = gam * st + kt[:, b:b + 1] * v_ref[b:b + 1, vs]
            snew_ref[0, b, hh] = sn
            o_ref[b:b + 1, vs] = jnp.sum(qt[:, b:b + 1] * sn, axis=0, keepdims=True)


def _ret_sample(decay, q, k, v, state):
    m = q.shape[0]
    bb = BB_RET
    return pl.pallas_call(
        _ret_sample_kernel, grid=(m // bb,),
        in_specs=[_smem_spec(),
                  pl.BlockSpec((bb, Q_W), lambda i: (i, 0)),
                  pl.BlockSpec((bb, Q_W), lambda i: (i, 0)),
                  pl.BlockSpec((bb, V_W), lambda i: (i, 0)),
                  pl.BlockSpec((1, bb, HEADS, DK, DV), lambda i: (0, i, 0, 0, 0))],
        out_specs=[pl.BlockSpec((1, bb, HEADS, DK, DV), lambda i: (0, i, 0, 0, 0)),
                   pl.BlockSpec((bb, V_W), lambda i: (i, 0))],
        out_shape=[jax.ShapeDtypeStruct(state.shape, F32), jax.ShapeDtypeStruct((m, V_W), F32)],
        compiler_params=_params("arbitrary"), name="ret_sample",
    )(decay, q, k, v, state)


def _mix_sample_out_kernel(x_ref, o_ref, gate_ref, sgu_ref, gn_ref, wout_ref, x1_ref):
    o = o_ref[...]
    parts = []
    for hh in range(HEADS):
        vs = slice(hh * DV, (hh + 1) * DV)
        parts.append((gate_ref[:, vs] * _groupnorm(o[:, vs], gn_ref[:, vs])).astype(BF16))
    y = _dot(jnp.concatenate(parts, axis=-1), wout_ref[0:V_W, :])
    y = y + _dot(sgu_ref[...].astype(BF16), wout_ref[V_W:V_W + U_W, :])
    x1_ref[...] = x_ref[...] + y


def _mix_sample_out(x, o, gate, sgu_out, gn, wout):
    m, d = x.shape
    vm = lambda: pl.BlockSpec(memory_space=pltpu.VMEM)
    return pl.pallas_call(
        _mix_sample_out_kernel, in_specs=[vm() for _ in range(6)], out_specs=vm(),
        out_shape=jax.ShapeDtypeStruct((m, d), F32),
        compiler_params=pltpu.CompilerParams(vmem_limit_bytes=VMEM_LIMIT), name="mix_sample_out",
    )(x, o, gate, sgu_out, gn, wout)


def _conv_sample_in_kernel(x_ref, g_ref, win_ref, bin_ref, xg_ref):
    xg_ref[...] = _glu_in(x_ref[...], g_ref[...], win_ref, bin_ref)


def _conv_sample_in(x, g, win, bin_):
    m, d = x.shape
    vm = lambda: pl.BlockSpec(memory_space=pltpu.VMEM)
    return pl.pallas_call(
        _conv_sample_in_kernel, in_specs=[vm() for _ in range(4)], out_specs=vm(),
        out_shape=jax.ShapeDtypeStruct((m, d), F32),
        compiler_params=pltpu.CompilerParams(vmem_limit_bytes=VMEM_LIMIT), name="conv_sample_in",
    )(x, g, win, bin_)


def _conv_sample_state_kernel(hist_ref, xg_ref, dww_ref, dwb_ref, newhist_ref, conv_ref):
    xg = xg_ref[...]
    acc = dwb_ref[...] + xg * dww_ref[HIST:CONV_WIDTH, :]
    for j in range(HIST):
        acc = acc + hist_ref[j] * dww_ref[j:j + 1, :]
    conv_ref[...] = acc
    for j in range(HIST - 1):
        newhist_ref[j] = hist_ref[j + 1]
    newhist_ref[HIST - 1] = xg


def _conv_sample_state(hist_t, xg, dww, dwb):
    _, m, d = hist_t.shape
    bb = BB_CONV
    return pl.pallas_call(
        _conv_sample_state_kernel, grid=(m // bb,),
        in_specs=[pl.BlockSpec((HIST, bb, d), lambda i: (0, i, 0)),
                  pl.BlockSpec((bb, d), lambda i: (i, 0)),
                  _const_spec((CONV_WIDTH, d)), _const_spec((1, d))],
        out_specs=[pl.BlockSpec((HIST, bb, d), lambda i: (0, i, 0)),
                   pl.BlockSpec((bb, d), lambda i: (i, 0))],
        out_shape=[jax.ShapeDtypeStruct(hist_t.shape, F32), jax.ShapeDtypeStruct((m, d), F32)],
        compiler_params=_params("arbitrary"), name="conv_sample_state",
    )(hist_t, xg, dww, dwb)


def _conv_sample_out_kernel(x_ref, conv_ref, lng_ref, lnb_ref, wout_ref, bout_ref, o_ref):
    c = _silu(_layernorm(conv_ref[...], lng_ref[...], lnb_ref[...])).astype(BF16)
    o_ref[...] = x_ref[...] + _dot(c, wout_ref[...]) + bout_ref[...]


def _conv_sample_out(x, conv, lng, lnb, wout, bout):
    m, d = x.shape
    vm = lambda: pl.BlockSpec(memory_space=pltpu.VMEM)
    return pl.pallas_call(
        _conv_sample_out_kernel, in_specs=[vm() for _ in range(6)], out_specs=vm(),
        out_shape=jax.ShapeDtypeStruct((m, d), F32),
        compiler_params=pltpu.CompilerParams(vmem_limit_bytes=VMEM_LIMIT), name="conv_sample_out",
    )(x, conv, lng, lnb, wout, bout)


def _rope_tables(pos):
    half = DK // 2
    inv = ROPE_BASE ** (-jnp.arange(half, dtype=F32) / half)
    ang = pos.astype(F32)[:, None] * inv[None, :]
    cos = jnp.cos(ang)
    sin = jnp.sin(ang)
    return jnp.concatenate([cos, cos], axis=-1), jnp.concatenate([-sin, sin], axis=-1)


def _decay_tables(c):
    log_g = jnp.log1p(-(2.0 ** (-5.0 - jnp.arange(HEADS, dtype=F32))))
    idx = jnp.arange(c, dtype=F32)
    diff = idx[:, None] - idx[None, :]
    dmask = jnp.where(diff >= 0, jnp.exp(jnp.maximum(diff, 0.0)[None] * log_g[:, None, None]), 0.0)
    xi = jnp.exp((idx[None, :] + 1.0) * log_g[:, None])
    zeta = jnp.exp((c - 1.0 - idx[None, :]) * log_g[:, None])
    chunk_decay = jnp.exp(c * log_g)
    return dmask, xi, zeta, chunk_decay


def kernel(x_prompt, x_sample, state_ret, state_conv, norm_mix_g, norm_ffn_g, final_norm_g, ab_w_in, ab_w_out, ret_gn_g, sgu_ln_g, sgu_ln_b, sgu_w, sgu_b, conv_w_in, conv_b_in, conv_dw_w, conv_dw_b, conv_ln_g, conv_ln_b, conv_w_out, conv_b_out, ffn_w1, ffn_w2):
    b, t, d = x_prompt.shape
    m_s = x_sample.shape[0]
    row = lambda a: a.reshape(1, -1)

    win = ab_w_in[0].astype(BF16)
    wout = ab_w_out[0].astype(BF16)
    cwin = conv_w_in[0].astype(BF16)
    cwout = conv_w_out[0].astype(BF16)
    gmix = [row(norm_mix_g[l]) for l in range(2)]
    gffn = [row(norm_ffn_g[l]) for l in range(2)]
    gfin = row(final_norm_g)
    gn, lng, lnb = row(ret_gn_g[0]), row(sgu_ln_g[0]), row(sgu_ln_b[0])
    cbin, cdwb, clng, clnb, cbout = (row(conv_b_in[0]), row(conv_dw_b[0]), row(conv_ln_g[0]),
                                     row(conv_ln_b[0]), row(conv_b_out[0]))

    cos_p, sin_p = _rope_tables(jnp.arange(t))
    dmask, xi, zeta, cd = _decay_tables(CHUNK)
    xi_b = jnp.broadcast_to(xi[:, :, None], (HEADS, CHUNK, DV))
    zeta_b = jnp.broadcast_to(zeta[:, :, None], (HEADS, CHUNK, DK))
    bsgu_b = jnp.broadcast_to(sgu_b[0][:, :CHUNK, None], (GROUPS, CHUNK, GD))
    x1, ret_state_p = _mix_prompt(x_prompt, gmix[0], win, wout, cos_p, sin_p, dmask, xi_b, zeta_b, cd,
                                  gn, lng, lnb, sgu_w[0], bsgu_b)

    xs = x_sample.reshape(m_s, d)
    cos_s, sin_s = _rope_tables(PAST_LEN + jnp.arange(1))
    _, _, _, decay1 = _decay_tables(1)
    sgu_wb = jnp.stack([sgu_w[0][:, 0, 0], sgu_b[0][:, 0]])
    q, k, v, gate, sgu_out, rows = _mix_sample_in(xs, gmix[0], win, cos_s, sin_s, lng, lnb, sgu_wb)
    ret_state_s, o = _ret_sample(decay1, q, k, v, state_ret)
    xs1 = _mix_sample_out(xs, o, gate, sgu_out, gn, wout)

    x2, xs2 = _ffn(x1.reshape(b * t, d), xs1, gffn[0], ffn_w1, ffn_w2, gfin, 0, False)

    x3, conv_state_p = _conv_prompt(x2.reshape(b, t, d), gmix[1], cwin, cbin, conv_dw_w[0], cdwb, clng, clnb,
                                    cwout, cbout)
    xg = _conv_sample_in(xs2, gmix[1], cwin, cbin)
    hist_t = jnp.transpose(state_conv[0], (1, 0, 2))
    newhist_t, conv = _conv_sample_state(hist_t, xg, conv_dw_w[0], cdwb)
    conv_state_s = jnp.transpose(newhist_t, (1, 0, 2))[None]
    xs3 = _conv_sample_out(xs2, conv, clng, clnb, cwout, cbout)

    y_prompt, y_sample = _ffn(x3, xs3, gffn[1], ffn_w1, ffn_w2, gfin, 1, True)

    return (y_prompt.reshape(b, t, d), y_sample.reshape(m_s, 1, d), ret_state_p, ret_state_s,
            rows.reshape(1, m_s, 1, U_W), conv_state_p, conv_state_s)
```

```python
import functools

import jax
import jax.numpy as jnp
from jax import lax
from jax.experimental import pallas as pl
from jax.experimental.pallas import tpu as pltpu

F32 = jnp.float32
BF16 = jnp.bfloat16

D_MODEL = 1024
HEADS = 4
DK = 128
DV = 256
GROUPS = 4
GD = 256
CHUNK = 128
ROPE_BASE = 10000.0
CONV_WIDTH = 31
HIST = CONV_WIDTH - 1
D_FF = 4 * D_MODEL
RMS_EPS = 1e-6
LN_EPS = 1e-5
PAST_LEN = 16384

Q_W = HEADS * DK
V_W = HEADS * DV
U_W = GROUPS * GD
OFF_V = 2 * Q_W
OFF_G = OFF_V + V_W
OFF_U = OFF_G + V_W
OFF_S = OFF_U + U_W
D_IN = OFF_S + U_W

SUBLANES = 8
VMEM_LIMIT = 56 * 1024 * 1024

TM_MIX = 512
TM_FFN = 512
TM_CONV = 512
CONV_RB = 64
CONV_LC = 128
HIST_PAD = 32
FFN_SLICES = 4
FFN_HS = D_FF // FFN_SLICES
BB_RET = 8
BB_CONV = 16


def _params(*sem):
    return pltpu.CompilerParams(dimension_semantics=sem, vmem_limit_bytes=VMEM_LIMIT)


def _const_spec(shape):
    n = len(shape)
    return pl.BlockSpec(shape, lambda *_: (0,) * n, pipeline_mode=pl.Buffered(1))


def _smem_spec():
    return pl.BlockSpec(memory_space=pltpu.SMEM)


def _rms(x, g):
    ms = jnp.mean(x * x, axis=-1, keepdims=True)
    return x * lax.rsqrt(ms + RMS_EPS) * g


def _layernorm(x, g, b):
    mu = jnp.mean(x, axis=-1, keepdims=True)
    xc = x - mu
    var = jnp.mean(xc * xc, axis=-1, keepdims=True)
    return xc * lax.rsqrt(var + LN_EPS) * g + b


def _groupnorm(x, g):
    mu = jnp.mean(x, axis=-1, keepdims=True)
    xc = x - mu
    var = jnp.mean(xc * xc, axis=-1, keepdims=True)
    return xc * lax.rsqrt(var + LN_EPS) * g


def _sigmoid(x):
    return 1.0 / (1.0 + jnp.exp(-x))


def _silu(x):
    return x * _sigmoid(x)


def _gelu(x):
    return 0.5 * x * (1.0 + lax.erf(x * 0.7071067811865476))


def _dot(a, b):
    return jnp.dot(a, b, preferred_element_type=F32)


def _dot_nt(a, b):
    return lax.dot_general(a, b, (((1,), (1,)), ((), ())), preferred_element_type=F32)


def _dot_tn(a, b):
    return lax.dot_general(a, b, (((0,), (0,)), ((), ())), preferred_element_type=F32)


def _rotary(blk, cos2, sin2):
    return blk * cos2 + pltpu.roll(blk, DK // 2, 1) * sin2


def _mix_prompt_kernel(cd_ref, x_ref, gmix_ref, win_ref, wout_ref, cos_ref, sin_ref,
                       dmask_ref, xi_ref, zeta_ref, gn_ref, lng_ref, lnb_ref,
                       wsgu_ref, bsgu_ref,
                       x1_ref, state_ref,
                       q_s, k_s, kz_s, v_s, gate_s, u_s, sn_s, mix_s, inner_s, kv_s, sb_s):
    t = pl.program_id(1)
    tm = x_ref.shape[1]

    @pl.when(t == 0)
    def _():
        state_ref[...] = jnp.zeros_like(state_ref)

    x = x_ref[0]
    h = _rms(x, gmix_ref[...]).astype(BF16)

    cos2 = cos_ref[...]
    sin2 = sin_ref[...]
    n_chunks = tm // CHUNK

    def proj(off, g):
        return _dot(h, win_ref[:, off + g * GD:off + (g + 1) * GD])

    for pair in range(Q_W // GD):
        qq = proj(0, pair)
        kk = proj(Q_W, pair)
        for e in range(GD // DK):
            j = pair * (GD // DK) + e
            ks = slice(j * DK, (j + 1) * DK)
            q_s[:, ks] = _rotary(qq[:, e * DK:(e + 1) * DK], cos2, sin2).astype(BF16)
            kj = _rotary(kk[:, e * DK:(e + 1) * DK], cos2, sin2) * (DK ** -0.5)
            k_s[:, ks] = kj.astype(BF16)
            for c in range(n_chunks):
                rs = slice(c * CHUNK, (c + 1) * CHUNK)
                kz_s[rs, ks] = (kj[rs, :] * zeta_ref[j]).astype(BF16)
    for g in range(GROUPS):
        gs = slice(g * GD, (g + 1) * GD)
        v_s[:, gs] = proj(OFF_V, g).astype(BF16)
        gate_s[:, gs] = _silu(proj(OFF_G, g))
        u_s[:, gs] = _gelu(proj(OFF_U, g))
        sn_s[:, gs] = _layernorm(_gelu(proj(OFF_S, g)), lng_ref[:, gs], lnb_ref[:, gs]).astype(BF16)

    row = lax.broadcasted_iota(jnp.int32, (CHUNK, CHUNK), 0)
    col = lax.broadcasted_iota(jnp.int32, (CHUNK, CHUNK), 1)
    tril = row >= col

    pairs = [(c, hh) for c in range(n_chunks) for hh in range(HEADS)]
    for c, hh in pairs:
        rs = slice(c * CHUNK, (c + 1) * CHUNK)
        ks = slice(hh * DK, (hh + 1) * DK)
        inner = _dot_nt(q_s[rs, ks], k_s[rs, ks]) * dmask_ref[hh]
        inner_s[c * HEADS + hh] = inner.astype(BF16)
    for c, hh in pairs:
        rs = slice(c * CHUNK, (c + 1) * CHUNK)
        kv_s[c * HEADS + hh] = _dot_tn(kz_s[rs, hh * DK:(hh + 1) * DK], v_s[rs, hh * DV:(hh + 1) * DV])
    for hh in range(HEADS):
        st = state_ref[0, 0, hh]
        for c in range(n_chunks):
            sb_s[c * HEADS + hh] = st.astype(BF16)
            st = cd_ref[hh] * st + kv_s[c * HEADS + hh]
        state_ref[0, 0, hh] = st
    for c, hh in pairs:
        rs = slice(c * CHUNK, (c + 1) * CHUNK)
        vs = slice(hh * DV, (hh + 1) * DV)
        o = _dot(inner_s[c * HEADS + hh], v_s[rs, vs])
        o = o + _dot(q_s[rs, hh * DK:(hh + 1) * DK], sb_s[c * HEADS + hh]) * xi_ref[hh]
        on = _groupnorm(o, gn_ref[:, vs])
        mix_s[rs, vs] = (gate_s[rs, vs] * on).astype(BF16)

    for c in range(n_chunks):
        rs = slice(c * CHUNK, (c + 1) * CHUNK)
        for g in range(GROUPS):
            gs = slice(g * GD, (g + 1) * GD)
            wg = jnp.where(tril, wsgu_ref[g], 0.0).astype(BF16)
            mixed = _dot(wg, sn_s[rs, gs]) + bsgu_ref[g]
            mix_s[rs, V_W + g * GD:V_W + (g + 1) * GD] = (u_s[rs, gs] * mixed).astype(BF16)

    x1_ref[0] = x + _dot(mix_s[...], wout_ref[...])


def _mix_prompt(x, gmix, win, wout, cos2, sin2, dmask, xi_b, zeta_b, cd, gn, lng, lnb, wsgu, bsgu_b):
    b, t, d = x.shape
    tm = TM_MIX
    n_pairs = (tm // CHUNK) * HEADS
    grid = (b, t // tm)
    in_specs = [
        _smem_spec(),
        pl.BlockSpec((1, tm, d), lambda i, j: (i, j, 0)),
        _const_spec((1, d)),
        _const_spec((d, D_IN)),
        _const_spec((V_W + U_W, d)),
        pl.BlockSpec((tm, DK), lambda i, j: (j, 0)),
        pl.BlockSpec((tm, DK), lambda i, j: (j, 0)),
        _const_spec((HEADS, CHUNK, CHUNK)),
        _const_spec((HEADS, CHUNK, DV)),
        _const_spec((HEADS, CHUNK, DK)),
        _const_spec((1, V_W)),
        _const_spec((1, U_W)),
        _const_spec((1, U_W)),
        _const_spec((GROUPS, CHUNK, CHUNK)),
        _const_spec((GROUPS, CHUNK, GD)),
    ]
    out_specs = [
        pl.BlockSpec((1, tm, d), lambda i, j: (i, j, 0)),
        pl.BlockSpec((1, 1, HEADS, DK, DV), lambda i, j: (0, i, 0, 0, 0)),
    ]
    out_shape = [
        jax.ShapeDtypeStruct((b, t, d), F32),
        jax.ShapeDtypeStruct((1, b, HEADS, DK, DV), F32),
    ]
    scratch = [
        pltpu.VMEM((tm, Q_W), BF16),
        pltpu.VMEM((tm, Q_W), BF16),
        pltpu.VMEM((tm, Q_W), BF16),
        pltpu.VMEM((tm, V_W), BF16),
        pltpu.VMEM((tm, V_W), F32),
        pltpu.VMEM((tm, U_W), F32),
        pltpu.VMEM((tm, U_W), BF16),
        pltpu.VMEM((tm, V_W + U_W), BF16),
        pltpu.VMEM((n_pairs, CHUNK, CHUNK), BF16),
        pltpu.VMEM((n_pairs, DK, DV), F32),
        pltpu.VMEM((n_pairs, DK, DV), BF16),
    ]
    return pl.pallas_call(
        _mix_prompt_kernel, grid=grid, in_specs=in_specs, out_specs=out_specs,
        out_shape=out_shape, scratch_shapes=scratch,
        compiler_params=_params("arbitrary", "arbitrary"), name="mix_prompt",
    )(cd, x, gmix, win, wout, cos2, sin2, dmask, xi_b, zeta_b, gn, lng, lnb, wsgu, bsgu_b)


def _ffn_tile(x, g, gfin, w1_s, w2_s, final, before_slice=None):
    h = _rms(x, g).astype(BF16)
    acc = x
    for j in range(FFN_SLICES):
        if before_slice is not None:
            before_slice(j)
        js = slice(j * FFN_HS, (j + 1) * FFN_HS)
        f = _dot(h, w1_s[:, js])
        a = jnp.square(jnp.maximum(f, 0.0)).astype(BF16)
        acc = acc + _dot(a, w2_s[js, :])
    if final:
        acc = _rms(acc, gfin)
    return acc


def _ffn_kernel(xp_ref, xs_ref, g_ref, gfin_ref, w1_hbm, w2_hbm, yp_ref, ys_ref,
                w1_s, w2_s, st1, st2, sem, *, layer, final, n_tiles):
    i = pl.program_id(0)

    def slice_copies(j):
        slot = j % 2
        js = pl.ds(j * FFN_HS, FFN_HS)
        return (pltpu.make_async_copy(w1_hbm.at[layer, :, js], st1.at[slot], sem.at[0, slot]),
                pltpu.make_async_copy(w2_hbm.at[layer, js, :], st2.at[slot], sem.at[1, slot]))

    def fetch_slice(j):
        js = slice(j * FFN_HS, (j + 1) * FFN_HS)
        for c in slice_copies(j):
            c.wait()
        w1_s[:, js] = st1[j % 2].astype(BF16)
        w2_s[js, :] = st2[j % 2].astype(BF16)
        if j + 2 < FFN_SLICES:
            for c in slice_copies(j + 2):
                c.start()

    @pl.when(i == 0)
    def _():
        for j in range(min(2, FFN_SLICES)):
            for c in slice_copies(j):
                c.start()
        yp_ref[...] = _ffn_tile(xp_ref[...], g_ref[...], gfin_ref[...], w1_s, w2_s, final, fetch_slice)

    @pl.when(jnp.logical_and(i > 0, i < n_tiles))
    def _():
        yp_ref[...] = _ffn_tile(xp_ref[...], g_ref[...], gfin_ref[...], w1_s, w2_s, final)

    @pl.when(i == n_tiles)
    def _():
        ys_ref[...] = _ffn_tile(xs_ref[...], g_ref[...], gfin_ref[...], w1_s, w2_s, final)


def _ffn(xp, xs, g, w1, w2, gfin, layer, final):
    m, d = xp.shape
    ms = xs.shape[0]
    tm = TM_FFN
    n_tiles = m // tm
    tile = lambda i: (jnp.minimum(i, n_tiles - 1), 0)
    return pl.pallas_call(
        functools.partial(_ffn_kernel, layer=layer, final=final, n_tiles=n_tiles),
        grid=(n_tiles + 1,),
        in_specs=[pl.BlockSpec((tm, d), tile), _const_spec((ms, d)),
                  _const_spec((1, d)), _const_spec((1, d)),
                  pl.BlockSpec(memory_space=pl.ANY), pl.BlockSpec(memory_space=pl.ANY)],
        out_specs=[pl.BlockSpec((tm, d), tile), pl.BlockSpec((ms, d), lambda i: (0, 0))],
        out_shape=[jax.ShapeDtypeStruct((m, d), F32), jax.ShapeDtypeStruct((ms, d), F32)],
        scratch_shapes=[pltpu.VMEM((d, D_FF), BF16), pltpu.VMEM((D_FF, d), BF16),
                        pltpu.VMEM((2, d, FFN_HS), F32), pltpu.VMEM((2, FFN_HS, d), F32),
                        pltpu.SemaphoreType.DMA((2, 2))],
        compiler_params=_params("arbitrary"), name="ffn_final" if final else "ffn",
    )(xp, xs, g, gfin, w1, w2)


def _glu_in(x, g, win_ref, bin_ref):
    h = _rms(x, g).astype(BF16)
    pieces = []
    for c in range(D_MODEL // GD):
        ca = slice(c * GD, (c + 1) * GD)
        cg = slice(D_MODEL + c * GD, D_MODEL + (c + 1) * GD)
        a = _dot(h, win_ref[:, ca]) + bin_ref[:, ca]
        gt = _dot(h, win_ref[:, cg]) + bin_ref[:, cg]
        pieces.append(a * _sigmoid(gt))
    return jnp.concatenate(pieces, axis=-1)


def _conv_block(r, full_s, conv_s, dww_ref, dwb_ref):
    off0 = HIST_PAD - HIST
    base = pl.multiple_of(r * CONV_RB, CONV_RB)
    for lc in range(D_MODEL // CONV_LC):
        ls = slice(lc * CONV_LC, (lc + 1) * CONV_LC)
        win = full_s[pl.ds(base, CONV_RB + HIST_PAD), ls]
        out = dwb_ref[:, ls]
        for p in range(SUBLANES):
            rows = CONV_RB + (SUBLANES if p else 0)
            z = None
            for a in range(HIST_PAD // SUBLANES + 1):
                m = SUBLANES * a + p
                if m < off0 or m > HIST_PAD:
                    continue
                term = dww_ref[m - off0:m - off0 + 1, ls] * win[SUBLANES * a:SUBLANES * a + rows, :]
                z = term if z is None else z + term
            out = out + z[p:p + CONV_RB, :]
        conv_s[pl.ds(base, CONV_RB), ls] = out


def _conv_prompt_kernel(x_ref, xprev_ref, gmix_ref, win_ref, bin_ref, dww_ref, dwb_ref, lng_ref, lnb_ref,
                        wout_ref, bout_ref, o_ref, hist_ref, full_s, conv_s, *, n_tiles, tiles_per_seq):
    s = pl.program_id(0)
    tm = x_ref.shape[0]
    off0 = HIST_PAD - HIST

    def front():
        full_s[HIST_PAD:HIST_PAD + tm, :] = _glu_in(x_ref[...], gmix_ref[...], win_ref, bin_ref)

    def back():
        c = _silu(_layernorm(conv_s[...], lng_ref[...], lnb_ref[...])).astype(BF16)
        o_ref[...] = xprev_ref[...] + _dot(c, wout_ref[...]) + bout_ref[...]

    @pl.when(s % tiles_per_seq == 0)
    def _():
        full_s[0:HIST_PAD, :] = jnp.zeros((HIST_PAD, D_MODEL), F32)

    @pl.when(s == 0)
    def _():
        front()

    @pl.when(jnp.logical_and(s > 0, s < n_tiles))
    def _():
        front()
        back()

    @pl.when(s == n_tiles)
    def _():
        back()

    @pl.when(s < n_tiles)
    def _():
        def body(r, carry):
            _conv_block(r, full_s, conv_s, dww_ref, dwb_ref)
            return carry
        lax.fori_loop(0, tm // CONV_RB, body, 0)

        @pl.when(s % tiles_per_seq == tiles_per_seq - 1)
        def _():
            hist_ref[0, 0] = full_s[tm + off0:tm + HIST_PAD, :]

        full_s[0:HIST_PAD, :] = full_s[tm:tm + HIST_PAD, :]


def _conv_prompt(x, gmix, win, bin_, dww, dwb, lng, lnb, wout, bout):
    b, t, d = x.shape
    tm = TM_CONV
    n_tiles = (b * t) // tm
    tiles_per_seq = t // tm
    in_specs = [
        pl.BlockSpec((tm, d), lambda s: (jnp.minimum(s, n_tiles - 1), 0)),
        pl.BlockSpec((tm, d), lambda s: (jnp.maximum(s - 1, 0), 0)),
        _const_spec((1, d)), _const_spec((d, 2 * d)), _const_spec((1, 2 * d)),
        _const_spec((CONV_WIDTH, d)), _const_spec((1, d)), _const_spec((1, d)), _const_spec((1, d)),
        _const_spec((d, d)), _const_spec((1, d)),
    ]
    out_specs = [
        pl.BlockSpec((tm, d), lambda s: (jnp.maximum(s - 1, 0), 0)),
        pl.BlockSpec((1, 1, HIST, d), lambda s: (0, jnp.minimum(s, n_tiles - 1) // tiles_per_seq, 0, 0)),
    ]
    out_shape = [jax.ShapeDtypeStruct((b * t, d), F32),
                 jax.ShapeDtypeStruct((1, b, HIST, d), F32)]
    scratch = [
        pltpu.VMEM((tm + HIST_PAD, d), F32),
        pltpu.VMEM((tm, d), F32),
    ]
    x2d = x.reshape(b * t, d)
    y, hist = pl.pallas_call(
        functools.partial(_conv_prompt_kernel, n_tiles=n_tiles, tiles_per_seq=tiles_per_seq),
        grid=(n_tiles + 1,), in_specs=in_specs, out_specs=out_specs,
        out_shape=out_shape, scratch_shapes=scratch,
        compiler_params=_params("arbitrary"), name="conv_prompt",
    )(x2d, x2d, gmix, win, bin_, dww, dwb, lng, lnb, wout, bout)
    return y, hist


def _mix_sample_in_kernel(sgu_ref, x_ref, gmix_ref, win_ref, cos_ref, sin_ref, lng_ref, lnb_ref,
                          q_ref, k_ref, v_ref, gate_ref, sgu_out_ref, rows_ref):
    h = _rms(x_ref[...], gmix_ref[...]).astype(BF16)
    qk = _dot(h, win_ref[:, 0:OFF_V])
    cos2 = cos_ref[...]
    sin2 = sin_ref[...]
    for j in range(HEADS):
        q_ref[:, j * DK:(j + 1) * DK] = _rotary(qk[:, j * DK:(j + 1) * DK], cos2, sin2)
        kj = qk[:, Q_W + j * DK:Q_W + (j + 1) * DK]
        k_ref[:, j * DK:(j + 1) * DK] = _rotary(kj, cos2, sin2) * (DK ** -0.5)
    v_ref[...] = _dot(h, win_ref[:, OFF_V:OFF_G])
    gate_ref[...] = _silu(_dot(h, win_ref[:, OFF_G:OFF_U]))
    u = _gelu(_dot(h, win_ref[:, OFF_U:OFF_S]))
    s = _gelu(_dot(h, win_ref[:, OFF_S:D_IN]))
    for g in range(GROUPS):
        gs = slice(g * GD, (g + 1) * GD)
        sn = _layernorm(s[:, gs], lng_ref[:, gs], lnb_ref[:, gs])
        rows_ref[:, gs] = sn
        sgu_out_ref[:, gs] = u[:, gs] * (sgu_ref[0, g] * sn + sgu_ref[1, g])


def _mix_sample_in(x, gmix, win, cos2, sin2, lng, lnb, sgu_wb):
    m, d = x.shape
    shp = lambda n: jax.ShapeDtypeStruct((m, n), F32)
    vm = lambda: pl.BlockSpec(memory_space=pltpu.VMEM)
    return pl.pallas_call(
        _mix_sample_in_kernel,
        in_specs=[_smem_spec()] + [vm() for _ in range(7)],
        out_specs=[vm() for _ in range(6)],
        out_shape=[shp(Q_W), shp(Q_W), shp(V_W), shp(V_W), shp(U_W), shp(U_W)],
        compiler_params=pltpu.CompilerParams(vmem_limit_bytes=VMEM_LIMIT), name="mix_sample_in",
    )(sgu_wb, x, gmix, win, cos2, sin2, lng, lnb)


def _ret_sample_kernel(decay_ref, q_ref, k_ref, v_ref, s_ref, snew_ref, o_ref):
    bb = q_ref.shape[0]
    for hh in range(HEADS):
        ks = slice(hh * DK, (hh + 1) * DK)
        vs = slice(hh * DV, (hh + 1) * DV)
        qt = q_ref[:, ks].T
        kt = k_ref[:, ks].T
        gam = decay_ref[hh]
        for b in range(bb):
            st = s_ref[0, b, hh]
            sn = gam * st + kt[:, b:b + 1] * v_ref[b:b + 1, vs]
            snew_ref[0, b, hh] = sn
            o_ref[b:b + 1, vs] = jnp.sum(qt[:, b:b + 1] * sn, axis=0, keepdims=True)


def _ret_sample(decay, q, k, v, state):
    m = q.shape[0]
    bb = BB_RET
    return pl.pallas_call(
        _ret_sample_kernel, grid=(m // bb,),
        in_specs=[_smem_spec(),
                  pl.BlockSpec((bb, Q_W), lambda i: (i, 0)),
                  pl.BlockSpec((bb, Q_W), lambda i: (i, 0)),
                  pl.BlockSpec((bb, V_W), lambda i: (i, 0)),
                  pl.BlockSpec((1, bb, HEADS, DK, DV), lambda i: (0, i, 0, 0, 0))],
        out_specs=[pl.BlockSpec((1, bb, HEADS, DK, DV), lambda i: (0, i, 0, 0, 0)),
                   pl.BlockSpec((bb, V_W), lambda i: (i, 0))],
        out_shape=[jax.ShapeDtypeStruct(state.shape, F32), jax.ShapeDtypeStruct((m, V_W), F32)],
        compiler_params=_params("arbitrary"), name="ret_sample",
    )(decay, q, k, v, state)


def _mix_sample_out_kernel(x_ref, o_ref, gate_ref, sgu_ref, gn_ref, wout_ref, x1_ref):
    o = o_ref[...]
    parts = []
    for hh in range(HEADS):
        vs = slice(hh * DV, (hh + 1) * DV)
        parts.append((gate_ref[:, vs] * _groupnorm(o[:, vs], gn_ref[:, vs])).astype(BF16))
    y = _dot(jnp.concatenate(parts, axis=-1), wout_ref[0:V_W, :])
    y = y + _dot(sgu_ref[...].astype(BF16), wout_ref[V_W:V_W + U_W, :])
    x1_ref[...] = x_ref[...] + y


def _mix_sample_out(x, o, gate, sgu_out, gn, wout):
    m, d = x.shape
    vm = lambda: pl.BlockSpec(memory_space=pltpu.VMEM)
    return pl.pallas_call(
        _mix_sample_out_kernel, in_specs=[vm() for _ in range(6)], out_specs=vm(),
        out_shape=jax.ShapeDtypeStruct((m, d), F32),
        compiler_params=pltpu.CompilerParams(vmem_limit_bytes=VMEM_LIMIT), name="mix_sample_out",
    )(x, o, gate, sgu_out, gn, wout)


def _conv_sample_in_kernel(x_ref, g_ref, win_ref, bin_ref, xg_ref):
    xg_ref[...] = _glu_in(x_ref[...], g_ref[...], win_ref, bin_ref)


def _conv_sample_in(x, g, win, bin_):
    m, d = x.shape
    vm = lambda: pl.BlockSpec(memory_space=pltpu.VMEM)
    return pl.pallas_call(
        _conv_sample_in_kernel, in_specs=[vm() for _ in range(4)], out_specs=vm(),
        out_shape=jax.ShapeDtypeStruct((m, d), F32),
        compiler_params=pltpu.CompilerParams(vmem_limit_bytes=VMEM_LIMIT), name="conv_sample_in",
    )(x, g, win, bin_)


def _conv_sample_state_kernel(hist_ref, xg_ref, dww_ref, dwb_ref, newhist_ref, conv_ref):
    xg = xg_ref[...]
    acc = dwb_ref[...] + xg * dww_ref[HIST:CONV_WIDTH, :]
    for j in range(HIST):
        acc = acc + hist_ref[j] * dww_ref[j:j + 1, :]
    conv_ref[...] = acc
    for j in range(HIST - 1):
        newhist_ref[j] = hist_ref[j + 1]
    newhist_ref[HIST - 1] = xg


def _conv_sample_state(hist_t, xg, dww, dwb):
    _, m, d = hist_t.shape
    bb = BB_CONV
    return pl.pallas_call(
        _conv_sample_state_kernel, grid=(m // bb,),
        in_specs=[pl.BlockSpec((HIST, bb, d), lambda i: (0, i, 0)),
                  pl.BlockSpec((bb, d), lambda i: (i, 0)),
                  _const_spec((CONV_WIDTH, d)), _const_spec((1, d))],
        out_specs=[pl.BlockSpec((HIST, bb, d), lambda i: (0, i, 0)),
                   pl.BlockSpec((bb, d), lambda i: (i, 0))],
        out_shape=[jax.ShapeDtypeStruct(hist_t.shape, F32), jax.ShapeDtypeStruct((m, d), F32)],
        compiler_params=_params("arbitrary"), name="conv_sample_state",
    )(hist_t, xg, dww, dwb)


def _conv_sample_out_kernel(x_ref, conv_ref, lng_ref, lnb_ref, wout_ref, bout_ref, o_ref):
    c = _silu(_layernorm(conv_ref[...], lng_ref[...], lnb_ref[...])).astype(BF16)
    o_ref[...] = x_ref[...] + _dot(c, wout_ref[...]) + bout_ref[...]


def _conv_sample_out(x, conv, lng, lnb, wout, bout):
    m, d = x.shape
    vm = lambda: pl.BlockSpec(memory_space=pltpu.VMEM)
    return pl.pallas_call(
        _conv_sample_out_kernel, in_specs=[vm() for _ in range(6)], out_specs=vm(),
        out_shape=jax.ShapeDtypeStruct((m, d), F32),
        compiler_params=pltpu.CompilerParams(vmem_limit_bytes=VMEM_LIMIT), name="conv_sample_out",
    )(x, conv, lng, lnb, wout, bout)


def _rope_tables(pos):
    half = DK // 2
    inv = ROPE_BASE ** (-jnp.arange(half, dtype=F32) / half)
    ang = pos.astype(F32)[:, None] * inv[None, :]
    cos = jnp.cos(ang)
    sin = jnp.sin(ang)
    return jnp.concatenate([cos, cos], axis=-1), jnp.concatenate([-sin, sin], axis=-1)


def _decay_tables(c):
    log_g = jnp.log1p(-(2.0 ** (-5.0 - jnp.arange(HEADS, dtype=F32))))
    idx = jnp.arange(c, dtype=F32)
    diff = idx[:, None] - idx[None, :]
    dmask = jnp.where(diff >= 0, jnp.exp(jnp.maximum(diff, 0.0)[None] * log_g[:, None, None]), 0.0)
    xi = jnp.exp((idx[None, :] + 1.0) * log_g[:, None])
    zeta = jnp.exp((c - 1.0 - idx[None, :]) * log_g[:, None])
    chunk_decay = jnp.exp(c * log_g)
    return dmask, xi, zeta, chunk_decay


def kernel(x_prompt, x_sample, state_ret, state_conv, norm_mix_g, norm_ffn_g, final_norm_g, ab_w_in, ab_w_out, ret_gn_g, sgu_ln_g, sgu_ln_b, sgu_w, sgu_b, conv_w_in, conv_b_in, conv_dw_w, conv_dw_b, conv_ln_g, conv_ln_b, conv_w_out, conv_b_out, ffn_w1, ffn_w2):
    b, t, d = x_prompt.shape
    m_s = x_sample.shape[0]
    row = lambda a: a.reshape(1, -1)

    win = ab_w_in[0].astype(BF16)
    wout = ab_w_out[0].astype(BF16)
    cwin = conv_w_in[0].astype(BF16)
    cwout = conv_w_out[0].astype(BF16)
    gmix = [row(norm_mix_g[l]) for l in range(2)]
    gffn = [row(norm_ffn_g[l]) for l in range(2)]
    gfin = row(final_norm_g)
    gn, lng, lnb = row(ret_gn_g[0]), row(sgu_ln_g[0]), row(sgu_ln_b[0])
    cbin, cdwb, clng, clnb, cbout = (row(conv_b_in[0]), row(conv_dw_b[0]), row(conv_ln_g[0]),
                                     row(conv_ln_b[0]), row(conv_b_out[0]))

    cos_p, sin_p = _rope_tables(jnp.arange(t))
    dmask, xi, zeta, cd = _decay_tables(CHUNK)
    xi_b = jnp.broadcast_to(xi[:, :, None], (HEADS, CHUNK, DV))
    zeta_b = jnp.broadcast_to(zeta[:, :, None], (HEADS, CHUNK, DK))
    bsgu_b = jnp.broadcast_to(sgu_b[0][:, :CHUNK, None], (GROUPS, CHUNK, GD))
    x1, ret_state_p = _mix_prompt(x_prompt, gmix[0], win, wout, cos_p, sin_p, dmask, xi_b, zeta_b, cd,
                                  gn, lng, lnb, sgu_w[0], bsgu_b)

    xs = x_sample.reshape(m_s, d)
    cos_s, sin_s = _rope_tables(PAST_LEN + jnp.arange(1))
    _, _, _, decay1 = _decay_tables(1)
    sgu_wb = jnp.stack([sgu_w[0][:, 0, 0], sgu_b[0][:, 0]])
    q, k, v, gate, sgu_out, rows = _mix_sample_in(xs, gmix[0], win, cos_s, sin_s, lng, lnb, sgu_wb)
    ret_state_s, o = _ret_sample(decay1, q, k, v, state_ret)
    xs1 = _mix_sample_out(xs, o, gate, sgu_out, gn, wout)

    x2, xs2 = _ffn(x1.reshape(b * t, d), xs1, gffn[0], ffn_w1, ffn_w2, gfin, 0, False)

    x3, conv_state_p = _conv_prompt(x2.reshape(b, t, d), gmix[1], cwin, cbin, conv_dw_w[0], cdwb, clng, clnb,
                                    cwout, cbout)
    xg = _conv_sample_in(xs2, gmix[1], cwin, cbin)
    hist_t = jnp.transpose(state_conv[0], (1, 0, 2))
    newhist_t, conv = _conv_sample_state(hist_t, xg, conv_dw_w[0], cdwb)
    conv_state_s = jnp.transpose(newhist_t, (1, 0, 2))[None]
    xs3 = _conv_sample_out(xs2, conv, clng, clnb, cwout, cbout)

    y_prompt, y_sample = _ffn(x3, xs3, gffn[1], ffn_w1, ffn_w2, gfin, 1, True)

    return (y_prompt.reshape(b, t, d), y_sample.reshape(m_s, 1, d), ret_state_p, ret_state_s,
            rows.reshape(1, m_s, 1, U_W), conv_state_p, conv_state_s)
```

```python
import functools

import jax
import jax.numpy as jnp
from jax import lax
from jax.experimental import pallas as pl
from jax.experimental.pallas import tpu as pltpu

F32 = jnp.float32
BF16 = jnp.bfloat16

D_MODEL = 1024
HEADS = 4
DK = 128
DV = 256
GROUPS = 4
GD = 256
CHUNK = 128
ROPE_BASE = 10000.0
CONV_WIDTH = 31
HIST = CONV_WIDTH - 1
D_FF = 4 * D_MODEL
RMS_EPS = 1e-6
LN_EPS = 1e-5
PAST_LEN = 16384

Q_W = HEADS * DK
V_W = HEADS * DV
U_W = GROUPS * GD
OFF_V = 2 * Q_W
OFF_G = OFF_V + V_W
OFF_U = OFF_G + V_W
OFF_S = OFF_U + U_W
D_IN = OFF_S + U_W

SUBLANES = 8
VMEM_LIMIT = 56 * 1024 * 1024

TM_MIX = 512
TM_FFN = 512
TM_CONV = 512
CONV_RB = 64
CONV_LC = 128
HIST_PAD = 32
FFN_SLICES = 4
FFN_HS = D_FF // FFN_SLICES
L1_SLICES = 8
L1_FS = D_FF // L1_SLICES
BB_RET = 8
BB_CONV = 16


def _params(*sem):
    return pltpu.CompilerParams(dimension_semantics=sem, vmem_limit_bytes=VMEM_LIMIT)


def _const_spec(shape):
    n = len(shape)
    return pl.BlockSpec(shape, lambda *_: (0,) * n, pipeline_mode=pl.Buffered(1))


def _smem_spec():
    return pl.BlockSpec(memory_space=pltpu.SMEM)


def _rms(x, g):
    ms = jnp.mean(x * x, axis=-1, keepdims=True)
    return x * lax.rsqrt(ms + RMS_EPS) * g


def _layernorm(x, g, b):
    mu = jnp.mean(x, axis=-1, keepdims=True)
    xc = x - mu
    var = jnp.mean(xc * xc, axis=-1, keepdims=True)
    return xc * lax.rsqrt(var + LN_EPS) * g + b


def _groupnorm(x, g):
    mu = jnp.mean(x, axis=-1, keepdims=True)
    xc = x - mu
    var = jnp.mean(xc * xc, axis=-1, keepdims=True)
    return xc * lax.rsqrt(var + LN_EPS) * g


def _sigmoid(x):
    return 1.0 / (1.0 + jnp.exp(-x))


def _silu(x):
    return x * _sigmoid(x)


def _gelu(x):
    return 0.5 * x * (1.0 + lax.erf(x * 0.7071067811865476))


def _dot(a, b):
    return jnp.dot(a, b, preferred_element_type=F32)


def _dot_nt(a, b):
    return lax.dot_general(a, b, (((1,), (1,)), ((), ())), preferred_element_type=F32)


def _dot_tn(a, b):
    return lax.dot_general(a, b, (((0,), (0,)), ((), ())), preferred_element_type=F32)


def _rotary(blk, cos2, sin2):
    return blk * cos2 + pltpu.roll(blk, DK // 2, 1) * sin2


def _mix_prompt_kernel(cd_ref, x_ref, gmix_ref, win_ref, wout_ref, cos_ref, sin_ref,
                       dmask_ref, xi_ref, zeta_ref, gn_ref, lng_ref, lnb_ref,
                       wsgu_ref, bsgu_ref,
                       x1_ref, state_ref,
                       q_s, k_s, kz_s, v_s, gate_s, u_s, sn_s, mix_s, inner_s, kv_s, sb_s):
    t = pl.program_id(1)
    tm = x_ref.shape[1]

    @pl.when(t == 0)
    def _():
        state_ref[...] = jnp.zeros_like(state_ref)

    x = x_ref[0]
    h = _rms(x, gmix_ref[...]).astype(BF16)

    cos2 = cos_ref[...]
    sin2 = sin_ref[...]
    n_chunks = tm // CHUNK

    def proj(off, g):
        return _dot(h, win_ref[:, off + g * GD:off + (g + 1) * GD])

    for pair in range(Q_W // GD):
        qq = proj(0, pair)
        kk = proj(Q_W, pair)
        for e in range(GD // DK):
            j = pair * (GD // DK) + e
            ks = slice(j * DK, (j + 1) * DK)
            q_s[:, ks] = _rotary(qq[:, e * DK:(e + 1) * DK], cos2, sin2).astype(BF16)
            kj = _rotary(kk[:, e * DK:(e + 1) * DK], cos2, sin2) * (DK ** -0.5)
            k_s[:, ks] = kj.astype(BF16)
            for c in range(n_chunks):
                rs = slice(c * CHUNK, (c + 1) * CHUNK)
                kz_s[rs, ks] = (kj[rs, :] * zeta_ref[j]).astype(BF16)
    for g in range(GROUPS):
        gs = slice(g * GD, (g + 1) * GD)
        v_s[:, gs] = proj(OFF_V, g).astype(BF16)
        gate_s[:, gs] = _silu(proj(OFF_G, g))
        u_s[:, gs] = _gelu(proj(OFF_U, g))
        sn_s[:, gs] = _layernorm(_gelu(proj(OFF_S, g)), lng_ref[:, gs], lnb_ref[:, gs]).astype(BF16)

    row = lax.broadcasted_iota(jnp.int32, (CHUNK, CHUNK), 0)
    col = lax.broadcasted_iota(jnp.int32, (CHUNK, CHUNK), 1)
    tril = row >= col

    pairs = [(c, hh) for c in range(n_chunks) for hh in range(HEADS)]
    for c, hh in pairs:
        rs = slice(c * CHUNK, (c + 1) * CHUNK)
        ks = slice(hh * DK, (hh + 1) * DK)
        inner = _dot_nt(q_s[rs, ks], k_s[rs, ks]) * dmask_ref[hh]
        inner_s[c * HEADS + hh] = inner.astype(BF16)
    for c, hh in pairs:
        rs = slice(c * CHUNK, (c + 1) * CHUNK)
        kv_s[c * HEADS + hh] = _dot_tn(kz_s[rs, hh * DK:(hh + 1) * DK], v_s[rs, hh * DV:(hh + 1) * DV])
    for hh in range(HEADS):
        st = state_ref[0, 0, hh]
        for c in range(n_chunks):
            sb_s[c * HEADS + hh] = st.astype(BF16)
            st = cd_ref[hh] * st + kv_s[c * HEADS + hh]
        state_ref[0, 0, hh] = st
    for c, hh in pairs:
        rs = slice(c * CHUNK, (c + 1) * CHUNK)
        vs = slice(hh * DV, (hh + 1) * DV)
        o = _dot(inner_s[c * HEADS + hh], v_s[rs, vs])
        o = o + _dot(q_s[rs, hh * DK:(hh + 1) * DK], sb_s[c * HEADS + hh]) * xi_ref[hh]
        on = _groupnorm(o, gn_ref[:, vs])
        mix_s[rs, vs] = (gate_s[rs, vs] * on).astype(BF16)

    for c in range(n_chunks):
        rs = slice(c * CHUNK, (c + 1) * CHUNK)
        for g in range(GROUPS):
            gs = slice(g * GD, (g + 1) * GD)
            wg = jnp.where(tril, wsgu_ref[g], 0.0).astype(BF16)
            mixed = _dot(wg, sn_s[rs, gs]) + bsgu_ref[g]
            mix_s[rs, V_W + g * GD:V_W + (g + 1) * GD] = (u_s[rs, gs] * mixed).astype(BF16)

    x1_ref[0] = x + _dot(mix_s[...], wout_ref[...])


def _mix_prompt(x, gmix, win, wout, cos2, sin2, dmask, xi_b, zeta_b, cd, gn, lng, lnb, wsgu, bsgu_b):
    b, t, d = x.shape
    tm = TM_MIX
    n_pairs = (tm // CHUNK) * HEADS
    grid = (b, t // tm)
    in_specs = [
        _smem_spec(),
        pl.BlockSpec((1, tm, d), lambda i, j: (i, j, 0)),
        _const_spec((1, d)),
        _const_spec((d, D_IN)),
        _const_spec((V_W + U_W, d)),
        pl.BlockSpec((tm, DK), lambda i, j: (j, 0)),
        pl.BlockSpec((tm, DK), lambda i, j: (j, 0)),
        _const_spec((HEADS, CHUNK, CHUNK)),
        _const_spec((HEADS, CHUNK, DV)),
        _const_spec((HEADS, CHUNK, DK)),
        _const_spec((1, V_W)),
        _const_spec((1, U_W)),
        _const_spec((1, U_W)),
        _const_spec((GROUPS, CHUNK, CHUNK)),
        _const_spec((GROUPS, CHUNK, GD)),
    ]
    out_specs = [
        pl.BlockSpec((1, tm, d), lambda i, j: (i, j, 0)),
        pl.BlockSpec((1, 1, HEADS, DK, DV), lambda i, j: (0, i, 0, 0, 0)),
    ]
    out_shape = [
        jax.ShapeDtypeStruct((b, t, d), F32),
        jax.ShapeDtypeStruct((1, b, HEADS, DK, DV), F32),
    ]
    scratch = [
        pltpu.VMEM((tm, Q_W), BF16),
        pltpu.VMEM((tm, Q_W), BF16),
        pltpu.VMEM((tm, Q_W), BF16),
        pltpu.VMEM((tm, V_W), BF16),
        pltpu.VMEM((tm, V_W), F32),
        pltpu.VMEM((tm, U_W), F32),
        pltpu.VMEM((tm, U_W), BF16),
        pltpu.VMEM((tm, V_W + U_W), BF16),
        pltpu.VMEM((n_pairs, CHUNK, CHUNK), BF16),
        pltpu.VMEM((n_pairs, DK, DV), F32),
        pltpu.VMEM((n_pairs, DK, DV), BF16),
    ]
    return pl.pallas_call(
        _mix_prompt_kernel, grid=grid, in_specs=in_specs, out_specs=out_specs,
        out_shape=out_shape, scratch_shapes=scratch,
        compiler_params=_params("arbitrary", "arbitrary"), name="mix_prompt",
    )(cd, x, gmix, win, wout, cos2, sin2, dmask, xi_b, zeta_b, gn, lng, lnb, wsgu, bsgu_b)


def _ffn_tile(x, g, gfin, w1_s, w2_s, final, before_slice=None):
    h = _rms(x, g).astype(BF16)
    acc = x
    for j in range(FFN_SLICES):
        if before_slice is not None:
            before_slice(j)
        js = slice(j * FFN_HS, (j + 1) * FFN_HS)
        f = _dot(h, w1_s[:, js])
        a = jnp.square(jnp.maximum(f, 0.0)).astype(BF16)
        acc = acc + _dot(a, w2_s[js, :])
    if final:
        acc = _rms(acc, gfin)
    return acc


def _ffn_kernel(xp_ref, xs_ref, g_ref, gfin_ref, w1_hbm, w2_hbm, yp_ref, ys_ref,
                w1_s, w2_s, st1, st2, sem, *, layer, final, n_tiles):
    i = pl.program_id(0)

    def slice_copies(j):
        slot = j % 2
        js = pl.ds(j * FFN_HS, FFN_HS)
        return (pltpu.make_async_copy(w1_hbm.at[layer, :, js], st1.at[slot], sem.at[0, slot]),
                pltpu.make_async_copy(w2_hbm.at[layer, js, :], st2.at[slot], sem.at[1, slot]))

    def fetch_slice(j):
        js = slice(j * FFN_HS, (j + 1) * FFN_HS)
        for c in slice_copies(j):
            c.wait()
        w1_s[:, js] = st1[j % 2].astype(BF16)
        w2_s[js, :] = st2[j % 2].astype(BF16)
        if j + 2 < FFN_SLICES:
            for c in slice_copies(j + 2):
                c.start()

    @pl.when(i == 0)
    def _():
        for j in range(min(2, FFN_SLICES)):
            for c in slice_copies(j):
                c.start()
        yp_ref[...] = _ffn_tile(xp_ref[...], g_ref[...], gfin_ref[...], w1_s, w2_s, final, fetch_slice)

    @pl.when(jnp.logical_and(i > 0, i < n_tiles))
    def _():
        yp_ref[...] = _ffn_tile(xp_ref[...], g_ref[...], gfin_ref[...], w1_s, w2_s, final)

    @pl.when(i == n_tiles)
    def _():
        ys_ref[...] = _ffn_tile(xs_ref[...], g_ref[...], gfin_ref[...], w1_s, w2_s, final)


def _ffn(xp, xs, g, w1, w2, gfin, layer, final):
    m, d = xp.shape
    ms = xs.shape[0]
    tm = TM_FFN
    n_tiles = m // tm
    tile = lambda i: (jnp.minimum(i, n_tiles - 1), 0)
    return pl.pallas_call(
        functools.partial(_ffn_kernel, layer=layer, final=final, n_tiles=n_tiles),
        grid=(n_tiles + 1,),
        in_specs=[pl.BlockSpec((tm, d), tile), _const_spec((ms, d)),
                  _const_spec((1, d)), _const_spec((1, d)),
                  pl.BlockSpec(memory_space=pl.ANY), pl.BlockSpec(memory_space=pl.ANY)],
        out_specs=[pl.BlockSpec((tm, d), tile), pl.BlockSpec((ms, d), lambda i: (0, 0))],
        out_shape=[jax.ShapeDtypeStruct((m, d), F32), jax.ShapeDtypeStruct((ms, d), F32)],
        scratch_shapes=[pltpu.VMEM((d, D_FF), BF16), pltpu.VMEM((D_FF, d), BF16),
                        pltpu.VMEM((2, d, FFN_HS), F32), pltpu.VMEM((2, FFN_HS, d), F32),
                        pltpu.SemaphoreType.DMA((2, 2))],
        compiler_params=_params("arbitrary"), name="ffn_final" if final else "ffn",
    )(xp, xs, g, gfin, w1, w2)


def _glu_in(x, g, win_ref, bin_ref):
    h = _rms(x, g).astype(BF16)
    pieces = []
    for c in range(D_MODEL // GD):
        ca = slice(c * GD, (c + 1) * GD)
        cg = slice(D_MODEL + c * GD, D_MODEL + (c + 1) * GD)
        a = _dot(h, win_ref[:, ca]) + bin_ref[:, ca]
        gt = _dot(h, win_ref[:, cg]) + bin_ref[:, cg]
        pieces.append(a * _sigmoid(gt))
    return jnp.concatenate(pieces, axis=-1)


def _conv_block(r, full_s, conv_s, dww_ref, dwb_ref):
    off0 = HIST_PAD - HIST
    base = pl.multiple_of(r * CONV_RB, CONV_RB)
    for lc in range(D_MODEL // CONV_LC):
        ls = slice(lc * CONV_LC, (lc + 1) * CONV_LC)
        win = full_s[pl.ds(base, CONV_RB + HIST_PAD), ls]
        out = dwb_ref[:, ls]
        for p in range(SUBLANES):
            rows = CONV_RB + (SUBLANES if p else 0)
            z = None
            for a in range(HIST_PAD // SUBLANES + 1):
                m = SUBLANES * a + p
                if m < off0 or m > HIST_PAD:
                    continue
                term = dww_ref[m - off0:m - off0 + 1, ls] * win[SUBLANES * a:SUBLANES * a + rows, :]
                z = term if z is None else z + term
            out = out + z[p:p + CONV_RB, :]
        conv_s[pl.ds(base, CONV_RB), ls] = out


def _layer1_kernel(x_ref, xs_ref, gmix_ref, win_ref, bin_ref, dww_ref, dwb_ref, lng_ref, lnb_ref,
                   wout_ref, bout_ref, gffn_ref, gfin_ref, w1_hbm, w2_hbm,
                   y_ref, ys_ref, hist_ref,
                   full_s, conv_s, xres_s, acc_s, hffn_s, w1_s, w2_s, st1, st2, sem,
                   *, layer, n_tiles, tiles_per_seq):
    s = pl.program_id(0)
    tm = x_ref.shape[0]
    off0 = HIST_PAD - HIST
    n_blocks = tm // CONV_RB

    def front():
        x = x_ref[...]
        full_s[HIST_PAD:HIST_PAD + tm, :] = _glu_in(x, gmix_ref[...], win_ref, bin_ref)
        return x

    def back():
        c = _silu(_layernorm(conv_s[...], lng_ref[...], lnb_ref[...])).astype(BF16)
        x3 = xres_s[...] + _dot(c, wout_ref[...]) + bout_ref[...]
        acc_s[...] = x3
        hffn_s[...] = _rms(x3, gffn_ref[...]).astype(BF16)

    def chunk_copies(j):
        slot = j % 2
        js = pl.ds(j * L1_FS, L1_FS)
        return (pltpu.make_async_copy(w1_hbm.at[layer, :, js], st1.at[slot], sem.at[0, slot]),
                pltpu.make_async_copy(w2_hbm.at[layer, js, :], st2.at[slot], sem.at[1, slot]))

    def fetch_chunk(j):
        for c in chunk_copies(j):
            c.wait()
        w1_s[j] = st1[j % 2].astype(BF16)
        w2_s[j] = st2[j % 2].astype(BF16)
        if j + 2 < L1_SLICES:
            for c in chunk_copies(j + 2):
                c.start()

    def ffn_slice(i, h):
        f = _dot(h, w1_s[i])
        a = jnp.square(jnp.maximum(f, 0.0)).astype(BF16)
        return _dot(a, w2_s[i])

    def conv_ffn_loop(with_conv, with_ffn):
        def body(i, carry):
            if with_ffn:
                acc_s[...] += ffn_slice(i, hffn_s[...])
            if with_conv:
                _conv_block(i, full_s, conv_s, dww_ref, dwb_ref)
            return carry
        lax.fori_loop(0, n_blocks, body, 0)

    @pl.when(s % tiles_per_seq == 0)
    def _():
        full_s[0:HIST_PAD, :] = jnp.zeros((HIST_PAD, D_MODEL), F32)

    @pl.when(s == 0)
    def _():
        for j in range(2):
            for c in chunk_copies(j):
                c.start()
        xres_s[...] = front()
        for r in range(n_blocks):
            _conv_block(r, full_s, conv_s, dww_ref, dwb_ref)
            fetch_chunk(r)

    @pl.when(jnp.logical_and(s > 0, s < n_tiles))
    def _():
        back()
        xres_s[...] = front()
        conv_ffn_loop(True, True)

    @pl.when(s == n_tiles)
    def _():
        back()
        conv_ffn_loop(False, True)

    @pl.when(jnp.logical_and(s > 0, s <= n_tiles))
    def _():
        y_ref[...] = _rms(acc_s[...], gfin_ref[...])

    @pl.when(s < n_tiles)
    def _():
        @pl.when(s % tiles_per_seq == tiles_per_seq - 1)
        def _():
            hist_ref[0, 0] = full_s[tm + off0:tm + HIST_PAD, :]

        full_s[0:HIST_PAD, :] = full_s[tm:tm + HIST_PAD, :]

    @pl.when(s == n_tiles + 1)
    def _():
        xs = xs_ref[...]
        h = _rms(xs, gffn_ref[...]).astype(BF16)
        acc = xs
        for j in range(L1_SLICES):
            acc = acc + ffn_slice(j, h)
        ys_ref[...] = _rms(acc, gfin_ref[...])


def _layer1(x, xs, seq_len, gmix, win, bin_, dww, dwb, lng, lnb, wout, bout, gffn, gfin, w1, w2, layer):
    m, d = x.shape
    ms = xs.shape[0]
    tm = TM_CONV
    n_tiles = m // tm
    tiles_per_seq = seq_len // tm
    assert tm // CONV_RB == L1_SLICES
    in_specs = [
        pl.BlockSpec((tm, d), lambda s: (jnp.minimum(s, n_tiles - 1), 0)), _const_spec((ms, d)),
        _const_spec((1, d)), _const_spec((d, 2 * d)), _const_spec((1, 2 * d)),
        _const_spec((CONV_WIDTH, d)), _const_spec((1, d)), _const_spec((1, d)), _const_spec((1, d)),
        _const_spec((d, d)), _const_spec((1, d)), _const_spec((1, d)), _const_spec((1, d)),
        pl.BlockSpec(memory_space=pl.ANY), pl.BlockSpec(memory_space=pl.ANY),
    ]
    out_specs = [
        pl.BlockSpec((tm, d), lambda s: (jnp.clip(s - 1, 0, n_tiles - 1), 0)),
        pl.BlockSpec((ms, d), lambda s: (0, 0)),
        pl.BlockSpec((1, 1, HIST, d), lambda s: (0, jnp.minimum(s, n_tiles - 1) // tiles_per_seq, 0, 0)),
    ]
    out_shape = [jax.ShapeDtypeStruct((m, d), F32), jax.ShapeDtypeStruct((ms, d), F32),
                 jax.ShapeDtypeStruct((1, m // seq_len, HIST, d), F32)]
    scratch = [
        pltpu.VMEM((tm + HIST_PAD, d), F32),
        pltpu.VMEM((tm, d), F32),
        pltpu.VMEM((tm, d), F32),
        pltpu.VMEM((tm, d), F32),
        pltpu.VMEM((tm, d), BF16),
        pltpu.VMEM((L1_SLICES, d, L1_FS), BF16),
        pltpu.VMEM((L1_SLICES, L1_FS, d), BF16),
        pltpu.VMEM((2, d, L1_FS), F32), pltpu.VMEM((2, L1_FS, d), F32),
        pltpu.SemaphoreType.DMA((2, 2)),
    ]
    return pl.pallas_call(
        functools.partial(_layer1_kernel, layer=layer, n_tiles=n_tiles, tiles_per_seq=tiles_per_seq),
        grid=(n_tiles + 2,), in_specs=in_specs, out_specs=out_specs,
        out_shape=out_shape, scratch_shapes=scratch,
        compiler_params=_params("arbitrary"), name="layer1",
    )(x, xs, gmix, win, bin_, dww, dwb, lng, lnb, wout, bout, gffn, gfin, w1, w2)


def _mix_sample_in_kernel(sgu_ref, x_ref, gmix_ref, win_ref, cos_ref, sin_ref, lng_ref, lnb_ref,
                          q_ref, k_ref, v_ref, gate_ref, sgu_out_ref, rows_ref):
    h = _rms(x_ref[...], gmix_ref[...]).astype(BF16)
    qk = _dot(h, win_ref[:, 0:OFF_V])
    cos2 = cos_ref[...]
    sin2 = sin_ref[...]
    for j in range(HEADS):
        q_ref[:, j * DK:(j + 1) * DK] = _rotary(qk[:, j * DK:(j + 1) * DK], cos2, sin2)
        kj = qk[:, Q_W + j * DK:Q_W + (j + 1) * DK]
        k_ref[:, j * DK:(j + 1) * DK] = _rotary(kj, cos2, sin2) * (DK ** -0.5)
    v_ref[...] = _dot(h, win_ref[:, OFF_V:OFF_G])
    gate_ref[...] = _silu(_dot(h, win_ref[:, OFF_G:OFF_U]))
    u = _gelu(_dot(h, win_ref[:, OFF_U:OFF_S]))
    s = _gelu(_dot(h, win_ref[:, OFF_S:D_IN]))
    for g in range(GROUPS):
        gs = slice(g * GD, (g + 1) * GD)
        sn = _layernorm(s[:, gs], lng_ref[:, gs], lnb_ref[:, gs])
        rows_ref[:, gs] = sn
        sgu_out_ref[:, gs] = u[:, gs] * (sgu_ref[0, g] * sn + sgu_ref[1, g])


def _mix_sample_in(x, gmix, win, cos2, sin2, lng, lnb, sgu_wb):
    m, d = x.shape
    shp = lambda n: jax.ShapeDtypeStruct((m, n), F32)
    vm = lambda: pl.BlockSpec(memory_space=pltpu.VMEM)
    return pl.pallas_call(
        _mix_sample_in_kernel,
        in_specs=[_smem_spec()] + [vm() for _ in range(7)],
        out_specs=[vm() for _ in range(6)],
        out_shape=[shp(Q_W), shp(Q_W), shp(V_W), shp(V_W), shp(U_W), shp(U_W)],
        compiler_params=pltpu.CompilerParams(vmem_limit_bytes=VMEM_LIMIT), name="mix_sample_in",
    )(sgu_wb, x, gmix, win, cos2, sin2, lng, lnb)


def _ret_sample_kernel(decay_ref, q_ref, k_ref, v_ref, s_ref, snew_ref, o_ref):
    bb = q_ref.shape[0]
    for hh in range(HEADS):
        ks = slice(hh * DK, (hh + 1) * DK)
        vs = slice(hh * DV, (hh + 1) * DV)
        qt = q_ref[:, ks].T
        kt = k_ref[:, ks].T
        gam = decay_ref[hh]
        for b in range(bb):
            st = s_ref[0, b, hh]
            sn = gam * st + kt[:, b:b + 1] * v_ref[b:b + 1, vs]
            snew_ref[0, b, hh] = sn
            o_ref[b:b + 1, vs] = jnp.sum(qt[:, b:b + 1] * sn, axis=0, keepdims=True)


def _ret_sample(decay, q, k, v, state):
    m = q.shape[0]
    bb = BB_RET
    return pl.pallas_call(
        _ret_sample_kernel, grid=(m // bb,),
        in_specs=[_smem_spec(),
                  pl.BlockSpec((bb, Q_W), lambda i: (i, 0)),
                  pl.BlockSpec((bb, Q_W), lambda i: (i, 0)),
                  pl.BlockSpec((bb, V_W), lambda i: (i, 0)),
                  pl.BlockSpec((1, bb, HEADS, DK, DV), lambda i: (0, i, 0, 0, 0))],
        out_specs=[pl.BlockSpec((1, bb, HEADS, DK, DV), lambda i: (0, i, 0, 0, 0)),
                   pl.BlockSpec((bb, V_W), lambda i: (i, 0))],
        out_shape=[jax.ShapeDtypeStruct(state.shape, F32), jax.ShapeDtypeStruct((m, V_W), F32)],
        compiler_params=_params("arbitrary"), name="ret_sample",
    )(decay, q, k, v, state)


def _mix_sample_out_kernel(x_ref, o_ref, gate_ref, sgu_ref, gn_ref, wout_ref, x1_ref):
    o = o_ref[...]
    parts = []
    for hh in range(HEADS):
        vs = slice(hh * DV, (hh + 1) * DV)
        parts.append((gate_ref[:, vs] * _groupnorm(o[:, vs], gn_ref[:, vs])).astype(BF16))
    y = _dot(jnp.concatenate(parts, axis=-1), wout_ref[0:V_W, :])
    y = y + _dot(sgu_ref[...].astype(BF16), wout_ref[V_W:V_W + U_W, :])
    x1_ref[...] = x_ref[...] + y


def _mix_sample_out(x, o, gate, sgu_out, gn, wout):
    m, d = x.shape
    vm = lambda: pl.BlockSpec(memory_space=pltpu.VMEM)
    return pl.pallas_call(
        _mix_sample_out_kernel, in_specs=[vm() for _ in range(6)], out_specs=vm(),
        out_shape=jax.ShapeDtypeStruct((m, d), F32),
        compiler_params=pltpu.CompilerParams(vmem_limit_bytes=VMEM_LIMIT), name="mix_sample_out",
    )(x, o, gate, sgu_out, gn, wout)


def _conv_sample_in_kernel(x_ref, g_ref, win_ref, bin_ref, xg_ref):
    xg_ref[...] = _glu_in(x_ref[...], g_ref[...], win_ref, bin_ref)


def _conv_sample_in(x, g, win, bin_):
    m, d = x.shape
    vm = lambda: pl.BlockSpec(memory_space=pltpu.VMEM)
    return pl.pallas_call(
        _conv_sample_in_kernel, in_specs=[vm() for _ in range(4)], out_specs=vm(),
        out_shape=jax.ShapeDtypeStruct((m, d), F32),
        compiler_params=pltpu.CompilerParams(vmem_limit_bytes=VMEM_LIMIT), name="conv_sample_in",
    )(x, g, win, bin_)


def _conv_sample_state_kernel(hist_ref, xg_ref, dww_ref, dwb_ref, newhist_ref, conv_ref):
    xg = xg_ref[...]
    acc = dwb_ref[...] + xg * dww_ref[HIST:CONV_WIDTH, :]
    for j in range(HIST):
        acc = acc + hist_ref[j] * dww_ref[j:j + 1, :]
    conv_ref[...] = acc
    for j in range(HIST - 1):
        newhist_ref[j] = hist_ref[j + 1]
    newhist_ref[HIST - 1] = xg


def _conv_sample_state(hist_t, xg, dww, dwb):
    _, m, d = hist_t.shape
    bb = BB_CONV
    return pl.pallas_call(
        _conv_sample_state_kernel, grid=(m // bb,),
        in_specs=[pl.BlockSpec((HIST, bb, d), lambda i: (0, i, 0)),
                  pl.BlockSpec((bb, d), lambda i: (i, 0)),
                  _const_spec((CONV_WIDTH, d)), _const_spec((1, d))],
        out_specs=[pl.BlockSpec((HIST, bb, d), lambda i: (0, i, 0)),
                   pl.BlockSpec((bb, d), lambda i: (i, 0))],
        out_shape=[jax.ShapeDtypeStruct(hist_t.shape, F32), jax.ShapeDtypeStruct((m, d), F32)],
        compiler_params=_params("arbitrary"), name="conv_sample_state",
    )(hist_t, xg, dww, dwb)


def _conv_sample_out_kernel(x_ref, conv_ref, lng_ref, lnb_ref, wout_ref, bout_ref, o_ref):
    c = _silu(_layernorm(conv_ref[...], lng_ref[...], lnb_ref[...])).astype(BF16)
    o_ref[...] = x_ref[...] + _dot(c, wout_ref[...]) + bout_ref[...]


def _conv_sample_out(x, conv, lng, lnb, wout, bout):
    m, d = x.shape
    vm = lambda: pl.BlockSpec(memory_space=pltpu.VMEM)
    return pl.pallas_call(
        _conv_sample_out_kernel, in_specs=[vm() for _ in range(6)], out_specs=vm(),
        out_shape=jax.ShapeDtypeStruct((m, d), F32),
        compiler_params=pltpu.CompilerParams(vmem_limit_bytes=VMEM_LIMIT), name="conv_sample_out",
    )(x, conv, lng, lnb, wout, bout)


def _rope_tables(pos):
    half = DK // 2
    inv = ROPE_BASE ** (-jnp.arange(half, dtype=F32) / half)
    ang = pos.astype(F32)[:, None] * inv[None, :]
    cos = jnp.cos(ang)
    sin = jnp.sin(ang)
    return jnp.concatenate([cos, cos], axis=-1), jnp.concatenate([-sin, sin], axis=-1)


def _decay_tables(c):
    log_g = jnp.log1p(-(2.0 ** (-5.0 - jnp.arange(HEADS, dtype=F32))))
    idx = jnp.arange(c, dtype=F32)
    diff = idx[:, None] - idx[None, :]
    dmask = jnp.where(diff >= 0, jnp.exp(jnp.maximum(diff, 0.0)[None] * log_g[:, None, None]), 0.0)
    xi = jnp.exp((idx[None, :] + 1.0) * log_g[:, None])
    zeta = jnp.exp((c - 1.0 - idx[None, :]) * log_g[:, None])
    chunk_decay = jnp.exp(c * log_g)
    return dmask, xi, zeta, chunk_decay


def kernel(x_prompt, x_sample, state_ret, state_conv, norm_mix_g, norm_ffn_g, final_norm_g, ab_w_in, ab_w_out, ret_gn_g, sgu_ln_g, sgu_ln_b, sgu_w, sgu_b, conv_w_in, conv_b_in, conv_dw_w, conv_dw_b, conv_ln_g, conv_ln_b, conv_w_out, conv_b_out, ffn_w1, ffn_w2):
    b, t, d = x_prompt.shape
    m_s = x_sample.shape[0]
    row = lambda a: a.reshape(1, -1)

    win = ab_w_in[0].astype(BF16)
    wout = ab_w_out[0].astype(BF16)
    cwin = conv_w_in[0].astype(BF16)
    cwout = conv_w_out[0].astype(BF16)
    gmix = [row(norm_mix_g[l]) for l in range(2)]
    gffn = [row(norm_ffn_g[l]) for l in range(2)]
    gfin = row(final_norm_g)
    gn, lng, lnb = row(ret_gn_g[0]), row(sgu_ln_g[0]), row(sgu_ln_b[0])
    cbin, cdwb, clng, clnb, cbout = (row(conv_b_in[0]), row(conv_dw_b[0]), row(conv_ln_g[0]),
                                     row(conv_ln_b[0]), row(conv_b_out[0]))

    cos_p, sin_p = _rope_tables(jnp.arange(t))
    dmask, xi, zeta, cd = _decay_tables(CHUNK)
    xi_b = jnp.broadcast_to(xi[:, :, None], (HEADS, CHUNK, DV))
    zeta_b = jnp.broadcast_to(zeta[:, :, None], (HEADS, CHUNK, DK))
    bsgu_b = jnp.broadcast_to(sgu_b[0][:, :CHUNK, None], (GROUPS, CHUNK, GD))
    x1, ret_state_p = _mix_prompt(x_prompt, gmix[0], win, wout, cos_p, sin_p, dmask, xi_b, zeta_b, cd,
                                  gn, lng, lnb, sgu_w[0], bsgu_b)

    xs = x_sample.reshape(m_s, d)
    cos_s, sin_s = _rope_tables(PAST_LEN + jnp.arange(1))
    _, _, _, decay1 = _decay_tables(1)
    sgu_wb = jnp.stack([sgu_w[0][:, 0, 0], sgu_b[0][:, 0]])
    q, k, v, gate, sgu_out, rows = _mix_sample_in(xs, gmix[0], win, cos_s, sin_s, lng, lnb, sgu_wb)
    ret_state_s, o = _ret_sample(decay1, q, k, v, state_ret)
    xs1 = _mix_sample_out(xs, o, gate, sgu_out, gn, wout)

    x2, xs2 = _ffn(x1.reshape(b * t, d), xs1, gffn[0], ffn_w1, ffn_w2, gfin, 0, False)

    xg = _conv_sample_in(xs2, gmix[1], cwin, cbin)
    hist_t = jnp.transpose(state_conv[0], (1, 0, 2))
    newhist_t, conv = _conv_sample_state(hist_t, xg, conv_dw_w[0], cdwb)
    conv_state_s = jnp.transpose(newhist_t, (1, 0, 2))[None]
    xs3 = _conv_sample_out(xs2, conv, clng, clnb, cwout, cbout)

    y_prompt, y_sample, conv_state_p = _layer1(x2, xs3, t, gmix[1], cwin, cbin, conv_dw_w[0], cdwb, clng, clnb,
                                               cwout, cbout, gffn[1], gfin, ffn_w1, ffn_w2, 1)

    return (y_prompt.reshape(b, t, d), y_sample.reshape(m_s, 1, d), ret_state_p, ret_state_s,
            rows.reshape(1, m_s, 1, U_W), conv_state_p, conv_state_s)
```

```python
import functools

import jax
import jax.numpy as jnp
from jax import lax
from jax.experimental import pallas as pl
from jax.experimental.pallas import tpu as pltpu

F32 = jnp.float32
BF16 = jnp.bfloat16

D_MODEL = 1024
HEADS = 4
DK = 128
DV = 256
GROUPS = 4
GD = 256
CHUNK = 128
ROPE_BASE = 10000.0
CONV_WIDTH = 31
HIST = CONV_WIDTH - 1
D_FF = 4 * D_MODEL
RMS_EPS = 1e-6
LN_EPS = 1e-5
PAST_LEN = 16384

Q_W = HEADS * DK
V_W = HEADS * DV
U_W = GROUPS * GD
OFF_V = 2 * Q_W
OFF_G = OFF_V + V_W
OFF_U = OFF_G + V_W
OFF_S = OFF_U + U_W
D_IN = OFF_S + U_W

SUBLANES = 8
VMEM_LIMIT = 56 * 1024 * 1024

TM_MIX = 512
TM_FFN = 512
TM_CONV = 512
CONV_RB = 64
CONV_LC = 128
HIST_PAD = 32
FFN_SLICES = 4
FFN_HS = D_FF // FFN_SLICES
BB_RET = 8
HEAD_PAIR = 2
BB_CONV = 16


def _params(*sem):
    return pltpu.CompilerParams(dimension_semantics=sem, vmem_limit_bytes=VMEM_LIMIT)


def _const_spec(shape):
    n = len(shape)
    return pl.BlockSpec(shape, lambda *_: (0,) * n, pipeline_mode=pl.Buffered(1))


def _smem_spec():
    return pl.BlockSpec(memory_space=pltpu.SMEM)


def _rms(x, g):
    ms = jnp.mean(x * x, axis=-1, keepdims=True)
    return x * lax.rsqrt(ms + RMS_EPS) * g


def _layernorm(x, g, b):
    mu = jnp.mean(x, axis=-1, keepdims=True)
    xc = x - mu
    var = jnp.mean(xc * xc, axis=-1, keepdims=True)
    return xc * lax.rsqrt(var + LN_EPS) * g + b


def _groupnorm(x, g):
    mu = jnp.mean(x, axis=-1, keepdims=True)
    xc = x - mu
    var = jnp.mean(xc * xc, axis=-1, keepdims=True)
    return xc * lax.rsqrt(var + LN_EPS) * g


def _sigmoid(x):
    return 1.0 / (1.0 + jnp.exp(-x))


def _silu(x):
    return x * _sigmoid(x)


def _gelu(x):
    return 0.5 * x * (1.0 + lax.erf(x * 0.7071067811865476))


def _dot(a, b):
    return jnp.dot(a, b, preferred_element_type=F32)


def _dot_nt(a, b):
    return lax.dot_general(a, b, (((1,), (1,)), ((), ())), preferred_element_type=F32)


def _dot_tn(a, b):
    return lax.dot_general(a, b, (((0,), (0,)), ((), ())), preferred_element_type=F32)


def _rotary(blk, cos2, sin2):
    return blk * cos2 + pltpu.roll(blk, DK // 2, 1) * sin2


def _sample_state_update(step, decay_ref, qs_ref, ks_ref, vs_ref, sin_ref, sout_ref, os_ref):
    pair = step % (HEADS // HEAD_PAIR)
    for li in range(HEAD_PAIR):
        ks = slice(li * DK, (li + 1) * DK)
        vs = slice(li * DV, (li + 1) * DV)
        qt = qs_ref[0, :, ks].T
        kt = ks_ref[0, :, ks].T
        gam = decay_ref[pair * HEAD_PAIR + li]
        for b in range(BB_RET):
            st = sin_ref[0, b, li]
            sn = gam * st + kt[:, b:b + 1] * vs_ref[0, b:b + 1, vs]
            sout_ref[0, b, li] = sn
            os_ref[0, b:b + 1, vs] = jnp.sum(qt[:, b:b + 1] * sn, axis=0, keepdims=True)


def _mix_prompt_kernel(cd_ref, decay_ref, x_ref, gmix_ref, win_ref, wout_ref, cos_ref, sin_ref,
                       dmask_ref, xi_ref, zeta_ref, gn_ref, lng_ref, lnb_ref,
                       wsgu_ref, bsgu_ref, qs_ref, ks_ref, vs_ref, sstate_ref,
                       x1_ref, state_ref, sstate_out_ref, os_ref,
                       q_s, k_s, kz_s, v_s, gate_s, u_s, sn_s, mix_s, inner_s, kv_s, sb_s):
    t = pl.program_id(1)
    tm = x_ref.shape[1]

    @pl.when(t == 0)
    def _():
        state_ref[...] = jnp.zeros_like(state_ref)

    _sample_state_update(pl.program_id(0) * pl.num_programs(1) + t, decay_ref, qs_ref, ks_ref, vs_ref,
                         sstate_ref, sstate_out_ref, os_ref)

    x = x_ref[0]
    h = _rms(x, gmix_ref[...]).astype(BF16)

    cos2 = cos_ref[...]
    sin2 = sin_ref[...]
    n_chunks = tm // CHUNK

    def proj(off, g):
        return _dot(h, win_ref[:, off + g * GD:off + (g + 1) * GD])

    for pair in range(Q_W // GD):
        qq = proj(0, pair)
        kk = proj(Q_W, pair)
        for e in range(GD // DK):
            j = pair * (GD // DK) + e
            ks = slice(j * DK, (j + 1) * DK)
            q_s[:, ks] = _rotary(qq[:, e * DK:(e + 1) * DK], cos2, sin2).astype(BF16)
            kj = _rotary(kk[:, e * DK:(e + 1) * DK], cos2, sin2) * (DK ** -0.5)
            k_s[:, ks] = kj.astype(BF16)
            for c in range(n_chunks):
                rs = slice(c * CHUNK, (c + 1) * CHUNK)
                kz_s[rs, ks] = (kj[rs, :] * zeta_ref[j]).astype(BF16)
    for g in range(GROUPS):
        gs = slice(g * GD, (g + 1) * GD)
        v_s[:, gs] = proj(OFF_V, g).astype(BF16)
        gate_s[:, gs] = _silu(proj(OFF_G, g))
        u_s[:, gs] = _gelu(proj(OFF_U, g))
        sn_s[:, gs] = _layernorm(_gelu(proj(OFF_S, g)), lng_ref[:, gs], lnb_ref[:, gs]).astype(BF16)

    row = lax.broadcasted_iota(jnp.int32, (CHUNK, CHUNK), 0)
    col = lax.broadcasted_iota(jnp.int32, (CHUNK, CHUNK), 1)
    tril = row >= col

    pairs = [(c, hh) for c in range(n_chunks) for hh in range(HEADS)]
    for c, hh in pairs:
        rs = slice(c * CHUNK, (c + 1) * CHUNK)
        ks = slice(hh * DK, (hh + 1) * DK)
        inner = _dot_nt(q_s[rs, ks], k_s[rs, ks]) * dmask_ref[hh]
        inner_s[c * HEADS + hh] = inner.astype(BF16)
    for c, hh in pairs:
        rs = slice(c * CHUNK, (c + 1) * CHUNK)
        kv_s[c * HEADS + hh] = _dot_tn(kz_s[rs, hh * DK:(hh + 1) * DK], v_s[rs, hh * DV:(hh + 1) * DV])
    for hh in range(HEADS):
        st = state_ref[0, 0, hh]
        for c in range(n_chunks):
            sb_s[c * HEADS + hh] = st.astype(BF16)
            st = cd_ref[hh] * st + kv_s[c * HEADS + hh]
        state_ref[0, 0, hh] = st
    for c, hh in pairs:
        rs = slice(c * CHUNK, (c + 1) * CHUNK)
        vs = slice(hh * DV, (hh + 1) * DV)
        o = _dot(inner_s[c * HEADS + hh], v_s[rs, vs])
        o = o + _dot(q_s[rs, hh * DK:(hh + 1) * DK], sb_s[c * HEADS + hh]) * xi_ref[hh]
        on = _groupnorm(o, gn_ref[:, vs])
        mix_s[rs, vs] = (gate_s[rs, vs] * on).astype(BF16)

    for c in range(n_chunks):
        rs = slice(c * CHUNK, (c + 1) * CHUNK)
        for g in range(GROUPS):
            gs = slice(g * GD, (g + 1) * GD)
            wg = jnp.where(tril, wsgu_ref[g], 0.0).astype(BF16)
            mixed = _dot(wg, sn_s[rs, gs]) + bsgu_ref[g]
            mix_s[rs, V_W + g * GD:V_W + (g + 1) * GD] = (u_s[rs, gs] * mixed).astype(BF16)

    x1_ref[0] = x + _dot(mix_s[...], wout_ref[...])


def _mix_prompt(x, gmix, win, wout, cos2, sin2, dmask, xi_b, zeta_b, cd, gn, lng, lnb, wsgu, bsgu_b,
                decay1, qs, ks, vs, sstate):
    b, t, d = x.shape
    tm = TM_MIX
    nt = t // tm
    n_pairs = (tm // CHUNK) * HEADS
    grid = (b, nt)
    ms = qs.shape[0]
    n_hp = HEADS // HEAD_PAIR
    assert b * nt == (ms // BB_RET) * n_hp
    by_pair = lambda a, w: a.reshape(ms, n_hp, w).transpose(1, 0, 2)
    qs2, ks2, vs2 = by_pair(qs, HEAD_PAIR * DK), by_pair(ks, HEAD_PAIR * DK), by_pair(vs, HEAD_PAIR * DV)
    pair_blk = lambda i, j: ((i * nt + j) % n_hp, (i * nt + j) // n_hp, 0)
    state_blk = lambda i, j: (0, (i * nt + j) // n_hp, (i * nt + j) % n_hp, 0, 0)
    in_specs = [
        _smem_spec(),
        _smem_spec(),
        pl.BlockSpec((1, tm, d), lambda i, j: (i, j, 0)),
        _const_spec((1, d)),
        _const_spec((d, D_IN)),
        _const_spec((V_W + U_W, d)),
        pl.BlockSpec((tm, DK), lambda i, j: (j, 0)),
        pl.BlockSpec((tm, DK), lambda i, j: (j, 0)),
        _const_spec((HEADS, CHUNK, CHUNK)),
        _const_spec((HEADS, CHUNK, DV)),
        _const_spec((HEADS, CHUNK, DK)),
        _const_spec((1, V_W)),
        _const_spec((1, U_W)),
        _const_spec((1, U_W)),
        _const_spec((GROUPS, CHUNK, CHUNK)),
        _const_spec((GROUPS, CHUNK, GD)),
        pl.BlockSpec((1, BB_RET, HEAD_PAIR * DK), pair_blk),
        pl.BlockSpec((1, BB_RET, HEAD_PAIR * DK), pair_blk),
        pl.BlockSpec((1, BB_RET, HEAD_PAIR * DV), pair_blk),
        pl.BlockSpec((1, BB_RET, HEAD_PAIR, DK, DV), state_blk),
    ]
    out_specs = [
        pl.BlockSpec((1, tm, d), lambda i, j: (i, j, 0)),
        pl.BlockSpec((1, 1, HEADS, DK, DV), lambda i, j: (0, i, 0, 0, 0)),
        pl.BlockSpec((1, BB_RET, HEAD_PAIR, DK, DV), state_blk),
        pl.BlockSpec((1, BB_RET, HEAD_PAIR * DV), pair_blk),
    ]
    out_shape = [
        jax.ShapeDtypeStruct((b, t, d), F32),
        jax.ShapeDtypeStruct((1, b, HEADS, DK, DV), F32),
        jax.ShapeDtypeStruct(sstate.shape, F32),
        jax.ShapeDtypeStruct((n_hp, ms, HEAD_PAIR * DV), F32),
    ]
    scratch = [
        pltpu.VMEM((tm, Q_W), BF16),
        pltpu.VMEM((tm, Q_W), BF16),
        pltpu.VMEM((tm, Q_W), BF16),
        pltpu.VMEM((tm, V_W), BF16),
        pltpu.VMEM((tm, V_W), F32),
        pltpu.VMEM((tm, U_W), F32),
        pltpu.VMEM((tm, U_W), BF16),
        pltpu.VMEM((tm, V_W + U_W), BF16),
        pltpu.VMEM((n_pairs, CHUNK, CHUNK), BF16),
        pltpu.VMEM((n_pairs, DK, DV), F32),
        pltpu.VMEM((n_pairs, DK, DV), BF16),
    ]
    x1, state_p, state_s, o2 = pl.pallas_call(
        _mix_prompt_kernel, grid=grid, in_specs=in_specs, out_specs=out_specs,
        out_shape=out_shape, scratch_shapes=scratch,
        compiler_params=_params("arbitrary", "arbitrary"), name="mix_prompt",
    )(cd, decay1, x, gmix, win, wout, cos2, sin2, dmask, xi_b, zeta_b, gn, lng, lnb, wsgu, bsgu_b,
      qs2, ks2, vs2, sstate)
    return x1, state_p, state_s, o2.transpose(1, 0, 2).reshape(ms, V_W)


def _ffn_tile(x, g, gfin, w1_s, w2_s, final, before_slice=None):
    h = _rms(x, g).astype(BF16)
    acc = x
    for j in range(FFN_SLICES):
        if before_slice is not None:
            before_slice(j)
        js = slice(j * FFN_HS, (j + 1) * FFN_HS)
        f = _dot(h, w1_s[:, js])
        a = jnp.square(jnp.maximum(f, 0.0)).astype(BF16)
        acc = acc + _dot(a, w2_s[js, :])
    if final:
        acc = _rms(acc, gfin)
    return acc


def _ffn_kernel(xp_ref, xs_ref, g_ref, gfin_ref, w1_hbm, w2_hbm, yp_ref, ys_ref,
                w1_s, w2_s, st1, st2, sem, *, layer, final, n_tiles):
    i = pl.program_id(0)

    def slice_copies(j):
        slot = j % 2
        js = pl.ds(j * FFN_HS, FFN_HS)
        return (pltpu.make_async_copy(w1_hbm.at[layer, :, js], st1.at[slot], sem.at[0, slot]),
                pltpu.make_async_copy(w2_hbm.at[layer, js, :], st2.at[slot], sem.at[1, slot]))

    def fetch_slice(j):
        js = slice(j * FFN_HS, (j + 1) * FFN_HS)
        for c in slice_copies(j):
            c.wait()
        w1_s[:, js] = st1[j % 2].astype(BF16)
        w2_s[js, :] = st2[j % 2].astype(BF16)
        if j + 2 < FFN_SLICES:
            for c in slice_copies(j + 2):
                c.start()

    @pl.when(i == 0)
    def _():
        for j in range(min(2, FFN_SLICES)):
            for c in slice_copies(j):
                c.start()
        yp_ref[...] = _ffn_tile(xp_ref[...], g_ref[...], gfin_ref[...], w1_s, w2_s, final, fetch_slice)

    @pl.when(jnp.logical_and(i > 0, i < n_tiles))
    def _():
        yp_ref[...] = _ffn_tile(xp_ref[...], g_ref[...], gfin_ref[...], w1_s, w2_s, final)

    @pl.when(i == n_tiles)
    def _():
        ys_ref[...] = _ffn_tile(xs_ref[...], g_ref[...], gfin_ref[...], w1_s, w2_s, final)


def _ffn(xp, xs, g, w1, w2, gfin, layer, final):
    m, d = xp.shape
    ms = xs.shape[0]
    tm = TM_FFN
    n_tiles = m // tm
    tile = lambda i: (jnp.minimum(i, n_tiles - 1), 0)
    return pl.pallas_call(
        functools.partial(_ffn_kernel, layer=layer, final=final, n_tiles=n_tiles),
        grid=(n_tiles + 1,),
        in_specs=[pl.BlockSpec((tm, d), tile), _const_spec((ms, d)),
                  _const_spec((1, d)), _const_spec((1, d)),
                  pl.BlockSpec(memory_space=pl.ANY), pl.BlockSpec(memory_space=pl.ANY)],
        out_specs=[pl.BlockSpec((tm, d), tile), pl.BlockSpec((ms, d), lambda i: (0, 0))],
        out_shape=[jax.ShapeDtypeStruct((m, d), F32), jax.ShapeDtypeStruct((ms, d), F32)],
        scratch_shapes=[pltpu.VMEM((d, D_FF), BF16), pltpu.VMEM((D_FF, d), BF16),
                        pltpu.VMEM((2, d, FFN_HS), F32), pltpu.VMEM((2, FFN_HS, d), F32),
                        pltpu.SemaphoreType.DMA((2, 2))],
        compiler_params=_params("arbitrary"), name="ffn_final" if final else "ffn",
    )(xp, xs, g, gfin, w1, w2)


def _glu_in(x, g, win_ref, bin_ref):
    h = _rms(x, g).astype(BF16)
    pieces = []
    for c in range(D_MODEL // GD):
        ca = slice(c * GD, (c + 1) * GD)
        cg = slice(D_MODEL + c * GD, D_MODEL + (c + 1) * GD)
        a = _dot(h, win_ref[:, ca]) + bin_ref[:, ca]
        gt = _dot(h, win_ref[:, cg]) + bin_ref[:, cg]
        pieces.append(a * _sigmoid(gt))
    return jnp.concatenate(pieces, axis=-1)


def _conv_block(r, full_s, conv_s, dww_ref, dwb_ref):
    off0 = HIST_PAD - HIST
    base = pl.multiple_of(r * CONV_RB, CONV_RB)
    for lc in range(D_MODEL // CONV_LC):
        ls = slice(lc * CONV_LC, (lc + 1) * CONV_LC)
        win = full_s[pl.ds(base, CONV_RB + HIST_PAD), ls]
        out = dwb_ref[:, ls]
        for p in range(SUBLANES):
            rows = CONV_RB + (SUBLANES if p else 0)
            z = None
            for a in range(HIST_PAD // SUBLANES + 1):
                m = SUBLANES * a + p
                if m < off0 or m > HIST_PAD:
                    continue
                term = dww_ref[m - off0:m - off0 + 1, ls] * win[SUBLANES * a:SUBLANES * a + rows, :]
                z = term if z is None else z + term
            out = out + z[p:p + CONV_RB, :]
        conv_s[pl.ds(base, CONV_RB), ls] = out


def _conv_prompt_kernel(x_ref, xprev_ref, gmix_ref, win_ref, bin_ref, dww_ref, dwb_ref, lng_ref, lnb_ref,
                        wout_ref, bout_ref, shist_ref, sxg_ref,
                        o_ref, hist_ref, snewhist_ref, sconv_ref,
                        full_s, conv_s, *, n_tiles, tiles_per_seq, sample_every):
    s = pl.program_id(0)
    tm = x_ref.shape[0]
    off0 = HIST_PAD - HIST

    @pl.when(jnp.logical_and(s % sample_every == 0, s < n_tiles))
    def _():
        _conv_sample_state_update(shist_ref, sxg_ref, dww_ref, dwb_ref, snewhist_ref, sconv_ref)

    def front():
        full_s[HIST_PAD:HIST_PAD + tm, :] = _glu_in(x_ref[...], gmix_ref[...], win_ref, bin_ref)

    def back():
        c = _silu(_layernorm(conv_s[...], lng_ref[...], lnb_ref[...])).astype(BF16)
        o_ref[...] = xprev_ref[...] + _dot(c, wout_ref[...]) + bout_ref[...]

    @pl.when(s % tiles_per_seq == 0)
    def _():
        full_s[0:HIST_PAD, :] = jnp.zeros((HIST_PAD, D_MODEL), F32)

    @pl.when(s == 0)
    def _():
        front()

    @pl.when(jnp.logical_and(s > 0, s < n_tiles))
    def _():
        front()
        back()

    @pl.when(s == n_tiles)
    def _():
        back()

    @pl.when(s < n_tiles)
    def _():
        def body(r, carry):
            _conv_block(r, full_s, conv_s, dww_ref, dwb_ref)
            return carry
        lax.fori_loop(0, tm // CONV_RB, body, 0)

        @pl.when(s % tiles_per_seq == tiles_per_seq - 1)
        def _():
            hist_ref[0, 0] = full_s[tm + off0:tm + HIST_PAD, :]

        full_s[0:HIST_PAD, :] = full_s[tm:tm + HIST_PAD, :]


def _conv_prompt(x, gmix, win, bin_, dww, dwb, lng, lnb, wout, bout, shist_t, sxg):
    b, t, d = x.shape
    tm = TM_CONV
    n_tiles = (b * t) // tm
    tiles_per_seq = t // tm
    ms = sxg.shape[0]
    n_sblocks = ms // BB_CONV
    sample_every = n_tiles // n_sblocks
    assert sample_every * n_sblocks == n_tiles
    sblock = lambda s: jnp.minimum(s // sample_every, n_sblocks - 1)
    in_specs = [
        pl.BlockSpec((tm, d), lambda s: (jnp.minimum(s, n_tiles - 1), 0)),
        pl.BlockSpec((tm, d), lambda s: (jnp.maximum(s - 1, 0), 0)),
        _const_spec((1, d)), _const_spec((d, 2 * d)), _const_spec((1, 2 * d)),
        _const_spec((CONV_WIDTH, d)), _const_spec((1, d)), _const_spec((1, d)), _const_spec((1, d)),
        _const_spec((d, d)), _const_spec((1, d)),
        pl.BlockSpec((HIST, BB_CONV, d), lambda s: (0, sblock(s), 0)),
        pl.BlockSpec((BB_CONV, d), lambda s: (sblock(s), 0)),
    ]
    out_specs = [
        pl.BlockSpec((tm, d), lambda s: (jnp.maximum(s - 1, 0), 0)),
        pl.BlockSpec((1, 1, HIST, d), lambda s: (0, jnp.minimum(s, n_tiles - 1) // tiles_per_seq, 0, 0)),
        pl.BlockSpec((HIST, BB_CONV, d), lambda s: (0, sblock(s), 0)),
        pl.BlockSpec((BB_CONV, d), lambda s: (sblock(s), 0)),
    ]
    out_shape = [jax.ShapeDtypeStruct((b * t, d), F32),
                 jax.ShapeDtypeStruct((1, b, HIST, d), F32),
                 jax.ShapeDtypeStruct(shist_t.shape, F32),
                 jax.ShapeDtypeStruct((ms, d), F32)]
    scratch = [
        pltpu.VMEM((tm + HIST_PAD, d), F32),
        pltpu.VMEM((tm, d), F32),
    ]
    x2d = x.reshape(b * t, d)
    return pl.pallas_call(
        functools.partial(_conv_prompt_kernel, n_tiles=n_tiles, tiles_per_seq=tiles_per_seq,
                          sample_every=sample_every),
        grid=(n_tiles + 1,), in_specs=in_specs, out_specs=out_specs,
        out_shape=out_shape, scratch_shapes=scratch,
        compiler_params=_params("arbitrary"), name="conv_prompt",
    )(x2d, x2d, gmix, win, bin_, dww, dwb, lng, lnb, wout, bout, shist_t, sxg)


def _mix_sample_in_kernel(sgu_ref, x_ref, gmix_ref, win_ref, cos_ref, sin_ref, lng_ref, lnb_ref,
                          q_ref, k_ref, v_ref, gate_ref, sgu_out_ref, rows_ref):
    h = _rms(x_ref[...], gmix_ref[...]).astype(BF16)
    qk = _dot(h, win_ref[:, 0:OFF_V])
    cos2 = cos_ref[...]
    sin2 = sin_ref[...]
    for j in range(HEADS):
        q_ref[:, j * DK:(j + 1) * DK] = _rotary(qk[:, j * DK:(j + 1) * DK], cos2, sin2)
        kj = qk[:, Q_W + j * DK:Q_W + (j + 1) * DK]
        k_ref[:, j * DK:(j + 1) * DK] = _rotary(kj, cos2, sin2) * (DK ** -0.5)
    v_ref[...] = _dot(h, win_ref[:, OFF_V:OFF_G])
    gate_ref[...] = _silu(_dot(h, win_ref[:, OFF_G:OFF_U]))
    u = _gelu(_dot(h, win_ref[:, OFF_U:OFF_S]))
    s = _gelu(_dot(h, win_ref[:, OFF_S:D_IN]))
    for g in range(GROUPS):
        gs = slice(g * GD, (g + 1) * GD)
        sn = _layernorm(s[:, gs], lng_ref[:, gs], lnb_ref[:, gs])
        rows_ref[:, gs] = sn
        sgu_out_ref[:, gs] = u[:, gs] * (sgu_ref[0, g] * sn + sgu_ref[1, g])


def _mix_sample_in(x, gmix, win, cos2, sin2, lng, lnb, sgu_wb):
    m, d = x.shape
    shp = lambda n: jax.ShapeDtypeStruct((m, n), F32)
    vm = lambda: pl.BlockSpec(memory_space=pltpu.VMEM)
    return pl.pallas_call(
        _mix_sample_in_kernel,
        in_specs=[_smem_spec()] + [vm() for _ in range(7)],
        out_specs=[vm() for _ in range(6)],
        out_shape=[shp(Q_W), shp(Q_W), shp(V_W), shp(V_W), shp(U_W), shp(U_W)],
        compiler_params=pltpu.CompilerParams(vmem_limit_bytes=VMEM_LIMIT), name="mix_sample_in",
    )(sgu_wb, x, gmix, win, cos2, sin2, lng, lnb)


def _mix_sample_out_kernel(x_ref, o_ref, gate_ref, sgu_ref, gn_ref, wout_ref, x1_ref):
    o = o_ref[...]
    parts = []
    for hh in range(HEADS):
        vs = slice(hh * DV, (hh + 1) * DV)
        parts.append((gate_ref[:, vs] * _groupnorm(o[:, vs], gn_ref[:, vs])).astype(BF16))
    y = _dot(jnp.concatenate(parts, axis=-1), wout_ref[0:V_W, :])
    y = y + _dot(sgu_ref[...].astype(BF16), wout_ref[V_W:V_W + U_W, :])
    x1_ref[...] = x_ref[...] + y


def _mix_sample_out(x, o, gate, sgu_out, gn, wout):
    m, d = x.shape
    vm = lambda: pl.BlockSpec(memory_space=pltpu.VMEM)
    return pl.pallas_call(
        _mix_sample_out_kernel, in_specs=[vm() for _ in range(6)], out_specs=vm(),
        out_shape=jax.ShapeDtypeStruct((m, d), F32),
        compiler_params=pltpu.CompilerParams(vmem_limit_bytes=VMEM_LIMIT), name="mix_sample_out",
    )(x, o, gate, sgu_out, gn, wout)


def _conv_sample_in_kernel(x_ref, g_ref, win_ref, bin_ref, xg_ref):
    xg_ref[...] = _glu_in(x_ref[...], g_ref[...], win_ref, bin_ref)


def _conv_sample_in(x, g, win, bin_):
    m, d = x.shape
    vm = lambda: pl.BlockSpec(memory_space=pltpu.VMEM)
    return pl.pallas_call(
        _conv_sample_in_kernel, in_specs=[vm() for _ in range(4)], out_specs=vm(),
        out_shape=jax.ShapeDtypeStruct((m, d), F32),
        compiler_params=pltpu.CompilerParams(vmem_limit_bytes=VMEM_LIMIT), name="conv_sample_in",
    )(x, g, win, bin_)


def _conv_sample_state_update(hist_ref, xg_ref, dww_ref, dwb_ref, newhist_ref, conv_ref):
    xg = xg_ref[...]
    acc = dwb_ref[...] + xg * dww_ref[HIST:CONV_WIDTH, :]
    for j in range(HIST):
        acc = acc + hist_ref[j] * dww_ref[j:j + 1, :]
    conv_ref[...] = acc
    for j in range(HIST - 1):
        newhist_ref[j] = hist_ref[j + 1]
    newhist_ref[HIST - 1] = xg


def _conv_sample_out_kernel(x_ref, conv_ref, lng_ref, lnb_ref, wout_ref, bout_ref, o_ref):
    c = _silu(_layernorm(conv_ref[...], lng_ref[...], lnb_ref[...])).astype(BF16)
    o_ref[...] = x_ref[...] + _dot(c, wout_ref[...]) + bout_ref[...]


def _conv_sample_out(x, conv, lng, lnb, wout, bout):
    m, d = x.shape
    vm = lambda: pl.BlockSpec(memory_space=pltpu.VMEM)
    return pl.pallas_call(
        _conv_sample_out_kernel, in_specs=[vm() for _ in range(6)], out_specs=vm(),
        out_shape=jax.ShapeDtypeStruct((m, d), F32),
        compiler_params=pltpu.CompilerParams(vmem_limit_bytes=VMEM_LIMIT), name="conv_sample_out",
    )(x, conv, lng, lnb, wout, bout)


def _rope_tables(pos):
    half = DK // 2
    inv = ROPE_BASE ** (-jnp.arange(half, dtype=F32) / half)
    ang = pos.astype(F32)[:, None] * inv[None, :]
    cos = jnp.cos(ang)
    sin = jnp.sin(ang)
    return jnp.concatenate([cos, cos], axis=-1), jnp.concatenate([-sin, sin], axis=-1)


def _decay_tables(c):
    log_g = jnp.log1p(-(2.0 ** (-5.0 - jnp.arange(HEADS, dtype=F32))))
    idx = jnp.arange(c, dtype=F32)
    diff = idx[:, None] - idx[None, :]
    dmask = jnp.where(diff >= 0, jnp.exp(jnp.maximum(diff, 0.0)[None] * log_g[:, None, None]), 0.0)
    xi = jnp.exp((idx[None, :] + 1.0) * log_g[:, None])
    zeta = jnp.exp((c - 1.0 - idx[None, :]) * log_g[:, None])
    chunk_decay = jnp.exp(c * log_g)
    return dmask, xi, zeta, chunk_decay


def kernel(x_prompt, x_sample, state_ret, state_conv, norm_mix_g, norm_ffn_g, final_norm_g, ab_w_in, ab_w_out, ret_gn_g, sgu_ln_g, sgu_ln_b, sgu_w, sgu_b, conv_w_in, conv_b_in, conv_dw_w, conv_dw_b, conv_ln_g, conv_ln_b, conv_w_out, conv_b_out, ffn_w1, ffn_w2):
    b, t, d = x_prompt.shape
    m_s = x_sample.shape[0]
    row = lambda a: a.reshape(1, -1)

    win = ab_w_in[0].astype(BF16)
    wout = ab_w_out[0].astype(BF16)
    cwin = conv_w_in[0].astype(BF16)
    cwout = conv_w_out[0].astype(BF16)
    gmix = [row(norm_mix_g[l]) for l in range(2)]
    gffn = [row(norm_ffn_g[l]) for l in range(2)]
    gfin = row(final_norm_g)
    gn, lng, lnb = row(ret_gn_g[0]), row(sgu_ln_g[0]), row(sgu_ln_b[0])
    cbin, cdwb, clng, clnb, cbout = (row(conv_b_in[0]), row(conv_dw_b[0]), row(conv_ln_g[0]),
                                     row(conv_ln_b[0]), row(conv_b_out[0]))

    cos_p, sin_p = _rope_tables(jnp.arange(t))
    dmask, xi, zeta, cd = _decay_tables(CHUNK)
    xi_b = jnp.broadcast_to(xi[:, :, None], (HEADS, CHUNK, DV))
    zeta_b = jnp.broadcast_to(zeta[:, :, None], (HEADS, CHUNK, DK))
    bsgu_b = jnp.broadcast_to(sgu_b[0][:, :CHUNK, None], (GROUPS, CHUNK, GD))

    xs = x_sample.reshape(m_s, d)
    cos_s, sin_s = _rope_tables(PAST_LEN + jnp.arange(1))
    _, _, _, decay1 = _decay_tables(1)
    sgu_wb = jnp.stack([sgu_w[0][:, 0, 0], sgu_b[0][:, 0]])
    q, k, v, gate, sgu_out, rows = _mix_sample_in(xs, gmix[0], win, cos_s, sin_s, lng, lnb, sgu_wb)
    x1, ret_state_p, ret_state_s, o = _mix_prompt(x_prompt, gmix[0], win, wout, cos_p, sin_p, dmask, xi_b, zeta_b,
                                                  cd, gn, lng, lnb, sgu_w[0], bsgu_b, decay1, q, k, v, state_ret)
    xs1 = _mix_sample_out(xs, o, gate, sgu_out, gn, wout)

    x2, xs2 = _ffn(x1.reshape(b * t, d), xs1, gffn[0], ffn_w1, ffn_w2, gfin, 0, False)

    xg = _conv_sample_in(xs2, gmix[1], cwin, cbin)
    hist_t = jnp.transpose(state_conv[0], (1, 0, 2))
    x3, conv_state_p, newhist_t, conv = _conv_prompt(x2.reshape(b, t, d), gmix[1], cwin, cbin, conv_dw_w[0], cdwb,
                                                     clng, clnb, cwout, cbout, hist_t, xg)
    conv_state_s = jnp.transpose(newhist_t, (1, 0, 2))[None]
    xs3 = _conv_sample_out(xs2, conv, clng, clnb, cwout, cbout)

    y_prompt, y_sample = _ffn(x3, xs3, gffn[1], ffn_w1, ffn_w2, gfin, 1, True)

    return (y_prompt.reshape(b, t, d), y_sample.reshape(m_s, 1, d), ret_state_p, ret_state_s,
            rows.reshape(1, m_s, 1, U_W), conv_state_p, conv_state_s)
```

```python
import functools

import jax
import jax.numpy as jnp
from jax import lax
from jax.experimental import pallas as pl
from jax.experimental.pallas import tpu as pltpu

F32 = jnp.float32
BF16 = jnp.bfloat16

D_MODEL = 1024
HEADS = 4
DK = 128
DV = 256
GROUPS = 4
GD = 256
CHUNK = 128
ROPE_BASE = 10000.0
CONV_WIDTH = 31
HIST = CONV_WIDTH - 1
D_FF = 4 * D_MODEL
RMS_EPS = 1e-6
LN_EPS = 1e-5
PAST_LEN = 16384

Q_W = HEADS * DK
V_W = HEADS * DV
U_W = GROUPS * GD
OFF_V = 2 * Q_W
OFF_G = OFF_V + V_W
OFF_U = OFF_G + V_W
OFF_S = OFF_U + U_W
D_IN = OFF_S + U_W

SUBLANES = 8
VMEM_LIMIT = 56 * 1024 * 1024

TM_MIX = 512
TM_FFN = 512
TM_CONV = 512
CONV_RB = 32
LANE_TILES = D_MODEL // 128
HIST_PAD = 32
FFN_SLICES = 4
FFN_HS = D_FF // FFN_SLICES
BB_RET = 8
HEAD_PAIR = 2
BB_CONV = 16


def _params(*sem):
    return pltpu.CompilerParams(dimension_semantics=sem, vmem_limit_bytes=VMEM_LIMIT)


def _const_spec(shape):
    n = len(shape)
    return pl.BlockSpec(shape, lambda *_: (0,) * n, pipeline_mode=pl.Buffered(1))


def _smem_spec():
    return pl.BlockSpec(memory_space=pltpu.SMEM)


def _rms(x, g):
    ms = jnp.mean(x * x, axis=-1, keepdims=True)
    return x * lax.rsqrt(ms + RMS_EPS) * g


def _layernorm(x, g, b):
    mu = jnp.mean(x, axis=-1, keepdims=True)
    xc = x - mu
    var = jnp.mean(xc * xc, axis=-1, keepdims=True)
    return xc * lax.rsqrt(var + LN_EPS) * g + b


def _groupnorm(x, g):
    mu = jnp.mean(x, axis=-1, keepdims=True)
    xc = x - mu
    var = jnp.mean(xc * xc, axis=-1, keepdims=True)
    return xc * lax.rsqrt(var + LN_EPS) * g


def _sigmoid(x):
    return 1.0 / (1.0 + jnp.exp(-x))


def _silu(x):
    return x * _sigmoid(x)


def _gelu(x):
    return 0.5 * x * (1.0 + lax.erf(x * 0.7071067811865476))


def _dot(a, b):
    return jnp.dot(a, b, preferred_element_type=F32)


def _dot_nt(a, b):
    return lax.dot_general(a, b, (((1,), (1,)), ((), ())), preferred_element_type=F32)


def _dot_tn(a, b):
    return lax.dot_general(a, b, (((0,), (0,)), ((), ())), preferred_element_type=F32)


def _rotary(blk, cos2, sin2):
    return blk * cos2 + pltpu.roll(blk, DK // 2, 1) * sin2


def _sample_state_update(step, decay_ref, qs_ref, ks_ref, vs_ref, sin_ref, sout_ref, os_ref):
    pair = step % (HEADS // HEAD_PAIR)
    for li in range(HEAD_PAIR):
        ks = slice(li * DK, (li + 1) * DK)
        vs = slice(li * DV, (li + 1) * DV)
        qt = qs_ref[0, :, ks].T
        kt = ks_ref[0, :, ks].T
        gam = decay_ref[pair * HEAD_PAIR + li]
        for b in range(BB_RET):
            st = sin_ref[0, b, li]
            sn = gam * st + kt[:, b:b + 1] * vs_ref[0, b:b + 1, vs]
            sout_ref[0, b, li] = sn
            os_ref[0, b:b + 1, vs] = jnp.sum(qt[:, b:b + 1] * sn, axis=0, keepdims=True)


def _mix_prompt_kernel(cd_ref, decay_ref, x_ref, gmix_ref, win_ref, wout_ref, cos_ref, sin_ref,
                       dmask_ref, xi_ref, zeta_ref, gn_ref, lng_ref, lnb_ref,
                       wsgu_ref, bsgu_ref, qs_ref, ks_ref, vs_ref, sstate_ref,
                       x1_ref, state_ref, sstate_out_ref, os_ref,
                       q_s, k_s, kz_s, v_s, gate_s, u_s, sn_s, mix_s, inner_s, kv_s, sb_s):
    t = pl.program_id(1)
    tm = x_ref.shape[1]

    @pl.when(t == 0)
    def _():
        state_ref[...] = jnp.zeros_like(state_ref)

    _sample_state_update(pl.program_id(0) * pl.num_programs(1) + t, decay_ref, qs_ref, ks_ref, vs_ref,
                         sstate_ref, sstate_out_ref, os_ref)

    x = x_ref[0]
    h = _rms(x, gmix_ref[...]).astype(BF16)

    cos2 = cos_ref[...]
    sin2 = sin_ref[...]
    n_chunks = tm // CHUNK

    def proj(off, g):
        return _dot(h, win_ref[:, off + g * GD:off + (g + 1) * GD])

    for pair in range(Q_W // GD):
        qq = proj(0, pair)
        kk = proj(Q_W, pair)
        for e in range(GD // DK):
            j = pair * (GD // DK) + e
            ks = slice(j * DK, (j + 1) * DK)
            q_s[:, ks] = _rotary(qq[:, e * DK:(e + 1) * DK], cos2, sin2).astype(BF16)
            kj = _rotary(kk[:, e * DK:(e + 1) * DK], cos2, sin2) * (DK ** -0.5)
            k_s[:, ks] = kj.astype(BF16)
            for c in range(n_chunks):
                rs = slice(c * CHUNK, (c + 1) * CHUNK)
                kz_s[rs, ks] = (kj[rs, :] * zeta_ref[j]).astype(BF16)
    for g in range(GROUPS):
        gs = slice(g * GD, (g + 1) * GD)
        v_s[:, gs] = proj(OFF_V, g).astype(BF16)
        gate_s[:, gs] = _silu(proj(OFF_G, g))
        u_s[:, gs] = _gelu(proj(OFF_U, g))
        sn_s[:, gs] = _layernorm(_gelu(proj(OFF_S, g)), lng_ref[:, gs], lnb_ref[:, gs]).astype(BF16)

    row = lax.broadcasted_iota(jnp.int32, (CHUNK, CHUNK), 0)
    col = lax.broadcasted_iota(jnp.int32, (CHUNK, CHUNK), 1)
    tril = row >= col

    pairs = [(c, hh) for c in range(n_chunks) for hh in range(HEADS)]
    for c, hh in pairs:
        rs = slice(c * CHUNK, (c + 1) * CHUNK)
        ks = slice(hh * DK, (hh + 1) * DK)
        inner = _dot_nt(q_s[rs, ks], k_s[rs, ks]) * dmask_ref[hh]
        inner_s[c * HEADS + hh] = inner.astype(BF16)
    for c, hh in pairs:
        rs = slice(c * CHUNK, (c + 1) * CHUNK)
        kv_s[c * HEADS + hh] = _dot_tn(kz_s[rs, hh * DK:(hh + 1) * DK], v_s[rs, hh * DV:(hh + 1) * DV])
    for hh in range(HEADS):
        st = state_ref[0, 0, hh]
        for c in range(n_chunks):
            sb_s[c * HEADS + hh] = st.astype(BF16)
            st = cd_ref[hh] * st + kv_s[c * HEADS + hh]
        state_ref[0, 0, hh] = st
    for c, hh in pairs:
        rs = slice(c * CHUNK, (c + 1) * CHUNK)
        vs = slice(hh * DV, (hh + 1) * DV)
        o = _dot(inner_s[c * HEADS + hh], v_s[rs, vs])
        o = o + _dot(q_s[rs, hh * DK:(hh + 1) * DK], sb_s[c * HEADS + hh]) * xi_ref[hh]
        on = _groupnorm(o, gn_ref[:, vs])
        mix_s[rs, vs] = (gate_s[rs, vs] * on).astype(BF16)

    for c in range(n_chunks):
        rs = slice(c * CHUNK, (c + 1) * CHUNK)
        for g in range(GROUPS):
            gs = slice(g * GD, (g + 1) * GD)
            wg = jnp.where(tril, wsgu_ref[g], 0.0).astype(BF16)
            mixed = _dot(wg, sn_s[rs, gs]) + bsgu_ref[g]
            mix_s[rs, V_W + g * GD:V_W + (g + 1) * GD] = (u_s[rs, gs] * mixed).astype(BF16)

    x1_ref[0] = x + _dot(mix_s[...], wout_ref[...])


def _mix_prompt(x, gmix, win, wout, cos2, sin2, dmask, xi_b, zeta_b, cd, gn, lng, lnb, wsgu, bsgu_b,
                decay1, qs, ks, vs, sstate):
    b, t, d = x.shape
    tm = TM_MIX
    nt = t // tm
    n_pairs = (tm // CHUNK) * HEADS
    grid = (b, nt)
    ms = qs.shape[0]
    n_hp = HEADS // HEAD_PAIR
    assert b * nt == (ms // BB_RET) * n_hp
    by_pair = lambda a, w: a.reshape(ms, n_hp, w).transpose(1, 0, 2)
    qs2, ks2, vs2 = by_pair(qs, HEAD_PAIR * DK), by_pair(ks, HEAD_PAIR * DK), by_pair(vs, HEAD_PAIR * DV)
    pair_blk = lambda i, j: ((i * nt + j) % n_hp, (i * nt + j) // n_hp, 0)
    state_blk = lambda i, j: (0, (i * nt + j) // n_hp, (i * nt + j) % n_hp, 0, 0)
    in_specs = [
        _smem_spec(),
        _smem_spec(),
        pl.BlockSpec((1, tm, d), lambda i, j: (i, j, 0)),
        _const_spec((1, d)),
        _const_spec((d, D_IN)),
        _const_spec((V_W + U_W, d)),
        pl.BlockSpec((tm, DK), lambda i, j: (j, 0)),
        pl.BlockSpec((tm, DK), lambda i, j: (j, 0)),
        _const_spec((HEADS, CHUNK, CHUNK)),
        _const_spec((HEADS, CHUNK, DV)),
        _const_spec((HEADS, CHUNK, DK)),
        _const_spec((1, V_W)),
        _const_spec((1, U_W)),
        _const_spec((1, U_W)),
        _const_spec((GROUPS, CHUNK, CHUNK)),
        _const_spec((GROUPS, CHUNK, GD)),
        pl.BlockSpec((1, BB_RET, HEAD_PAIR * DK), pair_blk),
        pl.BlockSpec((1, BB_RET, HEAD_PAIR * DK), pair_blk),
        pl.BlockSpec((1, BB_RET, HEAD_PAIR * DV), pair_blk),
        pl.BlockSpec((1, BB_RET, HEAD_PAIR, DK, DV), state_blk),
    ]
    out_specs = [
        pl.BlockSpec((1, tm, d), lambda i, j: (i, j, 0)),
        pl.BlockSpec((1, 1, HEADS, DK, DV), lambda i, j: (0, i, 0, 0, 0)),
        pl.BlockSpec((1, BB_RET, HEAD_PAIR, DK, DV), state_blk),
        pl.BlockSpec((1, BB_RET, HEAD_PAIR * DV), pair_blk),
    ]
    out_shape = [
        jax.ShapeDtypeStruct((b, t, d), F32),
        jax.ShapeDtypeStruct((1, b, HEADS, DK, DV), F32),
        jax.ShapeDtypeStruct(sstate.shape, F32),
        jax.ShapeDtypeStruct((n_hp, ms, HEAD_PAIR * DV), F32),
    ]
    scratch = [
        pltpu.VMEM((tm, Q_W), BF16),
        pltpu.VMEM((tm, Q_W), BF16),
        pltpu.VMEM((tm, Q_W), BF16),
        pltpu.VMEM((tm, V_W), BF16),
        pltpu.VMEM((tm, V_W), F32),
        pltpu.VMEM((tm, U_W), F32),
        pltpu.VMEM((tm, U_W), BF16),
        pltpu.VMEM((tm, V_W + U_W), BF16),
        pltpu.VMEM((n_pairs, CHUNK, CHUNK), BF16),
        pltpu.VMEM((n_pairs, DK, DV), F32),
        pltpu.VMEM((n_pairs, DK, DV), BF16),
    ]
    x1, state_p, state_s, o2 = pl.pallas_call(
        _mix_prompt_kernel, grid=grid, in_specs=in_specs, out_specs=out_specs,
        out_shape=out_shape, scratch_shapes=scratch,
        compiler_params=_params("arbitrary", "arbitrary"), name="mix_prompt",
    )(cd, decay1, x, gmix, win, wout, cos2, sin2, dmask, xi_b, zeta_b, gn, lng, lnb, wsgu, bsgu_b,
      qs2, ks2, vs2, sstate)
    return x1, state_p, state_s, o2.transpose(1, 0, 2).reshape(ms, V_W)


def _ffn_tile(x, g, gfin, w1_s, w2_s, final, before_slice=None):
    h = _rms(x, g).astype(BF16)
    acc = x
    for j in range(FFN_SLICES):
        if before_slice is not None:
            before_slice(j)
        js = slice(j * FFN_HS, (j + 1) * FFN_HS)
        f = _dot(h, w1_s[:, js])
        a = jnp.square(jnp.maximum(f, 0.0)).astype(BF16)
        acc = acc + _dot(a, w2_s[js, :])
    if final:
        acc = _rms(acc, gfin)
    return acc


def _ffn_kernel(xp_ref, xs_ref, g_ref, gfin_ref, w1_hbm, w2_hbm, yp_ref, ys_ref,
                w1_s, w2_s, st1, st2, sem, *, layer, final, n_tiles):
    i = pl.program_id(0)

    def slice_copies(j):
        slot = j % 2
        js = pl.ds(j * FFN_HS, FFN_HS)
        return (pltpu.make_async_copy(w1_hbm.at[layer, :, js], st1.at[slot], sem.at[0, slot]),
                pltpu.make_async_copy(w2_hbm.at[layer, js, :], st2.at[slot], sem.at[1, slot]))

    def fetch_slice(j):
        js = slice(j * FFN_HS, (j + 1) * FFN_HS)
        for c in slice_copies(j):
            c.wait()
        w1_s[:, js] = st1[j % 2].astype(BF16)
        w2_s[js, :] = st2[j % 2].astype(BF16)
        if j + 2 < FFN_SLICES:
            for c in slice_copies(j + 2):
                c.start()

    @pl.when(i == 0)
    def _():
        for j in range(min(2, FFN_SLICES)):
            for c in slice_copies(j):
                c.start()
        yp_ref[...] = _ffn_tile(xp_ref[...], g_ref[...], gfin_ref[...], w1_s, w2_s, final, fetch_slice)

    @pl.when(jnp.logical_and(i > 0, i < n_tiles))
    def _():
        yp_ref[...] = _ffn_tile(xp_ref[...], g_ref[...], gfin_ref[...], w1_s, w2_s, final)

    @pl.when(i == n_tiles)
    def _():
        ys_ref[...] = _ffn_tile(xs_ref[...], g_ref[...], gfin_ref[...], w1_s, w2_s, final)


def _ffn(xp, xs, g, w1, w2, gfin, layer, final):
    m, d = xp.shape
    ms = xs.shape[0]
    tm = TM_FFN
    n_tiles = m // tm
    tile = lambda i: (jnp.minimum(i, n_tiles - 1), 0)
    return pl.pallas_call(
        functools.partial(_ffn_kernel, layer=layer, final=final, n_tiles=n_tiles),
        grid=(n_tiles + 1,),
        in_specs=[pl.BlockSpec((tm, d), tile), _const_spec((ms, d)),
                  _const_spec((1, d)), _const_spec((1, d)),
                  pl.BlockSpec(memory_space=pl.ANY), pl.BlockSpec(memory_space=pl.ANY)],
        out_specs=[pl.BlockSpec((tm, d), tile), pl.BlockSpec((ms, d), lambda i: (0, 0))],
        out_shape=[jax.ShapeDtypeStruct((m, d), F32), jax.ShapeDtypeStruct((ms, d), F32)],
        scratch_shapes=[pltpu.VMEM((d, D_FF), BF16), pltpu.VMEM((D_FF, d), BF16),
                        pltpu.VMEM((2, d, FFN_HS), F32), pltpu.VMEM((2, FFN_HS, d), F32),
                        pltpu.SemaphoreType.DMA((2, 2))],
        compiler_params=_params("arbitrary"), name="ffn_final" if final else "ffn",
    )(xp, xs, g, gfin, w1, w2)


def _glu_pieces(x, g, win_ref, bin_ref):
    h = _rms(x, g).astype(BF16)
    for c in range(D_MODEL // GD):
        ca = slice(c * GD, (c + 1) * GD)
        cg = slice(D_MODEL + c * GD, D_MODEL + (c + 1) * GD)
        a = _dot(h, win_ref[:, ca]) + bin_ref[:, ca]
        gt = _dot(h, win_ref[:, cg]) + bin_ref[:, cg]
        yield c, a * _sigmoid(gt)


def _glu_in(x, g, win_ref, bin_ref):
    return jnp.concatenate([p for _, p in _glu_pieces(x, g, win_ref, bin_ref)], axis=-1)


def _from_token_rows(buf, first_token, n):
    return jnp.concatenate(
        [buf[pl.ds(first_token * LANE_TILES + g, n, stride=LANE_TILES), :] for g in range(LANE_TILES)], axis=-1)


def _conv_block(r, full_s, conv_s, w3_ref, b3_ref):
    off0 = HIST_PAD - HIST
    acc = jnp.broadcast_to(b3_ref[...][None], (CONV_RB, LANE_TILES, 128))
    for j in range(CONV_WIDTH):
        start = pl.multiple_of((r * CONV_RB + j + off0) * LANE_TILES, LANE_TILES)
        xj = full_s[pl.ds(start, CONV_RB * LANE_TILES), :].reshape(CONV_RB, LANE_TILES, 128)
        acc = acc + xj * w3_ref[j][None]
    out0 = pl.multiple_of(r * CONV_RB * LANE_TILES, CONV_RB * LANE_TILES)
    conv_s[pl.ds(out0, CONV_RB * LANE_TILES), :] = acc.reshape(CONV_RB * LANE_TILES, 128)


def _conv_prompt_kernel(x_ref, xprev_ref, gmix_ref, win_ref, bin_ref, dww_ref, dwb_ref, w3_ref, b3_ref,
                        lng_ref, lnb_ref, wout_ref, bout_ref, shist_ref, sxg_ref,
                        o_ref, hist_ref, snewhist_ref, sconv_ref,
                        full_s, conv_s, *, n_tiles, tiles_per_seq, sample_every):
    s = pl.program_id(0)
    tm = x_ref.shape[0]
    off0 = HIST_PAD - HIST
    hist_rows = HIST_PAD * LANE_TILES

    @pl.when(jnp.logical_and(s % sample_every == 0, s < n_tiles))
    def _():
        _conv_sample_state_update(shist_ref, sxg_ref, dww_ref, dwb_ref, snewhist_ref, sconv_ref)

    def front():
        for c, piece in _glu_pieces(x_ref[...], gmix_ref[...], win_ref, bin_ref):
            for e in range(GD // 128):
                g = c * (GD // 128) + e
                full_s[pl.ds(hist_rows + g, tm, stride=LANE_TILES), :] = piece[:, e * 128:(e + 1) * 128]

    def back():
        conv = _from_token_rows(conv_s, 0, tm)
        c = _silu(_layernorm(conv, lng_ref[...], lnb_ref[...])).astype(BF16)
        o_ref[...] = xprev_ref[...] + _dot(c, wout_ref[...]) + bout_ref[...]

    @pl.when(s % tiles_per_seq == 0)
    def _():
        full_s[0:hist_rows, :] = jnp.zeros((hist_rows, 128), F32)

    @pl.when(s == 0)
    def _():
        front()

    @pl.when(jnp.logical_and(s > 0, s < n_tiles))
    def _():
        front()
        back()

    @pl.when(s == n_tiles)
    def _():
        back()

    @pl.when(s < n_tiles)
    def _():
        def body(r, carry):
            _conv_block(r, full_s, conv_s, w3_ref, b3_ref)
            return carry
        lax.fori_loop(0, tm // CONV_RB, body, 0)

        @pl.when(s % tiles_per_seq == tiles_per_seq - 1)
        def _():
            hist_ref[0, 0] = _from_token_rows(full_s, tm + off0, HIST)

        full_s[0:hist_rows, :] = full_s[tm * LANE_TILES:tm * LANE_TILES + hist_rows, :]


def _conv_prompt(x, gmix, win, bin_, dww, dwb, lng, lnb, wout, bout, shist_t, sxg):
    b, t, d = x.shape
    tm = TM_CONV
    n_tiles = (b * t) // tm
    tiles_per_seq = t // tm
    ms = sxg.shape[0]
    n_sblocks = ms // BB_CONV
    sample_every = n_tiles // n_sblocks
    assert sample_every * n_sblocks == n_tiles
    sblock = lambda s: jnp.minimum(s // sample_every, n_sblocks - 1)
    in_specs = [
        pl.BlockSpec((tm, d), lambda s: (jnp.minimum(s, n_tiles - 1), 0)),
        pl.BlockSpec((tm, d), lambda s: (jnp.maximum(s - 1, 0), 0)),
        _const_spec((1, d)), _const_spec((d, 2 * d)), _const_spec((1, 2 * d)),
        _const_spec((CONV_WIDTH, d)), _const_spec((1, d)),
        _const_spec((CONV_WIDTH, LANE_TILES, 128)), _const_spec((LANE_TILES, 128)),
        _const_spec((1, d)), _const_spec((1, d)),
        _const_spec((d, d)), _const_spec((1, d)),
        pl.BlockSpec((HIST, BB_CONV, d), lambda s: (0, sblock(s), 0)),
        pl.BlockSpec((BB_CONV, d), lambda s: (sblock(s), 0)),
    ]
    out_specs = [
        pl.BlockSpec((tm, d), lambda s: (jnp.maximum(s - 1, 0), 0)),
        pl.BlockSpec((1, 1, HIST, d), lambda s: (0, jnp.minimum(s, n_tiles - 1) // tiles_per_seq, 0, 0)),
        pl.BlockSpec((HIST, BB_CONV, d), lambda s: (0, sblock(s), 0)),
        pl.BlockSpec((BB_CONV, d), lambda s: (sblock(s), 0)),
    ]
    out_shape = [jax.ShapeDtypeStruct((b * t, d), F32),
                 jax.ShapeDtypeStruct((1, b, HIST, d), F32),
                 jax.ShapeDtypeStruct(shist_t.shape, F32),
                 jax.ShapeDtypeStruct((ms, d), F32)]
    scratch = [
        pltpu.VMEM(((tm + HIST_PAD) * LANE_TILES, 128), F32),
        pltpu.VMEM((tm * LANE_TILES, 128), F32),
    ]
    x2d = x.reshape(b * t, d)
    w3 = dww.reshape(CONV_WIDTH, LANE_TILES, 128)
    b3 = dwb.reshape(LANE_TILES, 128)
    return pl.pallas_call(
        functools.partial(_conv_prompt_kernel, n_tiles=n_tiles, tiles_per_seq=tiles_per_seq,
                          sample_every=sample_every),
        grid=(n_tiles + 1,), in_specs=in_specs, out_specs=out_specs,
        out_shape=out_shape, scratch_shapes=scratch,
        compiler_params=_params("arbitrary"), name="conv_prompt",
    )(x2d, x2d, gmix, win, bin_, dww, dwb, w3, b3, lng, lnb, wout, bout, shist_t, sxg)


def _mix_sample_in_kernel(sgu_ref, x_ref, gmix_ref, win_ref, cos_ref, sin_ref, lng_ref, lnb_ref,
                          q_ref, k_ref, v_ref, gate_ref, sgu_out_ref, rows_ref):
    h = _rms(x_ref[...], gmix_ref[...]).astype(BF16)
    qk = _dot(h, win_ref[:, 0:OFF_V])
    cos2 = cos_ref[...]
    sin2 = sin_ref[...]
    for j in range(HEADS):
        q_ref[:, j * DK:(j + 1) * DK] = _rotary(qk[:, j * DK:(j + 1) * DK], cos2, sin2)
        kj = qk[:, Q_W + j * DK:Q_W + (j + 1) * DK]
        k_ref[:, j * DK:(j + 1) * DK] = _rotary(kj, cos2, sin2) * (DK ** -0.5)
    v_ref[...] = _dot(h, win_ref[:, OFF_V:OFF_G])
    gate_ref[...] = _silu(_dot(h, win_ref[:, OFF_G:OFF_U]))
    u = _gelu(_dot(h, win_ref[:, OFF_U:OFF_S]))
    s = _gelu(_dot(h, win_ref[:, OFF_S:D_IN]))
    for g in range(GROUPS):
        gs = slice(g * GD, (g + 1) * GD)
        sn = _layernorm(s[:, gs], lng_ref[:, gs], lnb_ref[:, gs])
        rows_ref[:, gs] = sn
        sgu_out_ref[:, gs] = u[:, gs] * (sgu_ref[0, g] * sn + sgu_ref[1, g])


def _mix_sample_in(x, gmix, win, cos2, sin2, lng, lnb, sgu_wb):
    m, d = x.shape
    shp = lambda n: jax.ShapeDtypeStruct((m, n), F32)
    vm = lambda: pl.BlockSpec(memory_space=pltpu.VMEM)
    return pl.pallas_call(
        _mix_sample_in_kernel,
        in_specs=[_smem_spec()] + [vm() for _ in range(7)],
        out_specs=[vm() for _ in range(6)],
        out_shape=[shp(Q_W), shp(Q_W), shp(V_W), shp(V_W), shp(U_W), shp(U_W)],
        compiler_params=pltpu.CompilerParams(vmem_limit_bytes=VMEM_LIMIT), name="mix_sample_in",
    )(sgu_wb, x, gmix, win, cos2, sin2, lng, lnb)


def _mix_sample_out_kernel(x_ref, o_ref, gate_ref, sgu_ref, gn_ref, wout_ref, x1_ref):
    o = o_ref[...]
    parts = []
    for hh in range(HEADS):
        vs = slice(hh * DV, (hh + 1) * DV)
        parts.append((gate_ref[:, vs] * _groupnorm(o[:, vs], gn_ref[:, vs])).astype(BF16))
    y = _dot(jnp.concatenate(parts, axis=-1), wout_ref[0:V_W, :])
    y = y + _dot(sgu_ref[...].astype(BF16), wout_ref[V_W:V_W + U_W, :])
    x1_ref[...] = x_ref[...] + y


def _mix_sample_out(x, o, gate, sgu_out, gn, wout):
    m, d = x.shape
    vm = lambda: pl.BlockSpec(memory_space=pltpu.VMEM)
    return pl.pallas_call(
        _mix_sample_out_kernel, in_specs=[vm() for _ in range(6)], out_specs=vm(),
        out_shape=jax.ShapeDtypeStruct((m, d), F32),
        compiler_params=pltpu.CompilerParams(vmem_limit_bytes=VMEM_LIMIT), name="mix_sample_out",
    )(x, o, gate, sgu_out, gn, wout)


def _conv_sample_in_kernel(x_ref, g_ref, win_ref, bin_ref, xg_ref):
    xg_ref[...] = _glu_in(x_ref[...], g_ref[...], win_ref, bin_ref)


def _conv_sample_in(x, g, win, bin_):
    m, d = x.shape
    vm = lambda: pl.BlockSpec(memory_space=pltpu.VMEM)
    return pl.pallas_call(
        _conv_sample_in_kernel, in_specs=[vm() for _ in range(4)], out_specs=vm(),
        out_shape=jax.ShapeDtypeStruct((m, d), F32),
        compiler_params=pltpu.CompilerParams(vmem_limit_bytes=VMEM_LIMIT), name="conv_sample_in",
    )(x, g, win, bin_)


def _conv_sample_state_update(hist_ref, xg_ref, dww_ref, dwb_ref, newhist_ref, conv_ref):
    xg = xg_ref[...]
    acc = dwb_ref[...] + xg * dww_ref[HIST:CONV_WIDTH, :]
    for j in range(HIST):
        acc = acc + hist_ref[j] * dww_ref[j:j + 1, :]
    conv_ref[...] = acc
    for j in range(HIST - 1):
        newhist_ref[j] = hist_ref[j + 1]
    newhist_ref[HIST - 1] = xg


def _conv_sample_out_kernel(x_ref, conv_ref, lng_ref, lnb_ref, wout_ref, bout_ref, o_ref):
    c = _silu(_layernorm(conv_ref[...], lng_ref[...], lnb_ref[...])).astype(BF16)
    o_ref[...] = x_ref[...] + _dot(c, wout_ref[...]) + bout_ref[...]


def _conv_sample_out(x, conv, lng, lnb, wout, bout):
    m, d = x.shape
    vm = lambda: pl.BlockSpec(memory_space=pltpu.VMEM)
    return pl.pallas_call(
        _conv_sample_out_kernel, in_specs=[vm() for _ in range(6)], out_specs=vm(),
        out_shape=jax.ShapeDtypeStruct((m, d), F32),
        compiler_params=pltpu.CompilerParams(vmem_limit_bytes=VMEM_LIMIT), name="conv_sample_out",
    )(x, conv, lng, lnb, wout, bout)


def _rope_tables(pos):
    half = DK // 2
    inv = ROPE_BASE ** (-jnp.arange(half, dtype=F32) / half)
    ang = pos.astype(F32)[:, None] * inv[None, :]
    cos = jnp.cos(ang)
    sin = jnp.sin(ang)
    return jnp.concatenate([cos, cos], axis=-1), jnp.concatenate([-sin, sin], axis=-1)


def _decay_tables(c):
    log_g = jnp.log1p(-(2.0 ** (-5.0 - jnp.arange(HEADS, dtype=F32))))
    idx = jnp.arange(c, dtype=F32)
    diff = idx[:, None] - idx[None, :]
    dmask = jnp.where(diff >= 0, jnp.exp(jnp.maximum(diff, 0.0)[None] * log_g[:, None, None]), 0.0)
    xi = jnp.exp((idx[None, :] + 1.0) * log_g[:, None])
    zeta = jnp.exp((c - 1.0 - idx[None, :]) * log_g[:, None])
    chunk_decay = jnp.exp(c * log_g)
    return dmask, xi, zeta, chunk_decay


def kernel(x_prompt, x_sample, state_ret, state_conv, norm_mix_g, norm_ffn_g, final_norm_g, ab_w_in, ab_w_out, ret_gn_g, sgu_ln_g, sgu_ln_b, sgu_w, sgu_b, conv_w_in, conv_b_in, conv_dw_w, conv_dw_b, conv_ln_g, conv_ln_b, conv_w_out, conv_b_out, ffn_w1, ffn_w2):
    b, t, d = x_prompt.shape
    m_s = x_sample.shape[0]
    row = lambda a: a.reshape(1, -1)

    win = ab_w_in[0].astype(BF16)
    wout = ab_w_out[0].astype(BF16)
    cwin = conv_w_in[0].astype(BF16)
    cwout = conv_w_out[0].astype(BF16)
    gmix = [row(norm_mix_g[l]) for l in range(2)]
    gffn = [row(norm_ffn_g[l]) for l in range(2)]
    gfin = row(final_norm_g)
    gn, lng, lnb = row(ret_gn_g[0]), row(sgu_ln_g[0]), row(sgu_ln_b[0])
    cbin, cdwb, clng, clnb, cbout = (row(conv_b_in[0]), row(conv_dw_b[0]), row(conv_ln_g[0]),
                                     row(conv_ln_b[0]), row(conv_b_out[0]))

    cos_p, sin_p = _rope_tables(jnp.arange(t))
    dmask, xi, zeta, cd = _decay_tables(CHUNK)
    xi_b = jnp.broadcast_to(xi[:, :, None], (HEADS, CHUNK, DV))
    zeta_b = jnp.broadcast_to(zeta[:, :, None], (HEADS, CHUNK, DK))
    bsgu_b = jnp.broadcast_to(sgu_b[0][:, :CHUNK, None], (GROUPS, CHUNK, GD))

    xs = x_sample.reshape(m_s, d)
    cos_s, sin_s = _rope_tables(PAST_LEN + jnp.arange(1))
    _, _, _, decay1 = _decay_tables(1)
    sgu_wb = jnp.stack([sgu_w[0][:, 0, 0], sgu_b[0][:, 0]])
    q, k, v, gate, sgu_out, rows = _mix_sample_in(xs, gmix[0], win, cos_s, sin_s, lng, lnb, sgu_wb)
    x1, ret_state_p, ret_state_s, o = _mix_prompt(x_prompt, gmix[0], win, wout, cos_p, sin_p, dmask, xi_b, zeta_b,
                                                  cd, gn, lng, lnb, sgu_w[0], bsgu_b, decay1, q, k, v, state_ret)
    xs1 = _mix_sample_out(xs, o, gate, sgu_out, gn, wout)

    x2, xs2 = _ffn(x1.reshape(b * t, d), xs1, gffn[0], ffn_w1, ffn_w2, gfin, 0, False)

    xg = _conv_sample_in(xs2, gmix[1], cwin, cbin)
    hist_t = jnp.transpose(state_conv[0], (1, 0, 2))
    x3, conv_state_p, newhist_t, conv = _conv_prompt(x2.reshape(b, t, d), gmix[1], cwin, cbin, conv_dw_w[0], cdwb,
                                                     clng, clnb, cwout, cbout, hist_t, xg)
    conv_state_s = jnp.transpose(newhist_t, (1, 0, 2))[None]
    xs3 = _conv_sample_out(xs2, conv, clng, clnb, cwout, cbout)

    y_prompt, y_sample = _ffn(x3, xs3, gffn[1], ffn_w1, ffn_w2, gfin, 1, True)

    return (y_prompt.reshape(b, t, d), y_sample.reshape(m_s, 1, d), ret_state_p, ret_state_s,
            rows.reshape(1, m_s, 1, U_W), conv_state_p, conv_state_s)
```

```python
import functools

import jax
import jax.numpy as jnp
from jax import lax
from jax.experimental import pallas as pl
from jax.experimental.pallas import tpu as pltpu

F32 = jnp.float32
BF16 = jnp.bfloat16

D_MODEL = 1024
HEADS = 4
DK = 128
DV = 256
GROUPS = 4
GD = 256
CHUNK = 128
ROPE_BASE = 10000.0
CONV_WIDTH = 31
HIST = CONV_WIDTH - 1
D_FF = 4 * D_MODEL
RMS_EPS = 1e-6
LN_EPS = 1e-5
PAST_LEN = 16384

Q_W = HEADS * DK
V_W = HEADS * DV
U_W = GROUPS * GD
OFF_V = 2 * Q_W
OFF_G = OFF_V + V_W
OFF_U = OFF_G + V_W
OFF_S = OFF_U + U_W
D_IN = OFF_S + U_W

SUBLANES = 8
VMEM_LIMIT = 56 * 1024 * 1024

TM_MIX = 512
TM_FFN = 512
TM_CONV = 512
CONV_RB = 32
LANE_TILES = D_MODEL // 128
HIST_PAD = 32
FFN_SLICES = 4
FFN_HS = D_FF // FFN_SLICES
CAST_STEPS = 8
BB_RET = 8
HEAD_PAIR = 2
BB_CONV = 16


def _params(*sem):
    return pltpu.CompilerParams(dimension_semantics=sem, vmem_limit_bytes=VMEM_LIMIT)


def _const_spec(shape):
    n = len(shape)
    return pl.BlockSpec(shape, lambda *_: (0,) * n, pipeline_mode=pl.Buffered(1))


def _smem_spec():
    return pl.BlockSpec(memory_space=pltpu.SMEM)


def _rms(x, g):
    ms = jnp.mean(x * x, axis=-1, keepdims=True)
    return x * lax.rsqrt(ms + RMS_EPS) * g


def _layernorm(x, g, b):
    mu = jnp.mean(x, axis=-1, keepdims=True)
    xc = x - mu
    var = jnp.mean(xc * xc, axis=-1, keepdims=True)
    return xc * lax.rsqrt(var + LN_EPS) * g + b


def _groupnorm(x, g):
    mu = jnp.mean(x, axis=-1, keepdims=True)
    xc = x - mu
    var = jnp.mean(xc * xc, axis=-1, keepdims=True)
    return xc * lax.rsqrt(var + LN_EPS) * g


def _sigmoid(x):
    return 1.0 / (1.0 + jnp.exp(-x))


def _silu(x):
    return x * _sigmoid(x)


def _gelu(x):
    return 0.5 * x * (1.0 + lax.erf(x * 0.7071067811865476))


def _dot(a, b):
    return jnp.dot(a, b, preferred_element_type=F32)


def _dot_nt(a, b):
    return lax.dot_general(a, b, (((1,), (1,)), ((), ())), preferred_element_type=F32)


def _dot_tn(a, b):
    return lax.dot_general(a, b, (((0,), (0,)), ((), ())), preferred_element_type=F32)


def _rotary(blk, cos2, sin2):
    return blk * cos2 + pltpu.roll(blk, DK // 2, 1) * sin2


def _sample_state_update(step, decay_ref, qs_ref, ks_ref, vs_ref, sin_ref, sout_ref, os_ref):
    pair = step % (HEADS // HEAD_PAIR)
    for li in range(HEAD_PAIR):
        ks = slice(li * DK, (li + 1) * DK)
        vs = slice(li * DV, (li + 1) * DV)
        qt = qs_ref[0, :, ks].T
        kt = ks_ref[0, :, ks].T
        gam = decay_ref[pair * HEAD_PAIR + li]
        for b in range(BB_RET):
            st = sin_ref[0, b, li]
            sn = gam * st + kt[:, b:b + 1] * vs_ref[0, b:b + 1, vs]
            sout_ref[0, b, li] = sn
            os_ref[0, b:b + 1, vs] = jnp.sum(qt[:, b:b + 1] * sn, axis=0, keepdims=True)


def _mix_prompt_kernel(cd_ref, decay_ref, x_ref, gmix_ref, win_ref, wout_ref, cos_ref, sin_ref,
                       dmask_ref, xi_ref, zeta_ref, gn_ref, lng_ref, lnb_ref,
                       wsgu_ref, bsgu_ref, qs_ref, ks_ref, vs_ref, sstate_ref,
                       x1_ref, state_ref, sstate_out_ref, os_ref,
                       q_s, k_s, kz_s, v_s, gate_s, u_s, sn_s, mix_s, inner_s, kv_s, sb_s):
    t = pl.program_id(1)
    tm = x_ref.shape[1]

    @pl.when(t == 0)
    def _():
        state_ref[...] = jnp.zeros_like(state_ref)

    _sample_state_update(pl.program_id(0) * pl.num_programs(1) + t, decay_ref, qs_ref, ks_ref, vs_ref,
                         sstate_ref, sstate_out_ref, os_ref)

    x = x_ref[0]
    h = _rms(x, gmix_ref[...]).astype(BF16)

    cos2 = cos_ref[...]
    sin2 = sin_ref[...]
    n_chunks = tm // CHUNK

    def proj(off, g):
        return _dot(h, win_ref[:, off + g * GD:off + (g + 1) * GD])

    for pair in range(Q_W // GD):
        qq = proj(0, pair)
        kk = proj(Q_W, pair)
        for e in range(GD // DK):
            j = pair * (GD // DK) + e
            ks = slice(j * DK, (j + 1) * DK)
            q_s[:, ks] = _rotary(qq[:, e * DK:(e + 1) * DK], cos2, sin2).astype(BF16)
            kj = _rotary(kk[:, e * DK:(e + 1) * DK], cos2, sin2) * (DK ** -0.5)
            k_s[:, ks] = kj.astype(BF16)
            for c in range(n_chunks):
                rs = slice(c * CHUNK, (c + 1) * CHUNK)
                kz_s[rs, ks] = (kj[rs, :] * zeta_ref[j]).astype(BF16)
    for g in range(GROUPS):
        gs = slice(g * GD, (g + 1) * GD)
        v_s[:, gs] = proj(OFF_V, g).astype(BF16)
        gate_s[:, gs] = _silu(proj(OFF_G, g))
        u_s[:, gs] = _gelu(proj(OFF_U, g))
        sn_s[:, gs] = _layernorm(_gelu(proj(OFF_S, g)), lng_ref[:, gs], lnb_ref[:, gs]).astype(BF16)

    row = lax.broadcasted_iota(jnp.int32, (CHUNK, CHUNK), 0)
    col = lax.broadcasted_iota(jnp.int32, (CHUNK, CHUNK), 1)
    tril = row >= col

    pairs = [(c, hh) for c in range(n_chunks) for hh in range(HEADS)]
    for c, hh in pairs:
        rs = slice(c * CHUNK, (c + 1) * CHUNK)
        ks = slice(hh * DK, (hh + 1) * DK)
        inner = _dot_nt(q_s[rs, ks], k_s[rs, ks]) * dmask_ref[hh]
        inner_s[c * HEADS + hh] = inner.astype(BF16)
    for c, hh in pairs:
        rs = slice(c * CHUNK, (c + 1) * CHUNK)
        kv_s[c * HEADS + hh] = _dot_tn(kz_s[rs, hh * DK:(hh + 1) * DK], v_s[rs, hh * DV:(hh + 1) * DV])
    for hh in range(HEADS):
        st = state_ref[0, 0, hh]
        for c in range(n_chunks):
            sb_s[c * HEADS + hh] = st.astype(BF16)
            st = cd_ref[hh] * st + kv_s[c * HEADS + hh]
        state_ref[0, 0, hh] = st
    for c, hh in pairs:
        rs = slice(c * CHUNK, (c + 1) * CHUNK)
        vs = slice(hh * DV, (hh + 1) * DV)
        o = _dot(inner_s[c * HEADS + hh], v_s[rs, vs])
        o = o + _dot(q_s[rs, hh * DK:(hh + 1) * DK], sb_s[c * HEADS + hh]) * xi_ref[hh]
        on = _groupnorm(o, gn_ref[:, vs])
        mix_s[rs, vs] = (gate_s[rs, vs] * on).astype(BF16)

    for c in range(n_chunks):
        rs = slice(c * CHUNK, (c + 1) * CHUNK)
        for g in range(GROUPS):
            gs = slice(g * GD, (g + 1) * GD)
            wg = jnp.where(tril, wsgu_ref[g], 0.0).astype(BF16)
            mixed = _dot(wg, sn_s[rs, gs]) + bsgu_ref[g]
            mix_s[rs, V_W + g * GD:V_W + (g + 1) * GD] = (u_s[rs, gs] * mixed).astype(BF16)

    x1_ref[0] = x + _dot(mix_s[...], wout_ref[...])


def _mix_prompt(x, gmix, win, wout, cos2, sin2, dmask, xi_b, zeta_b, cd, gn, lng, lnb, wsgu, bsgu_b,
                decay1, qs, ks, vs, sstate):
    b, t, d = x.shape
    tm = TM_MIX
    nt = t // tm
    n_pairs = (tm // CHUNK) * HEADS
    grid = (b, nt)
    ms = qs.shape[0]
    n_hp = HEADS // HEAD_PAIR
    assert b * nt == (ms // BB_RET) * n_hp
    by_pair = lambda a, w: a.reshape(ms, n_hp, w).transpose(1, 0, 2)
    qs2, ks2, vs2 = by_pair(qs, HEAD_PAIR * DK), by_pair(ks, HEAD_PAIR * DK), by_pair(vs, HEAD_PAIR * DV)
    pair_blk = lambda i, j: ((i * nt + j) % n_hp, (i * nt + j) // n_hp, 0)
    state_blk = lambda i, j: (0, (i * nt + j) // n_hp, (i * nt + j) % n_hp, 0, 0)
    in_specs = [
        _smem_spec(),
        _smem_spec(),
        pl.BlockSpec((1, tm, d), lambda i, j: (i, j, 0)),
        _const_spec((1, d)),
        _const_spec((d, D_IN)),
        _const_spec((V_W + U_W, d)),
        pl.BlockSpec((tm, DK), lambda i, j: (j, 0)),
        pl.BlockSpec((tm, DK), lambda i, j: (j, 0)),
        _const_spec((HEADS, CHUNK, CHUNK)),
        _const_spec((HEADS, CHUNK, DV)),
        _const_spec((HEADS, CHUNK, DK)),
        _const_spec((1, V_W)),
        _const_spec((1, U_W)),
        _const_spec((1, U_W)),
        _const_spec((GROUPS, CHUNK, CHUNK)),
        _const_spec((GROUPS, CHUNK, GD)),
        pl.BlockSpec((1, BB_RET, HEAD_PAIR * DK), pair_blk),
        pl.BlockSpec((1, BB_RET, HEAD_PAIR * DK), pair_blk),
        pl.BlockSpec((1, BB_RET, HEAD_PAIR * DV), pair_blk),
        pl.BlockSpec((1, BB_RET, HEAD_PAIR, DK, DV), state_blk),
    ]
    out_specs = [
        pl.BlockSpec((1, tm, d), lambda i, j: (i, j, 0)),
        pl.BlockSpec((1, 1, HEADS, DK, DV), lambda i, j: (0, i, 0, 0, 0)),
        pl.BlockSpec((1, BB_RET, HEAD_PAIR, DK, DV), state_blk),
        pl.BlockSpec((1, BB_RET, HEAD_PAIR * DV), pair_blk),
    ]
    out_shape = [
        jax.ShapeDtypeStruct((b, t, d), F32),
        jax.ShapeDtypeStruct((1, b, HEADS, DK, DV), F32),
        jax.ShapeDtypeStruct(sstate.shape, F32),
        jax.ShapeDtypeStruct((n_hp, ms, HEAD_PAIR * DV), F32),
    ]
    scratch = [
        pltpu.VMEM((tm, Q_W), BF16),
        pltpu.VMEM((tm, Q_W), BF16),
        pltpu.VMEM((tm, Q_W), BF16),
        pltpu.VMEM((tm, V_W), BF16),
        pltpu.VMEM((tm, V_W), F32),
        pltpu.VMEM((tm, U_W), F32),
        pltpu.VMEM((tm, U_W), BF16),
        pltpu.VMEM((tm, V_W + U_W), BF16),
        pltpu.VMEM((n_pairs, CHUNK, CHUNK), BF16),
        pltpu.VMEM((n_pairs, DK, DV), F32),
        pltpu.VMEM((n_pairs, DK, DV), BF16),
    ]
    x1, state_p, state_s, o2 = pl.pallas_call(
        _mix_prompt_kernel, grid=grid, in_specs=in_specs, out_specs=out_specs,
        out_shape=out_shape, scratch_shapes=scratch,
        compiler_params=_params("arbitrary", "arbitrary"), name="mix_prompt",
    )(cd, decay1, x, gmix, win, wout, cos2, sin2, dmask, xi_b, zeta_b, gn, lng, lnb, wsgu, bsgu_b,
      qs2, ks2, vs2, sstate)
    return x1, state_p, state_s, o2.transpose(1, 0, 2).reshape(ms, V_W)


def _ffn_tile(x, g, gfin, w1_s, w2_s, final, before_slice=None):
    h = _rms(x, g).astype(BF16)
    acc = x
    for j in range(FFN_SLICES):
        if before_slice is not None:
            before_slice(j)
        js = slice(j * FFN_HS, (j + 1) * FFN_HS)
        f = _dot(h, w1_s[:, js])
        a = jnp.square(jnp.maximum(f, 0.0)).astype(BF16)
        acc = acc + _dot(a, w2_s[js, :])
    if final:
        acc = _rms(acc, gfin)
    return acc


def _ffn_kernel(*refs, layer, final, n_tiles, n_casts):
    xp_ref, xs_ref, g_ref, gfin_ref, w1_hbm, w2_hbm = refs[:6]
    cast_in = refs[6:6 + n_casts]
    yp_ref, ys_ref = refs[6 + n_casts:8 + n_casts]
    cast_out = refs[8 + n_casts:8 + 2 * n_casts]
    w1_s, w2_s, st1, st2, sem = refs[8 + 2 * n_casts:]
    i = pl.program_id(0)

    if n_casts:
        @pl.when(jnp.logical_and(i >= 1, i <= CAST_STEPS))
        def _():
            for src, dst in zip(cast_in, cast_out):
                dst[...] = src[...].astype(BF16)

    def slice_copies(j):
        slot = j % 2
        js = pl.ds(j * FFN_HS, FFN_HS)
        return (pltpu.make_async_copy(w1_hbm.at[layer, :, js], st1.at[slot], sem.at[0, slot]),
                pltpu.make_async_copy(w2_hbm.at[layer, js, :], st2.at[slot], sem.at[1, slot]))

    def fetch_slice(j):
        js = slice(j * FFN_HS, (j + 1) * FFN_HS)
        for c in slice_copies(j):
            c.wait()
        w1_s[:, js] = st1[j % 2].astype(BF16)
        w2_s[js, :] = st2[j % 2].astype(BF16)
        if j + 2 < FFN_SLICES:
            for c in slice_copies(j + 2):
                c.start()

    @pl.when(i == 0)
    def _():
        for j in range(min(2, FFN_SLICES)):
            for c in slice_copies(j):
                c.start()
        yp_ref[...] = _ffn_tile(xp_ref[...], g_ref[...], gfin_ref[...], w1_s, w2_s, final, fetch_slice)

    @pl.when(jnp.logical_and(i > 0, i < n_tiles))
    def _():
        yp_ref[...] = _ffn_tile(xp_ref[...], g_ref[...], gfin_ref[...], w1_s, w2_s, final)

    @pl.when(i == n_tiles)
    def _():
        ys_ref[...] = _ffn_tile(xs_ref[...], g_ref[...], gfin_ref[...], w1_s, w2_s, final)


def _ffn(xp, xs, g, w1, w2, gfin, layer, final, casts=()):
    m, d = xp.shape
    ms = xs.shape[0]
    tm = TM_FFN
    n_tiles = m // tm
    assert n_tiles > CAST_STEPS
    tile = lambda i: (jnp.minimum(i, n_tiles - 1), 0)
    cast_blk = lambda i: (jnp.clip(i - 1, 0, CAST_STEPS - 1), 0)
    cast_specs = [pl.BlockSpec((c.shape[0] // CAST_STEPS, c.shape[1]), cast_blk) for c in casts]
    return pl.pallas_call(
        functools.partial(_ffn_kernel, layer=layer, final=final, n_tiles=n_tiles, n_casts=len(casts)),
        grid=(n_tiles + 1,),
        in_specs=[pl.BlockSpec((tm, d), tile), _const_spec((ms, d)),
                  _const_spec((1, d)), _const_spec((1, d)),
                  pl.BlockSpec(memory_space=pl.ANY), pl.BlockSpec(memory_space=pl.ANY)] + cast_specs,
        out_specs=[pl.BlockSpec((tm, d), tile), pl.BlockSpec((ms, d), lambda i: (0, 0))] + cast_specs,
        out_shape=[jax.ShapeDtypeStruct((m, d), F32), jax.ShapeDtypeStruct((ms, d), F32)]
        + [jax.ShapeDtypeStruct(c.shape, BF16) for c in casts],
        scratch_shapes=[pltpu.VMEM((d, D_FF), BF16), pltpu.VMEM((D_FF, d), BF16),
                        pltpu.VMEM((2, d, FFN_HS), F32), pltpu.VMEM((2, FFN_HS, d), F32),
                        pltpu.SemaphoreType.DMA((2, 2))],
        compiler_params=_params("arbitrary"), name="ffn_final" if final else "ffn",
    )(xp, xs, g, gfin, w1, w2, *casts)


def _glu_pieces(x, g, win_ref, bin_ref):
    h = _rms(x, g).astype(BF16)
    for c in range(D_MODEL // GD):
        ca = slice(c * GD, (c + 1) * GD)
        cg = slice(D_MODEL + c * GD, D_MODEL + (c + 1) * GD)
        a = _dot(h, win_ref[:, ca]) + bin_ref[:, ca]
        gt = _dot(h, win_ref[:, cg]) + bin_ref[:, cg]
        yield c, a * _sigmoid(gt)


def _glu_in(x, g, win_ref, bin_ref):
    return jnp.concatenate([p for _, p in _glu_pieces(x, g, win_ref, bin_ref)], axis=-1)


def _from_token_rows(buf, first_token, n):
    return jnp.concatenate(
        [buf[pl.ds(first_token * LANE_TILES + g, n, stride=LANE_TILES), :] for g in range(LANE_TILES)], axis=-1)


def _conv_block(r, full_s, conv_s, w3_ref, b3_ref):
    off0 = HIST_PAD - HIST
    acc = jnp.broadcast_to(b3_ref[...][None], (CONV_RB, LANE_TILES, 128))
    for j in range(CONV_WIDTH):
        start = pl.multiple_of((r * CONV_RB + j + off0) * LANE_TILES, LANE_TILES)
        xj = full_s[pl.ds(start, CONV_RB * LANE_TILES), :].reshape(CONV_RB, LANE_TILES, 128)
        acc = acc + xj * w3_ref[j][None]
    out0 = pl.multiple_of(r * CONV_RB * LANE_TILES, CONV_RB * LANE_TILES)
    conv_s[pl.ds(out0, CONV_RB * LANE_TILES), :] = acc.reshape(CONV_RB * LANE_TILES, 128)


def _conv_prompt_kernel(x_ref, xprev_ref, gmix_ref, win_ref, bin_ref, dww_ref, dwb_ref, w3_ref, b3_ref,
                        lng_ref, lnb_ref, wout_ref, bout_ref, shist_ref, sxg_ref,
                        o_ref, hist_ref, snewhist_ref, sconv_ref,
                        full_s, conv_s, *, n_tiles, tiles_per_seq, sample_every):
    s = pl.program_id(0)
    tm = x_ref.shape[0]
    off0 = HIST_PAD - HIST
    hist_rows = HIST_PAD * LANE_TILES

    @pl.when(jnp.logical_and(s % sample_every == 0, s < n_tiles))
    def _():
        _conv_sample_state_update(shist_ref, sxg_ref, dww_ref, dwb_ref, snewhist_ref, sconv_ref)

    def front():
        for c, piece in _glu_pieces(x_ref[...], gmix_ref[...], win_ref, bin_ref):
            for e in range(GD // 128):
                g = c * (GD // 128) + e
                full_s[pl.ds(hist_rows + g, tm, stride=LANE_TILES), :] = piece[:, e * 128:(e + 1) * 128]

    def back():
        conv = _from_token_rows(conv_s, 0, tm)
        c = _silu(_layernorm(conv, lng_ref[...], lnb_ref[...])).astype(BF16)
        o_ref[...] = xprev_ref[...] + _dot(c, wout_ref[...]) + bout_ref[...]

    @pl.when(s % tiles_per_seq == 0)
    def _():
        full_s[0:hist_rows, :] = jnp.zeros((hist_rows, 128), F32)

    @pl.when(s == 0)
    def _():
        front()

    @pl.when(jnp.logical_and(s > 0, s < n_tiles))
    def _():
        front()
        back()

    @pl.when(s == n_tiles)
    def _():
        back()

    @pl.when(s < n_tiles)
    def _():
        def body(r, carry):
            _conv_block(r, full_s, conv_s, w3_ref, b3_ref)
            return carry
        lax.fori_loop(0, tm // CONV_RB, body, 0)

        @pl.when(s % tiles_per_seq == tiles_per_seq - 1)
        def _():
            hist_ref[0, 0] = _from_token_rows(full_s, tm + off0, HIST)

        full_s[0:hist_rows, :] = full_s[tm * LANE_TILES:tm * LANE_TILES + hist_rows, :]


def _conv_prompt(x, gmix, win, bin_, dww, dwb, lng, lnb, wout, bout, shist_t, sxg):
    b, t, d = x.shape
    tm = TM_CONV
    n_tiles = (b * t) // tm
    tiles_per_seq = t // tm
    ms = sxg.shape[0]
    n_sblocks = ms // BB_CONV
    sample_every = n_tiles // n_sblocks
    assert sample_every * n_sblocks == n_tiles
    sblock = lambda s: jnp.minimum(s // sample_every, n_sblocks - 1)
    in_specs = [
        pl.BlockSpec((tm, d), lambda s: (jnp.minimum(s, n_tiles - 1), 0)),
        pl.BlockSpec((tm, d), lambda s: (jnp.maximum(s - 1, 0), 0)),
        _const_spec((1, d)), _const_spec((d, 2 * d)), _const_spec((1, 2 * d)),
        _const_spec((CONV_WIDTH, d)), _const_spec((1, d)),
        _const_spec((CONV_WIDTH, LANE_TILES, 128)), _const_spec((LANE_TILES, 128)),
        _const_spec((1, d)), _const_spec((1, d)),
        _const_spec((d, d)), _const_spec((1, d)),
        pl.BlockSpec((HIST, BB_CONV, d), lambda s: (0, sblock(s), 0)),
        pl.BlockSpec((BB_CONV, d), lambda s: (sblock(s), 0)),
    ]
    out_specs = [
        pl.BlockSpec((tm, d), lambda s: (jnp.maximum(s - 1, 0), 0)),
        pl.BlockSpec((1, 1, HIST, d), lambda s: (0, jnp.minimum(s, n_tiles - 1) // tiles_per_seq, 0, 0)),
        pl.BlockSpec((HIST, BB_CONV, d), lambda s: (0, sblock(s), 0)),
        pl.BlockSpec((BB_CONV, d), lambda s: (sblock(s), 0)),
    ]
    out_shape = [jax.ShapeDtypeStruct((b * t, d), F32),
                 jax.ShapeDtypeStruct((1, b, HIST, d), F32),
                 jax.ShapeDtypeStruct(shist_t.shape, F32),
                 jax.ShapeDtypeStruct((ms, d), F32)]
    scratch = [
        pltpu.VMEM(((tm + HIST_PAD) * LANE_TILES, 128), F32),
        pltpu.VMEM((tm * LANE_TILES, 128), F32),
    ]
    x2d = x.reshape(b * t, d)
    w3 = dww.reshape(CONV_WIDTH, LANE_TILES, 128)
    b3 = dwb.reshape(LANE_TILES, 128)
    return pl.pallas_call(
        functools.partial(_conv_prompt_kernel, n_tiles=n_tiles, tiles_per_seq=tiles_per_seq,
                          sample_every=sample_every),
        grid=(n_tiles + 1,), in_specs=in_specs, out_specs=out_specs,
        out_shape=out_shape, scratch_shapes=scratch,
        compiler_params=_params("arbitrary"), name="conv_prompt",
    )(x2d, x2d, gmix, win, bin_, dww, dwb, w3, b3, lng, lnb, wout, bout, shist_t, sxg)


def _mix_sample_in_kernel(sgu_ref, x_ref, gmix_ref, win_ref, cos_ref, sin_ref, lng_ref, lnb_ref,
                          q_ref, k_ref, v_ref, gate_ref, sgu_out_ref, rows_ref):
    h = _rms(x_ref[...], gmix_ref[...]).astype(BF16)
    qk = _dot(h, win_ref[:, 0:OFF_V])
    cos2 = cos_ref[...]
    sin2 = sin_ref[...]
    for j in range(HEADS):
        q_ref[:, j * DK:(j + 1) * DK] = _rotary(qk[:, j * DK:(j + 1) * DK], cos2, sin2)
        kj = qk[:, Q_W + j * DK:Q_W + (j + 1) * DK]
        k_ref[:, j * DK:(j + 1) * DK] = _rotary(kj, cos2, sin2) * (DK ** -0.5)
    v_ref[...] = _dot(h, win_ref[:, OFF_V:OFF_G])
    gate_ref[...] = _silu(_dot(h, win_ref[:, OFF_G:OFF_U]))
    u = _gelu(_dot(h, win_ref[:, OFF_U:OFF_S]))
    s = _gelu(_dot(h, win_ref[:, OFF_S:D_IN]))
    for g in range(GROUPS):
        gs = slice(g * GD, (g + 1) * GD)
        sn = _layernorm(s[:, gs], lng_ref[:, gs], lnb_ref[:, gs])
        rows_ref[:, gs] = sn
        sgu_out_ref[:, gs] = u[:, gs] * (sgu_ref[0, g] * sn + sgu_ref[1, g])


def _mix_sample_in(x, gmix, win, cos2, sin2, lng, lnb, sgu_wb):
    m, d = x.shape
    shp = lambda n: jax.ShapeDtypeStruct((m, n), F32)
    vm = lambda: pl.BlockSpec(memory_space=pltpu.VMEM)
    return pl.pallas_call(
        _mix_sample_in_kernel,
        in_specs=[_smem_spec()] + [vm() for _ in range(7)],
        out_specs=[vm() for _ in range(6)],
        out_shape=[shp(Q_W), shp(Q_W), shp(V_W), shp(V_W), shp(U_W), shp(U_W)],
        compiler_params=pltpu.CompilerParams(vmem_limit_bytes=VMEM_LIMIT), name="mix_sample_in",
    )(sgu_wb, x, gmix, win, cos2, sin2, lng, lnb)


def _mix_sample_out_kernel(x_ref, o_ref, gate_ref, sgu_ref, gn_ref, wout_ref, x1_ref):
    o = o_ref[...]
    parts = []
    for hh in range(HEADS):
        vs = slice(hh * DV, (hh + 1) * DV)
        parts.append((gate_ref[:, vs] * _groupnorm(o[:, vs], gn_ref[:, vs])).astype(BF16))
    y = _dot(jnp.concatenate(parts, axis=-1), wout_ref[0:V_W, :])
    y = y + _dot(sgu_ref[...].astype(BF16), wout_ref[V_W:V_W + U_W, :])
    x1_ref[...] = x_ref[...] + y


def _mix_sample_out(x, o, gate, sgu_out, gn, wout):
    m, d = x.shape
    vm = lambda: pl.BlockSpec(memory_space=pltpu.VMEM)
    return pl.pallas_call(
        _mix_sample_out_kernel, in_specs=[vm() for _ in range(6)], out_specs=vm(),
        out_shape=jax.ShapeDtypeStruct((m, d), F32),
        compiler_params=pltpu.CompilerParams(vmem_limit_bytes=VMEM_LIMIT), name="mix_sample_out",
    )(x, o, gate, sgu_out, gn, wout)


def _conv_sample_in_kernel(x_ref, g_ref, win_ref, bin_ref, xg_ref):
    xg_ref[...] = _glu_in(x_ref[...], g_ref[...], win_ref, bin_ref)


def _conv_sample_in(x, g, win, bin_):
    m, d = x.shape
    vm = lambda: pl.BlockSpec(memory_space=pltpu.VMEM)
    return pl.pallas_call(
        _conv_sample_in_kernel, in_specs=[vm() for _ in range(4)], out_specs=vm(),
        out_shape=jax.ShapeDtypeStruct((m, d), F32),
        compiler_params=pltpu.CompilerParams(vmem_limit_bytes=VMEM_LIMIT), name="conv_sample_in",
    )(x, g, win, bin_)


def _conv_sample_state_update(hist_ref, xg_ref, dww_ref, dwb_ref, newhist_ref, conv_ref):
    xg = xg_ref[...]
    acc = dwb_ref[...] + xg * dww_ref[HIST:CONV_WIDTH, :]
    for j in range(HIST):
        acc = acc + hist_ref[j] * dww_ref[j:j + 1, :]
    conv_ref[...] = acc
    for j in range(HIST - 1):
        newhist_ref[j] = hist_ref[j + 1]
    newhist_ref[HIST - 1] = xg


def _conv_sample_out_kernel(x_ref, conv_ref, lng_ref, lnb_ref, wout_ref, bout_ref, o_ref):
    c = _silu(_layernorm(conv_ref[...], lng_ref[...], lnb_ref[...])).astype(BF16)
    o_ref[...] = x_ref[...] + _dot(c, wout_ref[...]) + bout_ref[...]


def _conv_sample_out(x, conv, lng, lnb, wout, bout):
    m, d = x.shape
    vm = lambda: pl.BlockSpec(memory_space=pltpu.VMEM)
    return pl.pallas_call(
        _conv_sample_out_kernel, in_specs=[vm() for _ in range(6)], out_specs=vm(),
        out_shape=jax.ShapeDtypeStruct((m, d), F32),
        compiler_params=pltpu.CompilerParams(vmem_limit_bytes=VMEM_LIMIT), name="conv_sample_out",
    )(x, conv, lng, lnb, wout, bout)


def _rope_tables(pos):
    half = DK // 2
    inv = ROPE_BASE ** (-jnp.arange(half, dtype=F32) / half)
    ang = pos.astype(F32)[:, None] * inv[None, :]
    cos = jnp.cos(ang)
    sin = jnp.sin(ang)
    return jnp.concatenate([cos, cos], axis=-1), jnp.concatenate([-sin, sin], axis=-1)


def _decay_tables(c):
    log_g = jnp.log1p(-(2.0 ** (-5.0 - jnp.arange(HEADS, dtype=F32))))
    idx = jnp.arange(c, dtype=F32)
    diff = idx[:, None] - idx[None, :]
    dmask = jnp.where(diff >= 0, jnp.exp(jnp.maximum(diff, 0.0)[None] * log_g[:, None, None]), 0.0)
    xi = jnp.exp((idx[None, :] + 1.0) * log_g[:, None])
    zeta = jnp.exp((c - 1.0 - idx[None, :]) * log_g[:, None])
    chunk_decay = jnp.exp(c * log_g)
    return dmask, xi, zeta, chunk_decay


def kernel(x_prompt, x_sample, state_ret, state_conv, norm_mix_g, norm_ffn_g, final_norm_g, ab_w_in, ab_w_out, ret_gn_g, sgu_ln_g, sgu_ln_b, sgu_w, sgu_b, conv_w_in, conv_b_in, conv_dw_w, conv_dw_b, conv_ln_g, conv_ln_b, conv_w_out, conv_b_out, ffn_w1, ffn_w2):
    b, t, d = x_prompt.shape
    m_s = x_sample.shape[0]
    row = lambda a: a.reshape(1, -1)

    win = ab_w_in[0].astype(BF16)
    wout = ab_w_out[0].astype(BF16)
    gmix = [row(norm_mix_g[l]) for l in range(2)]
    gffn = [row(norm_ffn_g[l]) for l in range(2)]
    gfin = row(final_norm_g)
    gn, lng, lnb = row(ret_gn_g[0]), row(sgu_ln_g[0]), row(sgu_ln_b[0])
    cbin, cdwb, clng, clnb, cbout = (row(conv_b_in[0]), row(conv_dw_b[0]), row(conv_ln_g[0]),
                                     row(conv_ln_b[0]), row(conv_b_out[0]))

    cos_p, sin_p = _rope_tables(jnp.arange(t))
    dmask, xi, zeta, cd = _decay_tables(CHUNK)
    xi_b = jnp.broadcast_to(xi[:, :, None], (HEADS, CHUNK, DV))
    zeta_b = jnp.broadcast_to(zeta[:, :, None], (HEADS, CHUNK, DK))
    bsgu_b = jnp.broadcast_to(sgu_b[0][:, :CHUNK, None], (GROUPS, CHUNK, GD))

    xs = x_sample.reshape(m_s, d)
    cos_s, sin_s = _rope_tables(PAST_LEN + jnp.arange(1))
    _, _, _, decay1 = _decay_tables(1)
    sgu_wb = jnp.stack([sgu_w[0][:, 0, 0], sgu_b[0][:, 0]])
    q, k, v, gate, sgu_out, rows = _mix_sample_in(xs, gmix[0], win, cos_s, sin_s, lng, lnb, sgu_wb)
    x1, ret_state_p, ret_state_s, o = _mix_prompt(x_prompt, gmix[0], win, wout, cos_p, sin_p, dmask, xi_b, zeta_b,
                                                  cd, gn, lng, lnb, sgu_w[0], bsgu_b, decay1, q, k, v, state_ret)
    xs1 = _mix_sample_out(xs, o, gate, sgu_out, gn, wout)

    x2, xs2, cwin, cwout = _ffn(x1.reshape(b * t, d), xs1, gffn[0], ffn_w1, ffn_w2, gfin, 0, False,
                                casts=(conv_w_in[0], conv_w_out[0]))

    xg = _conv_sample_in(xs2, gmix[1], cwin, cbin)
    hist_t = jnp.transpose(state_conv[0], (1, 0, 2))
    x3, conv_state_p, newhist_t, conv = _conv_prompt(x2.reshape(b, t, d), gmix[1], cwin, cbin, conv_dw_w[0], cdwb,
                                                     clng, clnb, cwout, cbout, hist_t, xg)
    conv_state_s = jnp.transpose(newhist_t, (1, 0, 2))[None]
    xs3 = _conv_sample_out(xs2, conv, clng, clnb, cwout, cbout)

    y_prompt, y_sample = _ffn(x3, xs3, gffn[1], ffn_w1, ffn_w2, gfin, 1, True)

    return (y_prompt.reshape(b, t, d), y_sample.reshape(m_s, 1, d), ret_state_p, ret_state_s,
            rows.reshape(1, m_s, 1, U_W), conv_state_p, conv_state_s)
```

```python
import functools

import jax
import jax.numpy as jnp
from jax import lax
from jax.experimental import pallas as pl
from jax.experimental.pallas import tpu as pltpu

F32 = jnp.float32
BF16 = jnp.bfloat16

D_MODEL = 1024
HEADS = 4
DK = 128
DV = 256
GROUPS = 4
GD = 256
CHUNK = 128
ROPE_BASE = 10000.0
CONV_WIDTH = 31
HIST = CONV_WIDTH - 1
D_FF = 4 * D_MODEL
RMS_EPS = 1e-6
LN_EPS = 1e-5
PAST_LEN = 16384

Q_W = HEADS * DK
V_W = HEADS * DV
U_W = GROUPS * GD
OFF_V = 2 * Q_W
OFF_G = OFF_V + V_W
OFF_U = OFF_G + V_W
OFF_S = OFF_U + U_W
D_IN = OFF_S + U_W

SUBLANES = 8
VMEM_LIMIT = 56 * 1024 * 1024

TM_MIX = 512
TM_FFN = 512
TM_CONV = 512
CONV_RB = 32
LANE_TILES = D_MODEL // 128
HIST_PAD = 32
FFN_SLICES = 4
FFN_HS = D_FF // FFN_SLICES
CAST_STEPS = 8
BB_RET = 8
HEAD_PAIR = 2
BB_CONV = 16


def _params(*sem):
    return pltpu.CompilerParams(dimension_semantics=sem, vmem_limit_bytes=VMEM_LIMIT)


def _const_spec(shape):
    n = len(shape)
    return pl.BlockSpec(shape, lambda *_: (0,) * n, pipeline_mode=pl.Buffered(1))


def _smem_spec():
    return pl.BlockSpec(memory_space=pltpu.SMEM)


def _rms(x, g):
    ms = jnp.mean(x * x, axis=-1, keepdims=True)
    return x * lax.rsqrt(ms + RMS_EPS) * g


def _layernorm(x, g, b):
    mu = jnp.mean(x, axis=-1, keepdims=True)
    xc = x - mu
    var = jnp.mean(xc * xc, axis=-1, keepdims=True)
    return xc * lax.rsqrt(var + LN_EPS) * g + b


def _groupnorm(x, g):
    mu = jnp.mean(x, axis=-1, keepdims=True)
    xc = x - mu
    var = jnp.mean(xc * xc, axis=-1, keepdims=True)
    return xc * lax.rsqrt(var + LN_EPS) * g


def _sigmoid(x):
    return 1.0 / (1.0 + jnp.exp(-x))


def _silu(x):
    return x * _sigmoid(x)


def _gelu(x):
    return 0.5 * x * (1.0 + lax.erf(x * 0.7071067811865476))


def _dot(a, b):
    return jnp.dot(a, b, preferred_element_type=F32)


def _dot_nt(a, b):
    return lax.dot_general(a, b, (((1,), (1,)), ((), ())), preferred_element_type=F32)


def _dot_tn(a, b):
    return lax.dot_general(a, b, (((0,), (0,)), ((), ())), preferred_element_type=F32)


def _rotary(blk, cos2, sin2):
    return blk * cos2 + pltpu.roll(blk, DK // 2, 1) * sin2


def _sample_state_update(step, decay_ref, qs_ref, ks_ref, vs_ref, sin_ref, sout_ref, os_ref):
    pair = step % (HEADS // HEAD_PAIR)
    for li in range(HEAD_PAIR):
        ks = slice(li * DK, (li + 1) * DK)
        vs = slice(li * DV, (li + 1) * DV)
        qt = qs_ref[0, :, ks].T
        kt = ks_ref[0, :, ks].T
        gam = decay_ref[pair * HEAD_PAIR + li]
        for b in range(BB_RET):
            st = sin_ref[0, b, li]
            sn = gam * st + kt[:, b:b + 1] * vs_ref[0, b:b + 1, vs]
            sout_ref[0, b, li] = sn
            os_ref[0, b:b + 1, vs] = jnp.sum(qt[:, b:b + 1] * sn, axis=0, keepdims=True)


def _mix_prompt_kernel(cd_ref, decay_ref, x_ref, gmix_ref, win_ref, wout_ref, cos_ref, sin_ref,
                       dmask_ref, xi_ref, zeta_ref, gn_ref, lng_ref, lnb_ref,
                       wsgu_ref, bsgu_ref, qs_ref, ks_ref, vs_ref, sstate_ref,
                       x1_ref, state_ref, sstate_out_ref, os_ref,
                       q_s, k_s, kz_s, v_s, gate_s, u_s, sn_s, mix_s, inner_s, kv_s, sb_s):
    t = pl.program_id(1)
    tm = x_ref.shape[1]

    @pl.when(t == 0)
    def _():
        state_ref[...] = jnp.zeros_like(state_ref)

    _sample_state_update(pl.program_id(0) * pl.num_programs(1) + t, decay_ref, qs_ref, ks_ref, vs_ref,
                         sstate_ref, sstate_out_ref, os_ref)

    x = x_ref[0]
    h = _rms(x, gmix_ref[...]).astype(BF16)

    cos2 = cos_ref[...]
    sin2 = sin_ref[...]
    n_chunks = tm // CHUNK

    def proj(off, g):
        return _dot(h, win_ref[:, off + g * GD:off + (g + 1) * GD])

    for pair in range(Q_W // GD):
        qq = proj(0, pair)
        kk = proj(Q_W, pair)
        for e in range(GD // DK):
            j = pair * (GD // DK) + e
            ks = slice(j * DK, (j + 1) * DK)
            q_s[:, ks] = _rotary(qq[:, e * DK:(e + 1) * DK], cos2, sin2).astype(BF16)
            kj = _rotary(kk[:, e * DK:(e + 1) * DK], cos2, sin2) * (DK ** -0.5)
            k_s[:, ks] = kj.astype(BF16)
            for c in range(n_chunks):
                rs = slice(c * CHUNK, (c + 1) * CHUNK)
                kz_s[rs, ks] = (kj[rs, :] * zeta_ref[j]).astype(BF16)
    for g in range(GROUPS):
        gs = slice(g * GD, (g + 1) * GD)
        v_s[:, gs] = proj(OFF_V, g).astype(BF16)
        gate_s[:, gs] = _silu(proj(OFF_G, g))
        u_s[:, gs] = _gelu(proj(OFF_U, g))
        sn_s[:, gs] = _layernorm(_gelu(proj(OFF_S, g)), lng_ref[:, gs], lnb_ref[:, gs]).astype(BF16)

    row = lax.broadcasted_iota(jnp.int32, (CHUNK, CHUNK), 0)
    col = lax.broadcasted_iota(jnp.int32, (CHUNK, CHUNK), 1)
    tril = row >= col

    pairs = [(c, hh) for c in range(n_chunks) for hh in range(HEADS)]
    for c, hh in pairs:
        rs = slice(c * CHUNK, (c + 1) * CHUNK)
        ks = slice(hh * DK, (hh + 1) * DK)
        inner = _dot_nt(q_s[rs, ks], k_s[rs, ks]) * dmask_ref[hh]
        inner_s[c * HEADS + hh] = inner.astype(BF16)
    for c, hh in pairs:
        rs = slice(c * CHUNK, (c + 1) * CHUNK)
        kv_s[c * HEADS + hh] = _dot_tn(kz_s[rs, hh * DK:(hh + 1) * DK], v_s[rs, hh * DV:(hh + 1) * DV])
    for hh in range(HEADS):
        st = state_ref[0, 0, hh]
        for c in range(n_chunks):
            sb_s[c * HEADS + hh] = st.astype(BF16)
            st = cd_ref[hh] * st + kv_s[c * HEADS + hh]
        state_ref[0, 0, hh] = st
    for c, hh in pairs:
        rs = slice(c * CHUNK, (c + 1) * CHUNK)
        vs = slice(hh * DV, (hh + 1) * DV)
        o = _dot(inner_s[c * HEADS + hh], v_s[rs, vs])
        o = o + _dot(q_s[rs, hh * DK:(hh + 1) * DK], sb_s[c * HEADS + hh]) * xi_ref[hh]
        on = _groupnorm(o, gn_ref[:, vs])
        mix_s[rs, vs] = (gate_s[rs, vs] * on).astype(BF16)

    for c in range(n_chunks):
        rs = slice(c * CHUNK, (c + 1) * CHUNK)
        for g in range(GROUPS):
            gs = slice(g * GD, (g + 1) * GD)
            wg = jnp.where(tril, wsgu_ref[g], 0.0).astype(BF16)
            mixed = _dot(wg, sn_s[rs, gs]) + bsgu_ref[g]
            mix_s[rs, V_W + g * GD:V_W + (g + 1) * GD] = (u_s[rs, gs] * mixed).astype(BF16)

    x1_ref[0] = x + _dot(mix_s[...], wout_ref[...])


def _mix_prompt(x, gmix, win, wout, cos2, sin2, dmask, xi_b, zeta_b, cd, gn, lng, lnb, wsgu, bsgu_b,
                decay1, qs2, ks2, vs2, sstate):
    b, t, d = x.shape
    tm = TM_MIX
    nt = t // tm
    n_pairs = (tm // CHUNK) * HEADS
    grid = (b, nt)
    n_hp, ms, _ = qs2.shape
    assert b * nt == (ms // BB_RET) * n_hp
    pair_blk = lambda i, j: ((i * nt + j) % n_hp, (i * nt + j) // n_hp, 0)
    state_blk = lambda i, j: (0, (i * nt + j) // n_hp, (i * nt + j) % n_hp, 0, 0)
    in_specs = [
        _smem_spec(),
        _smem_spec(),
        pl.BlockSpec((1, tm, d), lambda i, j: (i, j, 0)),
        _const_spec((1, d)),
        _const_spec((d, D_IN)),
        _const_spec((V_W + U_W, d)),
        pl.BlockSpec((tm, DK), lambda i, j: (j, 0)),
        pl.BlockSpec((tm, DK), lambda i, j: (j, 0)),
        _const_spec((HEADS, CHUNK, CHUNK)),
        _const_spec((HEADS, CHUNK, DV)),
        _const_spec((HEADS, CHUNK, DK)),
        _const_spec((1, V_W)),
        _const_spec((1, U_W)),
        _const_spec((1, U_W)),
        _const_spec((GROUPS, CHUNK, CHUNK)),
        _const_spec((GROUPS, CHUNK, GD)),
        pl.BlockSpec((1, BB_RET, HEAD_PAIR * DK), pair_blk),
        pl.BlockSpec((1, BB_RET, HEAD_PAIR * DK), pair_blk),
        pl.BlockSpec((1, BB_RET, HEAD_PAIR * DV), pair_blk),
        pl.BlockSpec((1, BB_RET, HEAD_PAIR, DK, DV), state_blk),
    ]
    out_specs = [
        pl.BlockSpec((1, tm, d), lambda i, j: (i, j, 0)),
        pl.BlockSpec((1, 1, HEADS, DK, DV), lambda i, j: (0, i, 0, 0, 0)),
        pl.BlockSpec((1, BB_RET, HEAD_PAIR, DK, DV), state_blk),
        pl.BlockSpec((1, BB_RET, HEAD_PAIR * DV), pair_blk),
    ]
    out_shape = [
        jax.ShapeDtypeStruct((b, t, d), F32),
        jax.ShapeDtypeStruct((1, b, HEADS, DK, DV), F32),
        jax.ShapeDtypeStruct(sstate.shape, F32),
        jax.ShapeDtypeStruct((n_hp, ms, HEAD_PAIR * DV), F32),
    ]
    scratch = [
        pltpu.VMEM((tm, Q_W), BF16),
        pltpu.VMEM((tm, Q_W), BF16),
        pltpu.VMEM((tm, Q_W), BF16),
        pltpu.VMEM((tm, V_W), BF16),
        pltpu.VMEM((tm, V_W), F32),
        pltpu.VMEM((tm, U_W), F32),
        pltpu.VMEM((tm, U_W), BF16),
        pltpu.VMEM((tm, V_W + U_W), BF16),
        pltpu.VMEM((n_pairs, CHUNK, CHUNK), BF16),
        pltpu.VMEM((n_pairs, DK, DV), F32),
        pltpu.VMEM((n_pairs, DK, DV), BF16),
    ]
    x1, state_p, state_s, o2 = pl.pallas_call(
        _mix_prompt_kernel, grid=grid, in_specs=in_specs, out_specs=out_specs,
        out_shape=out_shape, scratch_shapes=scratch,
        compiler_params=_params("arbitrary", "arbitrary"), name="mix_prompt",
    )(cd, decay1, x, gmix, win, wout, cos2, sin2, dmask, xi_b, zeta_b, gn, lng, lnb, wsgu, bsgu_b,
      qs2, ks2, vs2, sstate)
    return x1, state_p, state_s, o2


def _ffn_tile(x, g, gfin, w1_s, w2_s, final, before_slice=None):
    h = _rms(x, g).astype(BF16)
    acc = x
    for j in range(FFN_SLICES):
        if before_slice is not None:
            before_slice(j)
        js = slice(j * FFN_HS, (j + 1) * FFN_HS)
        f = _dot(h, w1_s[:, js])
        a = jnp.square(jnp.maximum(f, 0.0)).astype(BF16)
        acc = acc + _dot(a, w2_s[js, :])
    if final:
        acc = _rms(acc, gfin)
    return acc


def _ffn_kernel(*refs, layer, final, n_tiles, n_casts):
    xp_ref, xs_ref, g_ref, gfin_ref, w1_hbm, w2_hbm = refs[:6]
    cast_in = refs[6:6 + n_casts]
    yp_ref, ys_ref = refs[6 + n_casts:8 + n_casts]
    cast_out = refs[8 + n_casts:8 + 2 * n_casts]
    w1_s, w2_s, st1, st2, sem = refs[8 + 2 * n_casts:]
    i = pl.program_id(0)

    if n_casts:
        @pl.when(jnp.logical_and(i >= 1, i <= CAST_STEPS))
        def _():
            for src, dst in zip(cast_in, cast_out):
                dst[...] = src[...].astype(BF16)

    def slice_copies(j):
        slot = j % 2
        js = pl.ds(j * FFN_HS, FFN_HS)
        return (pltpu.make_async_copy(w1_hbm.at[layer, :, js], st1.at[slot], sem.at[0, slot]),
                pltpu.make_async_copy(w2_hbm.at[layer, js, :], st2.at[slot], sem.at[1, slot]))

    def fetch_slice(j):
        js = slice(j * FFN_HS, (j + 1) * FFN_HS)
        for c in slice_copies(j):
            c.wait()
        w1_s[:, js] = st1[j % 2].astype(BF16)
        w2_s[js, :] = st2[j % 2].astype(BF16)
        if j + 2 < FFN_SLICES:
            for c in slice_copies(j + 2):
                c.start()

    @pl.when(i == 0)
    def _():
        for j in range(min(2, FFN_SLICES)):
            for c in slice_copies(j):
                c.start()
        yp_ref[...] = _ffn_tile(xp_ref[...], g_ref[...], gfin_ref[...], w1_s, w2_s, final, fetch_slice)

    @pl.when(jnp.logical_and(i > 0, i < n_tiles))
    def _():
        yp_ref[...] = _ffn_tile(xp_ref[...], g_ref[...], gfin_ref[...], w1_s, w2_s, final)

    @pl.when(i == n_tiles)
    def _():
        ys_ref[...] = _ffn_tile(xs_ref[...], g_ref[...], gfin_ref[...], w1_s, w2_s, final)


def _ffn(xp, xs, g, w1, w2, gfin, layer, final, casts=()):
    m, d = xp.shape
    ms = xs.shape[0]
    tm = TM_FFN
    n_tiles = m // tm
    assert n_tiles > CAST_STEPS
    tile = lambda i: (jnp.minimum(i, n_tiles - 1), 0)
    cast_blk = lambda i: (jnp.clip(i - 1, 0, CAST_STEPS - 1), 0)
    cast_specs = [pl.BlockSpec((c.shape[0] // CAST_STEPS, c.shape[1]), cast_blk) for c in casts]
    return pl.pallas_call(
        functools.partial(_ffn_kernel, layer=layer, final=final, n_tiles=n_tiles, n_casts=len(casts)),
        grid=(n_tiles + 1,),
        in_specs=[pl.BlockSpec((tm, d), tile), _const_spec((ms, d)),
                  _const_spec((1, d)), _const_spec((1, d)),
                  pl.BlockSpec(memory_space=pl.ANY), pl.BlockSpec(memory_space=pl.ANY)] + cast_specs,
        out_specs=[pl.BlockSpec((tm, d), tile), pl.BlockSpec((ms, d), lambda i: (0, 0))] + cast_specs,
        out_shape=[jax.ShapeDtypeStruct((m, d), F32), jax.ShapeDtypeStruct((ms, d), F32)]
        + [jax.ShapeDtypeStruct(c.shape, BF16) for c in casts],
        scratch_shapes=[pltpu.VMEM((d, D_FF), BF16), pltpu.VMEM((D_FF, d), BF16),
                        pltpu.VMEM((2, d, FFN_HS), F32), pltpu.VMEM((2, FFN_HS, d), F32),
                        pltpu.SemaphoreType.DMA((2, 2))],
        compiler_params=_params("arbitrary"), name="ffn_final" if final else "ffn",
    )(xp, xs, g, gfin, w1, w2, *casts)


def _glu_pieces(x, g, win_ref, bin_ref):
    h = _rms(x, g).astype(BF16)
    for c in range(D_MODEL // GD):
        ca = slice(c * GD, (c + 1) * GD)
        cg = slice(D_MODEL + c * GD, D_MODEL + (c + 1) * GD)
        a = _dot(h, win_ref[:, ca]) + bin_ref[:, ca]
        gt = _dot(h, win_ref[:, cg]) + bin_ref[:, cg]
        yield c, a * _sigmoid(gt)


def _glu_in(x, g, win_ref, bin_ref):
    return jnp.concatenate([p for _, p in _glu_pieces(x, g, win_ref, bin_ref)], axis=-1)


def _from_token_rows(buf, first_token, n):
    return jnp.concatenate(
        [buf[pl.ds(first_token * LANE_TILES + g, n, stride=LANE_TILES), :] for g in range(LANE_TILES)], axis=-1)


def _conv_block(r, full_s, conv_s, w3_ref, b3_ref):
    off0 = HIST_PAD - HIST
    acc = jnp.broadcast_to(b3_ref[...][None], (CONV_RB, LANE_TILES, 128))
    for j in range(CONV_WIDTH):
        start = pl.multiple_of((r * CONV_RB + j + off0) * LANE_TILES, LANE_TILES)
        xj = full_s[pl.ds(start, CONV_RB * LANE_TILES), :].reshape(CONV_RB, LANE_TILES, 128)
        acc = acc + xj * w3_ref[j][None]
    out0 = pl.multiple_of(r * CONV_RB * LANE_TILES, CONV_RB * LANE_TILES)
    conv_s[pl.ds(out0, CONV_RB * LANE_TILES), :] = acc.reshape(CONV_RB * LANE_TILES, 128)


def _conv_prompt_kernel(x_ref, xprev_ref, gmix_ref, win_ref, bin_ref, dww_ref, dwb_ref, w3_ref, b3_ref,
                        lng_ref, lnb_ref, wout_ref, bout_ref, shist_ref, sxg_ref,
                        o_ref, hist_ref, snewhist_ref, sconv_ref,
                        full_s, conv_s, *, n_tiles, tiles_per_seq, sample_every):
    s = pl.program_id(0)
    tm = x_ref.shape[0]
    off0 = HIST_PAD - HIST
    hist_rows = HIST_PAD * LANE_TILES

    @pl.when(jnp.logical_and(s % sample_every == 0, s < n_tiles))
    def _():
        _conv_sample_state_update(shist_ref, sxg_ref, dww_ref, dwb_ref, snewhist_ref, sconv_ref)

    def front():
        for c, piece in _glu_pieces(x_ref[...], gmix_ref[...], win_ref, bin_ref):
            for e in range(GD // 128):
                g = c * (GD // 128) + e
                full_s[pl.ds(hist_rows + g, tm, stride=LANE_TILES), :] = piece[:, e * 128:(e + 1) * 128]

    def back():
        conv = _from_token_rows(conv_s, 0, tm)
        c = _silu(_layernorm(conv, lng_ref[...], lnb_ref[...])).astype(BF16)
        o_ref[...] = xprev_ref[...] + _dot(c, wout_ref[...]) + bout_ref[...]

    @pl.when(s % tiles_per_seq == 0)
    def _():
        full_s[0:hist_rows, :] = jnp.zeros((hist_rows, 128), F32)

    @pl.when(s == 0)
    def _():
        front()

    @pl.when(jnp.logical_and(s > 0, s < n_tiles))
    def _():
        front()
        back()

    @pl.when(s == n_tiles)
    def _():
        back()

    @pl.when(s < n_tiles)
    def _():
        def body(r, carry):
            _conv_block(r, full_s, conv_s, w3_ref, b3_ref)
            return carry
        lax.fori_loop(0, tm // CONV_RB, body, 0)

        @pl.when(s % tiles_per_seq == tiles_per_seq - 1)
        def _():
            hist_ref[0, 0] = _from_token_rows(full_s, tm + off0, HIST)

        full_s[0:hist_rows, :] = full_s[tm * LANE_TILES:tm * LANE_TILES + hist_rows, :]


def _conv_prompt(x, gmix, win, bin_, dww, dwb, lng, lnb, wout, bout, shist_t, sxg):
    b, t, d = x.shape
    tm = TM_CONV
    n_tiles = (b * t) // tm
    tiles_per_seq = t // tm
    ms = sxg.shape[0]
    n_sblocks = ms // BB_CONV
    sample_every = n_tiles // n_sblocks
    assert sample_every * n_sblocks == n_tiles
    sblock = lambda s: jnp.minimum(s // sample_every, n_sblocks - 1)
    in_specs = [
        pl.BlockSpec((tm, d), lambda s: (jnp.minimum(s, n_tiles - 1), 0)),
        pl.BlockSpec((tm, d), lambda s: (jnp.maximum(s - 1, 0), 0)),
        _const_spec((1, d)), _const_spec((d, 2 * d)), _const_spec((1, 2 * d)),
        _const_spec((CONV_WIDTH, d)), _const_spec((1, d)),
        _const_spec((CONV_WIDTH, LANE_TILES, 128)), _const_spec((LANE_TILES, 128)),
        _const_spec((1, d)), _const_spec((1, d)),
        _const_spec((d, d)), _const_spec((1, d)),
        pl.BlockSpec((HIST, BB_CONV, d), lambda s: (0, sblock(s), 0)),
        pl.BlockSpec((BB_CONV, d), lambda s: (sblock(s), 0)),
    ]
    out_specs = [
        pl.BlockSpec((tm, d), lambda s: (jnp.maximum(s - 1, 0), 0)),
        pl.BlockSpec((1, 1, HIST, d), lambda s: (0, jnp.minimum(s, n_tiles - 1) // tiles_per_seq, 0, 0)),
        pl.BlockSpec((HIST, BB_CONV, d), lambda s: (0, sblock(s), 0)),
        pl.BlockSpec((BB_CONV, d), lambda s: (sblock(s), 0)),
    ]
    out_shape = [jax.ShapeDtypeStruct((b * t, d), F32),
                 jax.ShapeDtypeStruct((1, b, HIST, d), F32),
                 jax.ShapeDtypeStruct(shist_t.shape, F32),
                 jax.ShapeDtypeStruct((ms, d), F32)]
    scratch = [
        pltpu.VMEM(((tm + HIST_PAD) * LANE_TILES, 128), F32),
        pltpu.VMEM((tm * LANE_TILES, 128), F32),
    ]
    x2d = x.reshape(b * t, d)
    w3 = dww.reshape(CONV_WIDTH, LANE_TILES, 128)
    b3 = dwb.reshape(LANE_TILES, 128)
    return pl.pallas_call(
        functools.partial(_conv_prompt_kernel, n_tiles=n_tiles, tiles_per_seq=tiles_per_seq,
                          sample_every=sample_every),
        grid=(n_tiles + 1,), in_specs=in_specs, out_specs=out_specs,
        out_shape=out_shape, scratch_shapes=scratch,
        compiler_params=_params("arbitrary"), name="conv_prompt",
    )(x2d, x2d, gmix, win, bin_, dww, dwb, w3, b3, lng, lnb, wout, bout, shist_t, sxg)


def _mix_sample_in_kernel(sgu_ref, x_ref, gmix_ref, win_ref, cos_ref, sin_ref, lng_ref, lnb_ref,
                          q_ref, k_ref, v_ref, gate_ref, sgu_out_ref, rows_ref):
    h = _rms(x_ref[...], gmix_ref[...]).astype(BF16)
    qk = _dot(h, win_ref[:, 0:OFF_V])
    cos2 = cos_ref[...]
    sin2 = sin_ref[...]
    for j in range(HEADS):
        p, li = divmod(j, HEAD_PAIR)
        q_ref[p, :, li * DK:(li + 1) * DK] = _rotary(qk[:, j * DK:(j + 1) * DK], cos2, sin2)
        kj = qk[:, Q_W + j * DK:Q_W + (j + 1) * DK]
        k_ref[p, :, li * DK:(li + 1) * DK] = _rotary(kj, cos2, sin2) * (DK ** -0.5)
    v = _dot(h, win_ref[:, OFF_V:OFF_G])
    for p in range(HEADS // HEAD_PAIR):
        v_ref[p] = v[:, p * HEAD_PAIR * DV:(p + 1) * HEAD_PAIR * DV]
    gate_ref[...] = _silu(_dot(h, win_ref[:, OFF_G:OFF_U]))
    u = _gelu(_dot(h, win_ref[:, OFF_U:OFF_S]))
    s = _gelu(_dot(h, win_ref[:, OFF_S:D_IN]))
    for g in range(GROUPS):
        gs = slice(g * GD, (g + 1) * GD)
        sn = _layernorm(s[:, gs], lng_ref[:, gs], lnb_ref[:, gs])
        rows_ref[:, gs] = sn
        sgu_out_ref[:, gs] = u[:, gs] * (sgu_ref[0, g] * sn + sgu_ref[1, g])


def _mix_sample_in(x, gmix, win, cos2, sin2, lng, lnb, sgu_wb):
    m, d = x.shape
    shp = lambda n: jax.ShapeDtypeStruct((m, n), F32)
    n_hp = HEADS // HEAD_PAIR
    paired = lambda w: jax.ShapeDtypeStruct((n_hp, m, HEAD_PAIR * w), F32)
    vm = lambda: pl.BlockSpec(memory_space=pltpu.VMEM)
    return pl.pallas_call(
        _mix_sample_in_kernel,
        in_specs=[_smem_spec()] + [vm() for _ in range(7)],
        out_specs=[vm() for _ in range(6)],
        out_shape=[paired(DK), paired(DK), paired(DV), shp(V_W), shp(U_W), shp(U_W)],
        compiler_params=pltpu.CompilerParams(vmem_limit_bytes=VMEM_LIMIT), name="mix_sample_in",
    )(sgu_wb, x, gmix, win, cos2, sin2, lng, lnb)


def _mix_sample_out_kernel(x_ref, o_ref, gate_ref, sgu_ref, gn_ref, wout_ref, x1_ref):
    parts = []
    for hh in range(HEADS):
        p, li = divmod(hh, HEAD_PAIR)
        vs = slice(hh * DV, (hh + 1) * DV)
        o = o_ref[p, :, li * DV:(li + 1) * DV]
        parts.append((gate_ref[:, vs] * _groupnorm(o, gn_ref[:, vs])).astype(BF16))
    y = _dot(jnp.concatenate(parts, axis=-1), wout_ref[0:V_W, :])
    y = y + _dot(sgu_ref[...].astype(BF16), wout_ref[V_W:V_W + U_W, :])
    x1_ref[...] = x_ref[...] + y


def _mix_sample_out(x, o, gate, sgu_out, gn, wout):
    m, d = x.shape
    vm = lambda: pl.BlockSpec(memory_space=pltpu.VMEM)
    return pl.pallas_call(
        _mix_sample_out_kernel, in_specs=[vm() for _ in range(6)], out_specs=vm(),
        out_shape=jax.ShapeDtypeStruct((m, d), F32),
        compiler_params=pltpu.CompilerParams(vmem_limit_bytes=VMEM_LIMIT), name="mix_sample_out",
    )(x, o, gate, sgu_out, gn, wout)


def _conv_sample_in_kernel(x_ref, g_ref, win_ref, bin_ref, xg_ref):
    xg_ref[...] = _glu_in(x_ref[...], g_ref[...], win_ref, bin_ref)


def _conv_sample_in(x, g, win, bin_):
    m, d = x.shape
    vm = lambda: pl.BlockSpec(memory_space=pltpu.VMEM)
    return pl.pallas_call(
        _conv_sample_in_kernel, in_specs=[vm() for _ in range(4)], out_specs=vm(),
        out_shape=jax.ShapeDtypeStruct((m, d), F32),
        compiler_params=pltpu.CompilerParams(vmem_limit_bytes=VMEM_LIMIT), name="conv_sample_in",
    )(x, g, win, bin_)


def _conv_sample_state_update(hist_ref, xg_ref, dww_ref, dwb_ref, newhist_ref, conv_ref):
    xg = xg_ref[...]
    acc = dwb_ref[...] + xg * dww_ref[HIST:CONV_WIDTH, :]
    for j in range(HIST):
        acc = acc + hist_ref[j] * dww_ref[j:j + 1, :]
    conv_ref[...] = acc
    for j in range(HIST - 1):
        newhist_ref[j] = hist_ref[j + 1]
    newhist_ref[HIST - 1] = xg


def _conv_sample_out_kernel(x_ref, conv_ref, lng_ref, lnb_ref, wout_ref, bout_ref, o_ref):
    c = _silu(_layernorm(conv_ref[...], lng_ref[...], lnb_ref[...])).astype(BF16)
    o_ref[...] = x_ref[...] + _dot(c, wout_ref[...]) + bout_ref[...]


def _conv_sample_out(x, conv, lng, lnb, wout, bout):
    m, d = x.shape
    vm = lambda: pl.BlockSpec(memory_space=pltpu.VMEM)
    return pl.pallas_call(
        _conv_sample_out_kernel, in_specs=[vm() for _ in range(6)], out_specs=vm(),
        out_shape=jax.ShapeDtypeStruct((m, d), F32),
        compiler_params=pltpu.CompilerParams(vmem_limit_bytes=VMEM_LIMIT), name="conv_sample_out",
    )(x, conv, lng, lnb, wout, bout)


def _rope_tables(pos):
    half = DK // 2
    inv = ROPE_BASE ** (-jnp.arange(half, dtype=F32) / half)
    ang = pos.astype(F32)[:, None] * inv[None, :]
    cos = jnp.cos(ang)
    sin = jnp.sin(ang)
    return jnp.concatenate([cos, cos], axis=-1), jnp.concatenate([-sin, sin], axis=-1)


def _decay_tables(c):
    log_g = jnp.log1p(-(2.0 ** (-5.0 - jnp.arange(HEADS, dtype=F32))))
    idx = jnp.arange(c, dtype=F32)
    diff = idx[:, None] - idx[None, :]
    dmask = jnp.where(diff >= 0, jnp.exp(jnp.maximum(diff, 0.0)[None] * log_g[:, None, None]), 0.0)
    xi = jnp.exp((idx[None, :] + 1.0) * log_g[:, None])
    zeta = jnp.exp((c - 1.0 - idx[None, :]) * log_g[:, None])
    chunk_decay = jnp.exp(c * log_g)
    return dmask, xi, zeta, chunk_decay


def kernel(x_prompt, x_sample, state_ret, state_conv, norm_mix_g, norm_ffn_g, final_norm_g, ab_w_in, ab_w_out, ret_gn_g, sgu_ln_g, sgu_ln_b, sgu_w, sgu_b, conv_w_in, conv_b_in, conv_dw_w, conv_dw_b, conv_ln_g, conv_ln_b, conv_w_out, conv_b_out, ffn_w1, ffn_w2):
    b, t, d = x_prompt.shape
    m_s = x_sample.shape[0]
    row = lambda a: a.reshape(1, -1)

    win = ab_w_in[0].astype(BF16)
    wout = ab_w_out[0].astype(BF16)
    gmix = [row(norm_mix_g[l]) for l in range(2)]
    gffn = [row(norm_ffn_g[l]) for l in range(2)]
    gfin = row(final_norm_g)
    gn, lng, lnb = row(ret_gn_g[0]), row(sgu_ln_g[0]), row(sgu_ln_b[0])
    cbin, cdwb, clng, clnb, cbout = (row(conv_b_in[0]), row(conv_dw_b[0]), row(conv_ln_g[0]),
                                     row(conv_ln_b[0]), row(conv_b_out[0]))

    cos_p, sin_p = _rope_tables(jnp.arange(t))
    dmask, xi, zeta, cd = _decay_tables(CHUNK)
    xi_b = jnp.broadcast_to(xi[:, :, None], (HEADS, CHUNK, DV))
    zeta_b = jnp.broadcast_to(zeta[:, :, None], (HEADS, CHUNK, DK))
    bsgu_b = jnp.broadcast_to(sgu_b[0][:, :CHUNK, None], (GROUPS, CHUNK, GD))

    xs = x_sample.reshape(m_s, d)
    cos_s, sin_s = _rope_tables(PAST_LEN + jnp.arange(1))
    _, _, _, decay1 = _decay_tables(1)
    sgu_wb = jnp.stack([sgu_w[0][:, 0, 0], sgu_b[0][:, 0]])
    q, k, v, gate, sgu_out, rows = _mix_sample_in(xs, gmix[0], win, cos_s, sin_s, lng, lnb, sgu_wb)
    x1, ret_state_p, ret_state_s, o = _mix_prompt(x_prompt, gmix[0], win, wout, cos_p, sin_p, dmask, xi_b, zeta_b,
                                                  cd, gn, lng, lnb, sgu_w[0], bsgu_b, decay1, q, k, v, state_ret)
    xs1 = _mix_sample_out(xs, o, gate, sgu_out, gn, wout)

    x2, xs2, cwin, cwout = _ffn(x1.reshape(b * t, d), xs1, gffn[0], ffn_w1, ffn_w2, gfin, 0, False,
                                casts=(conv_w_in[0], conv_w_out[0]))

    xg = _conv_sample_in(xs2, gmix[1], cwin, cbin)
    hist_t = jnp.transpose(state_conv[0], (1, 0, 2))
    x3, conv_state_p, newhist_t, conv = _conv_prompt(x2.reshape(b, t, d), gmix[1], cwin, cbin, conv_dw_w[0], cdwb,
                                                     clng, clnb, cwout, cbout, hist_t, xg)
    conv_state_s = jnp.transpose(newhist_t, (1, 0, 2))[None]
    xs3 = _conv_sample_out(xs2, conv, clng, clnb, cwout, cbout)

    y_prompt, y_sample = _ffn(x3, xs3, gffn[1], ffn_w1, ffn_w2, gfin, 1, True)

    return (y_prompt.reshape(b, t, d), y_sample.reshape(m_s, 1, d), ret_state_p, ret_state_s,
            rows.reshape(1, m_s, 1, U_W), conv_state_p, conv_state_s)
```

```python
import functools

import jax
import jax.numpy as jnp
from jax import lax
from jax.experimental import pallas as pl
from jax.experimental.pallas import tpu as pltpu

F32 = jnp.float32
BF16 = jnp.bfloat16

D_MODEL = 1024
HEADS = 4
DK = 128
DV = 256
GROUPS = 4
GD = 256
CHUNK = 128
ROPE_BASE = 10000.0
CONV_WIDTH = 31
HIST = CONV_WIDTH - 1
D_FF = 4 * D_MODEL
RMS_EPS = 1e-6
LN_EPS = 1e-5
PAST_LEN = 16384

Q_W = HEADS * DK
V_W = HEADS * DV
U_W = GROUPS * GD
OFF_V = 2 * Q_W
OFF_G = OFF_V + V_W
OFF_U = OFF_G + V_W
OFF_S = OFF_U + U_W
D_IN = OFF_S + U_W

SUBLANES = 8
VMEM_LIMIT = 56 * 1024 * 1024

TM_MIX = 512
TM_FFN = 512
TM_CONV = 512
CONV_RB = 32
LANE_TILES = D_MODEL // 128
HIST_PAD = 32
FFN_SLICES = 4
FFN_HS = D_FF // FFN_SLICES
CAST_STEPS = 8
L1_SLICES = 8
L1_FS = D_FF // L1_SLICES
L1_CHUNKS = 16
L1_CH = D_FF // L1_CHUNKS
BB_RET = 8
HEAD_PAIR = 2
BB_CONV = 8


def _params(*sem):
    return pltpu.CompilerParams(dimension_semantics=sem, vmem_limit_bytes=VMEM_LIMIT)


def _const_spec(shape):
    n = len(shape)
    return pl.BlockSpec(shape, lambda *_: (0,) * n, pipeline_mode=pl.Buffered(1))


def _smem_spec():
    return pl.BlockSpec(memory_space=pltpu.SMEM)


def _rms(x, g):
    ms = jnp.mean(x * x, axis=-1, keepdims=True)
    return x * lax.rsqrt(ms + RMS_EPS) * g


def _layernorm(x, g, b):
    mu = jnp.mean(x, axis=-1, keepdims=True)
    xc = x - mu
    var = jnp.mean(xc * xc, axis=-1, keepdims=True)
    return xc * lax.rsqrt(var + LN_EPS) * g + b


def _groupnorm(x, g):
    mu = jnp.mean(x, axis=-1, keepdims=True)
    xc = x - mu
    var = jnp.mean(xc * xc, axis=-1, keepdims=True)
    return xc * lax.rsqrt(var + LN_EPS) * g


def _sigmoid(x):
    return 1.0 / (1.0 + jnp.exp(-x))


def _silu(x):
    return x * _sigmoid(x)


def _gelu(x):
    return 0.5 * x * (1.0 + lax.erf(x * 0.7071067811865476))


def _dot(a, b):
    return jnp.dot(a, b, preferred_element_type=F32)


def _dot_nt(a, b):
    return lax.dot_general(a, b, (((1,), (1,)), ((), ())), preferred_element_type=F32)


def _dot_tn(a, b):
    return lax.dot_general(a, b, (((0,), (0,)), ((), ())), preferred_element_type=F32)


def _rotary(blk, cos2, sin2):
    return blk * cos2 + pltpu.roll(blk, DK // 2, 1) * sin2


def _sample_state_update(step, decay_ref, qs_ref, ks_ref, vs_ref, sin_ref, sout_ref, os_ref):
    pair = step % (HEADS // HEAD_PAIR)
    for li in range(HEAD_PAIR):
        ks = slice(li * DK, (li + 1) * DK)
        vs = slice(li * DV, (li + 1) * DV)
        qt = qs_ref[0, :, ks].T
        kt = ks_ref[0, :, ks].T
        gam = decay_ref[pair * HEAD_PAIR + li]
        for b in range(BB_RET):
            st = sin_ref[0, b, li]
            sn = gam * st + kt[:, b:b + 1] * vs_ref[0, b:b + 1, vs]
            sout_ref[0, b, li] = sn
            os_ref[0, b:b + 1, vs] = jnp.sum(qt[:, b:b + 1] * sn, axis=0, keepdims=True)


def _mix_prompt_kernel(cd_ref, decay_ref, x_ref, gmix_ref, win_ref, wout_ref, cos_ref, sin_ref,
                       dmask_ref, xi_ref, zeta_ref, gn_ref, lng_ref, lnb_ref,
                       wsgu_ref, bsgu_ref, qs_ref, ks_ref, vs_ref, sstate_ref,
                       x1_ref, state_ref, sstate_out_ref, os_ref,
                       q_s, k_s, kz_s, v_s, gate_s, u_s, sn_s, mix_s, inner_s, kv_s, sb_s):
    t = pl.program_id(1)
    tm = x_ref.shape[1]

    @pl.when(t == 0)
    def _():
        state_ref[...] = jnp.zeros_like(state_ref)

    _sample_state_update(pl.program_id(0) * pl.num_programs(1) + t, decay_ref, qs_ref, ks_ref, vs_ref,
                         sstate_ref, sstate_out_ref, os_ref)

    x = x_ref[0]
    h = _rms(x, gmix_ref[...]).astype(BF16)

    cos2 = cos_ref[...]
    sin2 = sin_ref[...]
    n_chunks = tm // CHUNK

    def proj(off, g):
        return _dot(h, win_ref[:, off + g * GD:off + (g + 1) * GD])

    for pair in range(Q_W // GD):
        qq = proj(0, pair)
        kk = proj(Q_W, pair)
        for e in range(GD // DK):
            j = pair * (GD // DK) + e
            ks = slice(j * DK, (j + 1) * DK)
            q_s[:, ks] = _rotary(qq[:, e * DK:(e + 1) * DK], cos2, sin2).astype(BF16)
            kj = _rotary(kk[:, e * DK:(e + 1) * DK], cos2, sin2) * (DK ** -0.5)
            k_s[:, ks] = kj.astype(BF16)
            for c in range(n_chunks):
                rs = slice(c * CHUNK, (c + 1) * CHUNK)
                kz_s[rs, ks] = (kj[rs, :] * zeta_ref[j]).astype(BF16)
    for g in range(GROUPS):
        gs = slice(g * GD, (g + 1) * GD)
        v_s[:, gs] = proj(OFF_V, g).astype(BF16)
        gate_s[:, gs] = _silu(proj(OFF_G, g))
        u_s[:, gs] = _gelu(proj(OFF_U, g))
        sn_s[:, gs] = _layernorm(_gelu(proj(OFF_S, g)), lng_ref[:, gs], lnb_ref[:, gs]).astype(BF16)

    row = lax.broadcasted_iota(jnp.int32, (CHUNK, CHUNK), 0)
    col = lax.broadcasted_iota(jnp.int32, (CHUNK, CHUNK), 1)
    tril = row >= col

    pairs = [(c, hh) for c in range(n_chunks) for hh in range(HEADS)]
    for c, hh in pairs:
        rs = slice(c * CHUNK, (c + 1) * CHUNK)
        ks = slice(hh * DK, (hh + 1) * DK)
        inner = _dot_nt(q_s[rs, ks], k_s[rs, ks]) * dmask_ref[hh]
        inner_s[c * HEADS + hh] = inner.astype(BF16)
    for c, hh in pairs:
        rs = slice(c * CHUNK, (c + 1) * CHUNK)
        kv_s[c * HEADS + hh] = _dot_tn(kz_s[rs, hh * DK:(hh + 1) * DK], v_s[rs, hh * DV:(hh + 1) * DV])
    for hh in range(HEADS):
        st = state_ref[0, 0, hh]
        for c in range(n_chunks):
            sb_s[c * HEADS + hh] = st.astype(BF16)
            st = cd_ref[hh] * st + kv_s[c * HEADS + hh]
        state_ref[0, 0, hh] = st
    for c, hh in pairs:
        rs = slice(c * CHUNK, (c + 1) * CHUNK)
        vs = slice(hh * DV, (hh + 1) * DV)
        o = _dot(inner_s[c * HEADS + hh], v_s[rs, vs])
        o = o + _dot(q_s[rs, hh * DK:(hh + 1) * DK], sb_s[c * HEADS + hh]) * xi_ref[hh]
        on = _groupnorm(o, gn_ref[:, vs])
        mix_s[rs, vs] = (gate_s[rs, vs] * on).astype(BF16)

    for c in range(n_chunks):
        rs = slice(c * CHUNK, (c + 1) * CHUNK)
        for g in range(GROUPS):
            gs = slice(g * GD, (g + 1) * GD)
            wg = jnp.where(tril, wsgu_ref[g], 0.0).astype(BF16)
            mixed = _dot(wg, sn_s[rs, gs]) + bsgu_ref[g]
            mix_s[rs, V_W + g * GD:V_W + (g + 1) * GD] = (u_s[rs, gs] * mixed).astype(BF16)

    x1_ref[0] = x + _dot(mix_s[...], wout_ref[...])


def _mix_prompt(x, gmix, win, wout, cos2, sin2, dmask, xi_b, zeta_b, cd, gn, lng, lnb, wsgu, bsgu_b,
                decay1, qs2, ks2, vs2, sstate):
    b, t, d = x.shape
    tm = TM_MIX
    nt = t // tm
    n_pairs = (tm // CHUNK) * HEADS
    grid = (b, nt)
    n_hp, ms, _ = qs2.shape
    assert b * nt == (ms // BB_RET) * n_hp
    pair_blk = lambda i, j: ((i * nt + j) % n_hp, (i * nt + j) // n_hp, 0)
    state_blk = lambda i, j: (0, (i * nt + j) // n_hp, (i * nt + j) % n_hp, 0, 0)
    in_specs = [
        _smem_spec(),
        _smem_spec(),
        pl.BlockSpec((1, tm, d), lambda i, j: (i, j, 0)),
        _const_spec((1, d)),
        _const_spec((d, D_IN)),
        _const_spec((V_W + U_W, d)),
        pl.BlockSpec((tm, DK), lambda i, j: (j, 0)),
        pl.BlockSpec((tm, DK), lambda i, j: (j, 0)),
        _const_spec((HEADS, CHUNK, CHUNK)),
        _const_spec((HEADS, CHUNK, DV)),
        _const_spec((HEADS, CHUNK, DK)),
        _const_spec((1, V_W)),
        _const_spec((1, U_W)),
        _const_spec((1, U_W)),
        _const_spec((GROUPS, CHUNK, CHUNK)),
        _const_spec((GROUPS, CHUNK, GD)),
        pl.BlockSpec((1, BB_RET, HEAD_PAIR * DK), pair_blk),
        pl.BlockSpec((1, BB_RET, HEAD_PAIR * DK), pair_blk),
        pl.BlockSpec((1, BB_RET, HEAD_PAIR * DV), pair_blk),
        pl.BlockSpec((1, BB_RET, HEAD_PAIR, DK, DV), state_blk),
    ]
    out_specs = [
        pl.BlockSpec((1, tm, d), lambda i, j: (i, j, 0)),
        pl.BlockSpec((1, 1, HEADS, DK, DV), lambda i, j: (0, i, 0, 0, 0)),
        pl.BlockSpec((1, BB_RET, HEAD_PAIR, DK, DV), state_blk),
        pl.BlockSpec((1, BB_RET, HEAD_PAIR * DV), pair_blk),
    ]
    out_shape = [
        jax.ShapeDtypeStruct((b, t, d), F32),
        jax.ShapeDtypeStruct((1, b, HEADS, DK, DV), F32),
        jax.ShapeDtypeStruct(sstate.shape, F32),
        jax.ShapeDtypeStruct((n_hp, ms, HEAD_PAIR * DV), F32),
    ]
    scratch = [
        pltpu.VMEM((tm, Q_W), BF16),
        pltpu.VMEM((tm, Q_W), BF16),
        pltpu.VMEM((tm, Q_W), BF16),
        pltpu.VMEM((tm, V_W), BF16),
        pltpu.VMEM((tm, V_W), F32),
        pltpu.VMEM((tm, U_W), F32),
        pltpu.VMEM((tm, U_W), BF16),
        pltpu.VMEM((tm, V_W + U_W), BF16),
        pltpu.VMEM((n_pairs, CHUNK, CHUNK), BF16),
        pltpu.VMEM((n_pairs, DK, DV), F32),
        pltpu.VMEM((n_pairs, DK, DV), BF16),
    ]
    x1, state_p, state_s, o2 = pl.pallas_call(
        _mix_prompt_kernel, grid=grid, in_specs=in_specs, out_specs=out_specs,
        out_shape=out_shape, scratch_shapes=scratch,
        compiler_params=_params("arbitrary", "arbitrary"), name="mix_prompt",
    )(cd, decay1, x, gmix, win, wout, cos2, sin2, dmask, xi_b, zeta_b, gn, lng, lnb, wsgu, bsgu_b,
      qs2, ks2, vs2, sstate)
    return x1, state_p, state_s, o2


def _ffn_tile(x, g, gfin, w1_s, w2_s, final, before_slice=None):
    h = _rms(x, g).astype(BF16)
    acc = x
    for j in range(FFN_SLICES):
        if before_slice is not None:
            before_slice(j)
        js = slice(j * FFN_HS, (j + 1) * FFN_HS)
        f = _dot(h, w1_s[:, js])
        a = jnp.square(jnp.maximum(f, 0.0)).astype(BF16)
        acc = acc + _dot(a, w2_s[js, :])
    if final:
        acc = _rms(acc, gfin)
    return acc


def _ffn_kernel(*refs, layer, final, n_tiles, n_casts):
    xp_ref, xs_ref, g_ref, gfin_ref, w1_hbm, w2_hbm = refs[:6]
    cast_in = refs[6:6 + n_casts]
    yp_ref, ys_ref = refs[6 + n_casts:8 + n_casts]
    cast_out = refs[8 + n_casts:8 + 2 * n_casts]
    w1_s, w2_s, st1, st2, sem = refs[8 + 2 * n_casts:]
    i = pl.program_id(0)

    if n_casts:
        @pl.when(jnp.logical_and(i >= 1, i <= CAST_STEPS))
        def _():
            for src, dst in zip(cast_in, cast_out):
                dst[...] = src[...].astype(BF16)

    def slice_copies(j):
        slot = j % 2
        js = pl.ds(j * FFN_HS, FFN_HS)
        return (pltpu.make_async_copy(w1_hbm.at[layer, :, js], st1.at[slot], sem.at[0, slot]),
                pltpu.make_async_copy(w2_hbm.at[layer, js, :], st2.at[slot], sem.at[1, slot]))

    def fetch_slice(j):
        js = slice(j * FFN_HS, (j + 1) * FFN_HS)
        for c in slice_copies(j):
            c.wait()
        w1_s[:, js] = st1[j % 2].astype(BF16)
        w2_s[js, :] = st2[j % 2].astype(BF16)
        if j + 2 < FFN_SLICES:
            for c in slice_copies(j + 2):
                c.start()

    @pl.when(i == 0)
    def _():
        for j in range(min(2, FFN_SLICES)):
            for c in slice_copies(j):
                c.start()
        yp_ref[...] = _ffn_tile(xp_ref[...], g_ref[...], gfin_ref[...], w1_s, w2_s, final, fetch_slice)

    @pl.when(jnp.logical_and(i > 0, i < n_tiles))
    def _():
        yp_ref[...] = _ffn_tile(xp_ref[...], g_ref[...], gfin_ref[...], w1_s, w2_s, final)

    @pl.when(i == n_tiles)
    def _():
        ys_ref[...] = _ffn_tile(xs_ref[...], g_ref[...], gfin_ref[...], w1_s, w2_s, final)


def _ffn(xp, xs, g, w1, w2, gfin, layer, final, casts=()):
    m, d = xp.shape
    ms = xs.shape[0]
    tm = TM_FFN
    n_tiles = m // tm
    assert n_tiles > CAST_STEPS
    tile = lambda i: (jnp.minimum(i, n_tiles - 1), 0)
    cast_blk = lambda i: (jnp.clip(i - 1, 0, CAST_STEPS - 1), 0)
    cast_specs = [pl.BlockSpec((c.shape[0] // CAST_STEPS, c.shape[1]), cast_blk) for c in casts]
    return pl.pallas_call(
        functools.partial(_ffn_kernel, layer=layer, final=final, n_tiles=n_tiles, n_casts=len(casts)),
        grid=(n_tiles + 1,),
        in_specs=[pl.BlockSpec((tm, d), tile), _const_spec((ms, d)),
                  _const_spec((1, d)), _const_spec((1, d)),
                  pl.BlockSpec(memory_space=pl.ANY), pl.BlockSpec(memory_space=pl.ANY)] + cast_specs,
        out_specs=[pl.BlockSpec((tm, d), tile), pl.BlockSpec((ms, d), lambda i: (0, 0))] + cast_specs,
        out_shape=[jax.ShapeDtypeStruct((m, d), F32), jax.ShapeDtypeStruct((ms, d), F32)]
        + [jax.ShapeDtypeStruct(c.shape, BF16) for c in casts],
        scratch_shapes=[pltpu.VMEM((d, D_FF), BF16), pltpu.VMEM((D_FF, d), BF16),
                        pltpu.VMEM((2, d, FFN_HS), F32), pltpu.VMEM((2, FFN_HS, d), F32),
                        pltpu.SemaphoreType.DMA((2, 2))],
        compiler_params=_params("arbitrary"), name="ffn_final" if final else "ffn",
    )(xp, xs, g, gfin, w1, w2, *casts)


def _glu_pieces(x, g, win_ref, bin_ref):
    h = _rms(x, g).astype(BF16)
    for c in range(D_MODEL // GD):
        ca = slice(c * GD, (c + 1) * GD)
        cg = slice(D_MODEL + c * GD, D_MODEL + (c + 1) * GD)
        a = _dot(h, win_ref[:, ca]) + bin_ref[:, ca]
        gt = _dot(h, win_ref[:, cg]) + bin_ref[:, cg]
        yield c, a * _sigmoid(gt)


def _glu_in(x, g, win_ref, bin_ref):
    return jnp.concatenate([p for _, p in _glu_pieces(x, g, win_ref, bin_ref)], axis=-1)


def _from_token_rows(buf, first_token, n):
    return jnp.concatenate(
        [buf[pl.ds(first_token * LANE_TILES + g, n, stride=LANE_TILES), :] for g in range(LANE_TILES)], axis=-1)


def _conv_block(r, full_s, conv_s, w3_ref, b3_ref):
    off0 = HIST_PAD - HIST
    acc = jnp.broadcast_to(b3_ref[...][None], (CONV_RB, LANE_TILES, 128))
    for j in range(CONV_WIDTH):
        start = pl.multiple_of((r * CONV_RB + j + off0) * LANE_TILES, LANE_TILES)
        xj = full_s[pl.ds(start, CONV_RB * LANE_TILES), :].reshape(CONV_RB, LANE_TILES, 128)
        acc = acc + xj * w3_ref[j][None]
    out0 = pl.multiple_of(r * CONV_RB * LANE_TILES, CONV_RB * LANE_TILES)
    conv_s[pl.ds(out0, CONV_RB * LANE_TILES), :] = acc.reshape(CONV_RB * LANE_TILES, 128)


def _conv_prompt_kernel(x_ref, xprev_ref, gmix_ref, win_ref, bin_ref, dww_ref, dwb_ref, w3_ref, b3_ref,
                        lng_ref, lnb_ref, wout_ref, bout_ref, shist_ref, sxg_ref,
                        o_ref, hist_ref, snewhist_ref, sconv_ref,
                        full_s, conv_s, *, n_tiles, tiles_per_seq, sample_every):
    s = pl.program_id(0)
    tm = x_ref.shape[0]
    off0 = HIST_PAD - HIST
    hist_rows = HIST_PAD * LANE_TILES

    @pl.when(jnp.logical_and(s % sample_every == 0, s < n_tiles))
    def _():
        _conv_sample_state_update(shist_ref, sxg_ref, dww_ref, dwb_ref, snewhist_ref, sconv_ref)

    def front():
        for c, piece in _glu_pieces(x_ref[...], gmix_ref[...], win_ref, bin_ref):
            for e in range(GD // 128):
                g = c * (GD // 128) + e
                full_s[pl.ds(hist_rows + g, tm, stride=LANE_TILES), :] = piece[:, e * 128:(e + 1) * 128]

    def back():
        conv = _from_token_rows(conv_s, 0, tm)
        c = _silu(_layernorm(conv, lng_ref[...], lnb_ref[...])).astype(BF16)
        o_ref[...] = xprev_ref[...] + _dot(c, wout_ref[...]) + bout_ref[...]

    @pl.when(s % tiles_per_seq == 0)
    def _():
        full_s[0:hist_rows, :] = jnp.zeros((hist_rows, 128), F32)

    @pl.when(s == 0)
    def _():
        front()

    @pl.when(jnp.logical_and(s > 0, s < n_tiles))
    def _():
        front()
        back()

    @pl.when(s == n_tiles)
    def _():
        back()

    @pl.when(s < n_tiles)
    def _():
        def body(r, carry):
            _conv_block(r, full_s, conv_s, w3_ref, b3_ref)
            return carry
        lax.fori_loop(0, tm // CONV_RB, body, 0)

        @pl.when(s % tiles_per_seq == tiles_per_seq - 1)
        def _():
            hist_ref[0, 0] = _from_token_rows(full_s, tm + off0, HIST)

        full_s[0:hist_rows, :] = full_s[tm * LANE_TILES:tm * LANE_TILES + hist_rows, :]


def _conv_prompt(x, gmix, win, bin_, dww, dwb, lng, lnb, wout, bout, shist_t, sxg):
    b, t, d = x.shape
    tm = TM_CONV
    n_tiles = (b * t) // tm
    tiles_per_seq = t // tm
    ms = sxg.shape[0]
    n_sblocks = ms // BB_CONV
    sample_every = n_tiles // n_sblocks
    assert sample_every * n_sblocks == n_tiles
    sblock = lambda s: jnp.minimum(s // sample_every, n_sblocks - 1)
    in_specs = [
        pl.BlockSpec((tm, d), lambda s: (jnp.minimum(s, n_tiles - 1), 0)),
        pl.BlockSpec((tm, d), lambda s: (jnp.maximum(s - 1, 0), 0)),
        _const_spec((1, d)), _const_spec((d, 2 * d)), _const_spec((1, 2 * d)),
        _const_spec((CONV_WIDTH, d)), _const_spec((1, d)),
        _const_spec((CONV_WIDTH, LANE_TILES, 128)), _const_spec((LANE_TILES, 128)),
        _const_spec((1, d)), _const_spec((1, d)),
        _const_spec((d, d)), _const_spec((1, d)),
        pl.BlockSpec((HIST, BB_CONV, d), lambda s: (0, sblock(s), 0)),
        pl.BlockSpec((BB_CONV, d), lambda s: (sblock(s), 0)),
    ]
    out_specs = [
        pl.BlockSpec((tm, d), lambda s: (jnp.maximum(s - 1, 0), 0)),
        pl.BlockSpec((1, 1, HIST, d), lambda s: (0, jnp.minimum(s, n_tiles - 1) // tiles_per_seq, 0, 0)),
        pl.BlockSpec((HIST, BB_CONV, d), lambda s: (0, sblock(s), 0)),
        pl.BlockSpec((BB_CONV, d), lambda s: (sblock(s), 0)),
    ]
    out_shape = [jax.ShapeDtypeStruct((b * t, d), F32),
                 jax.ShapeDtypeStruct((1, b, HIST, d), F32),
                 jax.ShapeDtypeStruct(shist_t.shape, F32),
                 jax.ShapeDtypeStruct((ms, d), F32)]
    scratch = [
        pltpu.VMEM(((tm + HIST_PAD) * LANE_TILES, 128), F32),
        pltpu.VMEM((tm * LANE_TILES, 128), F32),
    ]
    x2d = x.reshape(b * t, d)
    w3 = dww.reshape(CONV_WIDTH, LANE_TILES, 128)
    b3 = dwb.reshape(LANE_TILES, 128)
    return pl.pallas_call(
        functools.partial(_conv_prompt_kernel, n_tiles=n_tiles, tiles_per_seq=tiles_per_seq,
                          sample_every=sample_every),
        grid=(n_tiles + 1,), in_specs=in_specs, out_specs=out_specs,
        out_shape=out_shape, scratch_shapes=scratch,
        compiler_params=_params("arbitrary"), name="conv_prompt",
    )(x2d, x2d, gmix, win, bin_, dww, dwb, w3, b3, lng, lnb, wout, bout, shist_t, sxg)


def _layer1_kernel(x_ref, xs_ref, gmix_ref, win_ref, bin_ref, dww_ref, dwb_ref, w3_ref, b3_ref, lng_ref, lnb_ref,
                   wout_ref, bout_ref, gffn_ref, gfin_ref, shist_ref, sxg_ref, w1_hbm, w2_hbm,
                   y_ref, ys_ref, hist_ref, snewhist_ref,
                   full_s, conv_s, xres_s, acc_s, hffn_s, sconv_s, w1_s, w2_s, st1, st2, sem,
                   *, layer, n_tiles, tiles_per_seq, sample_every):
    s = pl.program_id(0)
    tm = x_ref.shape[0]
    off0 = HIST_PAD - HIST
    hist_rows = HIST_PAD * LANE_TILES
    n_blocks = tm // CONV_RB
    blocks_per_slice = n_blocks // L1_SLICES
    chunks_per_slice = L1_CHUNKS // L1_SLICES

    @pl.when(jnp.logical_and(s % sample_every == 0, s < n_tiles))
    def _():
        rows = pl.ds(pl.multiple_of((s // sample_every) * BB_CONV, BB_CONV), BB_CONV)
        _conv_sample_state_update(shist_ref, sxg_ref, dww_ref, dwb_ref, snewhist_ref, sconv_s.at[rows])

    def front():
        x = x_ref[...]
        for c, piece in _glu_pieces(x, gmix_ref[...], win_ref, bin_ref):
            for e in range(GD // 128):
                g = c * (GD // 128) + e
                full_s[pl.ds(hist_rows + g, tm, stride=LANE_TILES), :] = piece[:, e * 128:(e + 1) * 128]
        return x

    def out_proj(conv, res):
        c = _silu(_layernorm(conv, lng_ref[...], lnb_ref[...])).astype(BF16)
        return res + _dot(c, wout_ref[...]) + bout_ref[...]

    def back():
        x3 = out_proj(_from_token_rows(conv_s, 0, tm), xres_s[...])
        acc_s[...] = x3
        hffn_s[...] = _rms(x3, gffn_ref[...]).astype(BF16)

    def chunk_copies(c):
        slot = c % 2
        cs = pl.ds(c * L1_CH, L1_CH)
        return (pltpu.make_async_copy(w1_hbm.at[layer, :, cs], st1.at[slot], sem.at[0, slot]),
                pltpu.make_async_copy(w2_hbm.at[layer, cs, :], st2.at[slot], sem.at[1, slot]))

    def fetch_chunk(c):
        j, e = divmod(c, chunks_per_slice)
        for cp in chunk_copies(c):
            cp.wait()
        w1_s[j, :, e * L1_CH:(e + 1) * L1_CH] = st1[c % 2].astype(BF16)
        w2_s[j, e * L1_CH:(e + 1) * L1_CH, :] = st2[c % 2].astype(BF16)
        if c + 2 < L1_CHUNKS:
            for cp in chunk_copies(c + 2):
                cp.start()

    def ffn_slice(i, h):
        f = _dot(h, w1_s[i])
        a = jnp.square(jnp.maximum(f, 0.0)).astype(BF16)
        return _dot(a, w2_s[i])

    def conv_ffn_loop(with_conv, with_ffn):
        def body(i, carry):
            if with_ffn:
                acc_s[...] += ffn_slice(i, hffn_s[...])
            if with_conv:
                for q in range(blocks_per_slice):
                    _conv_block(i * blocks_per_slice + q, full_s, conv_s, w3_ref, b3_ref)
            return carry
        lax.fori_loop(0, L1_SLICES, body, 0)

    @pl.when(s % tiles_per_seq == 0)
    def _():
        full_s[0:hist_rows, :] = jnp.zeros((hist_rows, 128), F32)

    @pl.when(s == 0)
    def _():
        for c in range(2):
            for cp in chunk_copies(c):
                cp.start()
        xres_s[...] = front()

        def body(r, carry):
            _conv_block(r, full_s, conv_s, w3_ref, b3_ref)
            return carry
        for c in range(L1_CHUNKS):
            lax.fori_loop(c * n_blocks // L1_CHUNKS, (c + 1) * n_blocks // L1_CHUNKS, body, 0)
            fetch_chunk(c)

    @pl.when(jnp.logical_and(s > 0, s < n_tiles))
    def _():
        back()
        xres_s[...] = front()
        conv_ffn_loop(True, True)

    @pl.when(s == n_tiles)
    def _():
        back()
        conv_ffn_loop(False, True)

    @pl.when(jnp.logical_and(s > 0, s <= n_tiles))
    def _():
        y_ref[...] = _rms(acc_s[...], gfin_ref[...])

    @pl.when(s < n_tiles)
    def _():
        @pl.when(s % tiles_per_seq == tiles_per_seq - 1)
        def _():
            hist_ref[0, 0] = _from_token_rows(full_s, tm + off0, HIST)

        full_s[0:hist_rows, :] = full_s[tm * LANE_TILES:tm * LANE_TILES + hist_rows, :]

    @pl.when(s == n_tiles + 1)
    def _():
        xs3 = out_proj(sconv_s[...], xs_ref[...])
        h = _rms(xs3, gffn_ref[...]).astype(BF16)
        acc = xs3
        for j in range(L1_SLICES):
            acc = acc + ffn_slice(j, h)
        ys_ref[...] = _rms(acc, gfin_ref[...])


def _layer1(x, xs, seq_len, gmix, win, bin_, dww, dwb, lng, lnb, wout, bout, gffn, gfin, shist_t, sxg, w1, w2, layer):
    m, d = x.shape
    ms = xs.shape[0]
    tm = TM_CONV
    n_tiles = m // tm
    tiles_per_seq = seq_len // tm
    n_sblocks = ms // BB_CONV
    sample_every = n_tiles // n_sblocks
    assert sample_every * n_sblocks == n_tiles and (tm // CONV_RB) % L1_SLICES == 0
    sblock = lambda s: jnp.minimum(s // sample_every, n_sblocks - 1)
    in_specs = [
        pl.BlockSpec((tm, d), lambda s: (jnp.minimum(s, n_tiles - 1), 0)), _const_spec((ms, d)),
        _const_spec((1, d)), _const_spec((d, 2 * d)), _const_spec((1, 2 * d)),
        _const_spec((CONV_WIDTH, d)), _const_spec((1, d)),
        _const_spec((CONV_WIDTH, LANE_TILES, 128)), _const_spec((LANE_TILES, 128)),
        _const_spec((1, d)), _const_spec((1, d)),
        _const_spec((d, d)), _const_spec((1, d)), _const_spec((1, d)), _const_spec((1, d)),
        pl.BlockSpec((HIST, BB_CONV, d), lambda s: (0, sblock(s), 0)),
        pl.BlockSpec((BB_CONV, d), lambda s: (sblock(s), 0)),
        pl.BlockSpec(memory_space=pl.ANY), pl.BlockSpec(memory_space=pl.ANY),
    ]
    out_specs = [
        pl.BlockSpec((tm, d), lambda s: (jnp.clip(s - 1, 0, n_tiles - 1), 0)),
        pl.BlockSpec((ms, d), lambda s: (0, 0)),
        pl.BlockSpec((1, 1, HIST, d), lambda s: (0, jnp.minimum(s, n_tiles - 1) // tiles_per_seq, 0, 0)),
        pl.BlockSpec((HIST, BB_CONV, d), lambda s: (0, sblock(s), 0)),
    ]
    out_shape = [jax.ShapeDtypeStruct((m, d), F32), jax.ShapeDtypeStruct((ms, d), F32),
                 jax.ShapeDtypeStruct((1, m // seq_len, HIST, d), F32),
                 jax.ShapeDtypeStruct(shist_t.shape, F32)]
    scratch = [
        pltpu.VMEM(((tm + HIST_PAD) * LANE_TILES, 128), F32),
        pltpu.VMEM((tm * LANE_TILES, 128), F32),
        pltpu.VMEM((tm, d), F32),
        pltpu.VMEM((tm, d), F32),
        pltpu.VMEM((tm, d), BF16),
        pltpu.VMEM((ms, d), F32),
        pltpu.VMEM((L1_SLICES, d, L1_FS), BF16),
        pltpu.VMEM((L1_SLICES, L1_FS, d), BF16),
        pltpu.VMEM((2, d, L1_CH), F32), pltpu.VMEM((2, L1_CH, d), F32),
        pltpu.SemaphoreType.DMA((2, 2)),
    ]
    w3 = dww.reshape(CONV_WIDTH, LANE_TILES, 128)
    b3 = dwb.reshape(LANE_TILES, 128)
    return pl.pallas_call(
        functools.partial(_layer1_kernel, layer=layer, n_tiles=n_tiles, tiles_per_seq=tiles_per_seq,
                          sample_every=sample_every),
        grid=(n_tiles + 2,), in_specs=in_specs, out_specs=out_specs,
        out_shape=out_shape, scratch_shapes=scratch,
        compiler_params=_params("arbitrary"), name="layer1",
    )(x, xs, gmix, win, bin_, dww, dwb, w3, b3, lng, lnb, wout, bout, gffn, gfin, shist_t, sxg, w1, w2)


def _mix_sample_in_kernel(sgu_ref, x_ref, gmix_ref, win_ref, cos_ref, sin_ref, lng_ref, lnb_ref,
                          q_ref, k_ref, v_ref, gate_ref, sgu_out_ref, rows_ref):
    h = _rms(x_ref[...], gmix_ref[...]).astype(BF16)
    qk = _dot(h, win_ref[:, 0:OFF_V])
    cos2 = cos_ref[...]
    sin2 = sin_ref[...]
    for j in range(HEADS):
        p, li = divmod(j, HEAD_PAIR)
        q_ref[p, :, li * DK:(li + 1) * DK] = _rotary(qk[:, j * DK:(j + 1) * DK], cos2, sin2)
        kj = qk[:, Q_W + j * DK:Q_W + (j + 1) * DK]
        k_ref[p, :, li * DK:(li + 1) * DK] = _rotary(kj, cos2, sin2) * (DK ** -0.5)
    v = _dot(h, win_ref[:, OFF_V:OFF_G])
    for p in range(HEADS // HEAD_PAIR):
        v_ref[p] = v[:, p * HEAD_PAIR * DV:(p + 1) * HEAD_PAIR * DV]
    gate_ref[...] = _silu(_dot(h, win_ref[:, OFF_G:OFF_U]))
    u = _gelu(_dot(h, win_ref[:, OFF_U:OFF_S]))
    s = _gelu(_dot(h, win_ref[:, OFF_S:D_IN]))
    for g in range(GROUPS):
        gs = slice(g * GD, (g + 1) * GD)
        sn = _layernorm(s[:, gs], lng_ref[:, gs], lnb_ref[:, gs])
        rows_ref[:, gs] = sn
        sgu_out_ref[:, gs] = u[:, gs] * (sgu_ref[0, g] * sn + sgu_ref[1, g])


def _mix_sample_in(x, gmix, win, cos2, sin2, lng, lnb, sgu_wb):
    m, d = x.shape
    shp = lambda n: jax.ShapeDtypeStruct((m, n), F32)
    n_hp = HEADS // HEAD_PAIR
    paired = lambda w: jax.ShapeDtypeStruct((n_hp, m, HEAD_PAIR * w), F32)
    vm = lambda: pl.BlockSpec(memory_space=pltpu.VMEM)
    return pl.pallas_call(
        _mix_sample_in_kernel,
        in_specs=[_smem_spec()] + [vm() for _ in range(7)],
        out_specs=[vm() for _ in range(6)],
        out_shape=[paired(DK), paired(DK), paired(DV), shp(V_W), shp(U_W), shp(U_W)],
        compiler_params=pltpu.CompilerParams(vmem_limit_bytes=VMEM_LIMIT), name="mix_sample_in",
    )(sgu_wb, x, gmix, win, cos2, sin2, lng, lnb)


def _mix_sample_out_kernel(x_ref, o_ref, gate_ref, sgu_ref, gn_ref, wout_ref, x1_ref):
    parts = []
    for hh in range(HEADS):
        p, li = divmod(hh, HEAD_PAIR)
        vs = slice(hh * DV, (hh + 1) * DV)
        o = o_ref[p, :, li * DV:(li + 1) * DV]
        parts.append((gate_ref[:, vs] * _groupnorm(o, gn_ref[:, vs])).astype(BF16))
    y = _dot(jnp.concatenate(parts, axis=-1), wout_ref[0:V_W, :])
    y = y + _dot(sgu_ref[...].astype(BF16), wout_ref[V_W:V_W + U_W, :])
    x1_ref[...] = x_ref[...] + y


def _mix_sample_out(x, o, gate, sgu_out, gn, wout):
    m, d = x.shape
    vm = lambda: pl.BlockSpec(memory_space=pltpu.VMEM)
    return pl.pallas_call(
        _mix_sample_out_kernel, in_specs=[vm() for _ in range(6)], out_specs=vm(),
        out_shape=jax.ShapeDtypeStruct((m, d), F32),
        compiler_params=pltpu.CompilerParams(vmem_limit_bytes=VMEM_LIMIT), name="mix_sample_out",
    )(x, o, gate, sgu_out, gn, wout)


def _conv_sample_in_kernel(x_ref, g_ref, win_ref, bin_ref, xg_ref):
    xg_ref[...] = _glu_in(x_ref[...], g_ref[...], win_ref, bin_ref)


def _conv_sample_in(x, g, win, bin_):
    m, d = x.shape
    vm = lambda: pl.BlockSpec(memory_space=pltpu.VMEM)
    return pl.pallas_call(
        _conv_sample_in_kernel, in_specs=[vm() for _ in range(4)], out_specs=vm(),
        out_shape=jax.ShapeDtypeStruct((m, d), F32),
        compiler_params=pltpu.CompilerParams(vmem_limit_bytes=VMEM_LIMIT), name="conv_sample_in",
    )(x, g, win, bin_)


def _conv_sample_state_update(hist_ref, xg_ref, dww_ref, dwb_ref, newhist_ref, conv_ref):
    xg = xg_ref[...]
    acc = dwb_ref[...] + xg * dww_ref[HIST:CONV_WIDTH, :]
    for j in range(HIST):
        acc = acc + hist_ref[j] * dww_ref[j:j + 1, :]
    conv_ref[...] = acc
    for j in range(HIST - 1):
        newhist_ref[j] = hist_ref[j + 1]
    newhist_ref[HIST - 1] = xg


def _conv_sample_out_kernel(x_ref, conv_ref, lng_ref, lnb_ref, wout_ref, bout_ref, o_ref):
    c = _silu(_layernorm(conv_ref[...], lng_ref[...], lnb_ref[...])).astype(BF16)
    o_ref[...] = x_ref[...] + _dot(c, wout_ref[...]) + bout_ref[...]


def _conv_sample_out(x, conv, lng, lnb, wout, bout):
    m, d = x.shape
    vm = lambda: pl.BlockSpec(memory_space=pltpu.VMEM)
    return pl.pallas_call(
        _conv_sample_out_kernel, in_specs=[vm() for _ in range(6)], out_specs=vm(),
        out_shape=jax.ShapeDtypeStruct((m, d), F32),
        compiler_params=pltpu.CompilerParams(vmem_limit_bytes=VMEM_LIMIT), name="conv_sample_out",
    )(x, conv, lng, lnb, wout, bout)


def _rope_tables(pos):
    half = DK // 2
    inv = ROPE_BASE ** (-jnp.arange(half, dtype=F32) / half)
    ang = pos.astype(F32)[:, None] * inv[None, :]
    cos = jnp.cos(ang)
    sin = jnp.sin(ang)
    return jnp.concatenate([cos, cos], axis=-1), jnp.concatenate([-sin, sin], axis=-1)


def _decay_tables(c):
    log_g = jnp.log1p(-(2.0 ** (-5.0 - jnp.arange(HEADS, dtype=F32))))
    idx = jnp.arange(c, dtype=F32)
    diff = idx[:, None] - idx[None, :]
    dmask = jnp.where(diff >= 0, jnp.exp(jnp.maximum(diff, 0.0)[None] * log_g[:, None, None]), 0.0)
    xi = jnp.exp((idx[None, :] + 1.0) * log_g[:, None])
    zeta = jnp.exp((c - 1.0 - idx[None, :]) * log_g[:, None])
    chunk_decay = jnp.exp(c * log_g)
    return dmask, xi, zeta, chunk_decay


def kernel(x_prompt, x_sample, state_ret, state_conv, norm_mix_g, norm_ffn_g, final_norm_g, ab_w_in, ab_w_out, ret_gn_g, sgu_ln_g, sgu_ln_b, sgu_w, sgu_b, conv_w_in, conv_b_in, conv_dw_w, conv_dw_b, conv_ln_g, conv_ln_b, conv_w_out, conv_b_out, ffn_w1, ffn_w2):
    b, t, d = x_prompt.shape
    m_s = x_sample.shape[0]
    row = lambda a: a.reshape(1, -1)

    win = ab_w_in[0].astype(BF16)
    wout = ab_w_out[0].astype(BF16)
    gmix = [row(norm_mix_g[l]) for l in range(2)]
    gffn = [row(norm_ffn_g[l]) for l in range(2)]
    gfin = row(final_norm_g)
    gn, lng, lnb = row(ret_gn_g[0]), row(sgu_ln_g[0]), row(sgu_ln_b[0])
    cbin, cdwb, clng, clnb, cbout = (row(conv_b_in[0]), row(conv_dw_b[0]), row(conv_ln_g[0]),
                                     row(conv_ln_b[0]), row(conv_b_out[0]))

    cos_p, sin_p = _rope_tables(jnp.arange(t))
    dmask, xi, zeta, cd = _decay_tables(CHUNK)
    xi_b = jnp.broadcast_to(xi[:, :, None], (HEADS, CHUNK, DV))
    zeta_b = jnp.broadcast_to(zeta[:, :, None], (HEADS, CHUNK, DK))
    bsgu_b = jnp.broadcast_to(sgu_b[0][:, :CHUNK, None], (GROUPS, CHUNK, GD))

    xs = x_sample.reshape(m_s, d)
    cos_s, sin_s = _rope_tables(PAST_LEN + jnp.arange(1))
    _, _, _, decay1 = _decay_tables(1)
    sgu_wb = jnp.stack([sgu_w[0][:, 0, 0], sgu_b[0][:, 0]])
    q, k, v, gate, sgu_out, rows = _mix_sample_in(xs, gmix[0], win, cos_s, sin_s, lng, lnb, sgu_wb)
    x1, ret_state_p, ret_state_s, o = _mix_prompt(x_prompt, gmix[0], win, wout, cos_p, sin_p, dmask, xi_b, zeta_b,
                                                  cd, gn, lng, lnb, sgu_w[0], bsgu_b, decay1, q, k, v, state_ret)
    xs1 = _mix_sample_out(xs, o, gate, sgu_out, gn, wout)

    x2, xs2, cwin, cwout = _ffn(x1.reshape(b * t, d), xs1, gffn[0], ffn_w1, ffn_w2, gfin, 0, False,
                                casts=(conv_w_in[0], conv_w_out[0]))

    xg = _conv_sample_in(xs2, gmix[1], cwin, cbin)
    hist_t = jnp.transpose(state_conv[0], (1, 0, 2))
    y_prompt, y_sample, conv_state_p, newhist_t = _layer1(
        x2, xs2, t, gmix[1], cwin, cbin, conv_dw_w[0], cdwb, clng, clnb, cwout, cbout, gffn[1], gfin,
        hist_t, xg, ffn_w1, ffn_w2, 1)
    conv_state_s = jnp.transpose(newhist_t, (1, 0, 2))[None]

    return (y_prompt.reshape(b, t, d), y_sample.reshape(m_s, 1, d), ret_state_p, ret_state_s,
            rows.reshape(1, m_s, 1, U_W), conv_state_p, conv_state_s)
```

```python
import functools

import jax
import jax.numpy as jnp
from jax import lax
from jax.experimental import pallas as pl
from jax.experimental.pallas import tpu as pltpu

F32 = jnp.float32
BF16 = jnp.bfloat16

D_MODEL = 1024
HEADS = 4
DK = 128
DV = 256
GROUPS = 4
GD = 256
CHUNK = 128
ROPE_BASE = 10000.0
CONV_WIDTH = 31
HIST = CONV_WIDTH - 1
D_FF = 4 * D_MODEL
RMS_EPS = 1e-6
LN_EPS = 1e-5
PAST_LEN = 16384

Q_W = HEADS * DK
V_W = HEADS * DV
U_W = GROUPS * GD
OFF_V = 2 * Q_W
OFF_G = OFF_V + V_W
OFF_U = OFF_G + V_W
OFF_S = OFF_U + U_W
D_IN = OFF_S + U_W

SUBLANES = 8
LANES = 128
VMEM_LIMIT = 56 * 1024 * 1024

TM_MIX = 512
TM_FFN = 512
TM_CONV = 512
CONV_RB = 32
LANE_TILES = D_MODEL // LANES
assert LANE_TILES == SUBLANES
HIST_PAD = 32
FFN_SLICES = 4
FFN_HS = D_FF // FFN_SLICES
CAST_STEPS = 8
BB_RET = 8
HEAD_PAIR = 2
BB_CONV = 16


def _params(*sem):
    return pltpu.CompilerParams(dimension_semantics=sem, vmem_limit_bytes=VMEM_LIMIT)


def _const_spec(shape):
    n = len(shape)
    return pl.BlockSpec(shape, lambda *_: (0,) * n, pipeline_mode=pl.Buffered(1))


def _smem_spec():
    return pl.BlockSpec(memory_space=pltpu.SMEM)


def _rms(x, g):
    ms = jnp.mean(x * x, axis=-1, keepdims=True)
    return x * lax.rsqrt(ms + RMS_EPS) * g


def _layernorm(x, g, b):
    mu = jnp.mean(x, axis=-1, keepdims=True)
    xc = x - mu
    var = jnp.mean(xc * xc, axis=-1, keepdims=True)
    return xc * lax.rsqrt(var + LN_EPS) * g + b


def _groupnorm(x, g):
    mu = jnp.mean(x, axis=-1, keepdims=True)
    xc = x - mu
    var = jnp.mean(xc * xc, axis=-1, keepdims=True)
    return xc * lax.rsqrt(var + LN_EPS) * g


def _sigmoid(x):
    return 1.0 / (1.0 + jnp.exp(-x))


def _silu(x):
    return x * _sigmoid(x)


def _gelu(x):
    return 0.5 * x * (1.0 + lax.erf(x * 0.7071067811865476))


def _dot(a, b):
    return jnp.dot(a, b, preferred_element_type=F32)


def _dot_nt(a, b):
    return lax.dot_general(a, b, (((1,), (1,)), ((), ())), preferred_element_type=F32)


def _dot_tn(a, b):
    return lax.dot_general(a, b, (((0,), (0,)), ((), ())), preferred_element_type=F32)


def _rotary(blk, cos2, sin2):
    return blk * cos2 + pltpu.roll(blk, DK // 2, 1) * sin2


def _sample_state_update(step, decay_ref, qs_ref, ks_ref, vs_ref, sin_ref, sout_ref, os_ref):
    pair = step % (HEADS // HEAD_PAIR)
    for li in range(HEAD_PAIR):
        ks = slice(li * DK, (li + 1) * DK)
        vs = slice(li * DV, (li + 1) * DV)
        qt = qs_ref[0, :, ks].T
        kt = ks_ref[0, :, ks].T
        gam = decay_ref[pair * HEAD_PAIR + li]
        for b in range(BB_RET):
            st = sin_ref[0, b, li]
            sn = gam * st + kt[:, b:b + 1] * vs_ref[0, b:b + 1, vs]
            sout_ref[0, b, li] = sn
            os_ref[0, b:b + 1, vs] = jnp.sum(qt[:, b:b + 1] * sn, axis=0, keepdims=True)


def _mix_prompt_kernel(cd_ref, decay_ref, x_ref, gmix_ref, win_ref, wout_ref, cos_ref, sin_ref,
                       dmask_ref, xi_ref, zeta_ref, gn_ref, lng_ref, lnb_ref,
                       wsgu_ref, bsgu_ref, qs_ref, ks_ref, vs_ref, sstate_ref,
                       x1_ref, state_ref, sstate_out_ref, os_ref,
                       q_s, k_s, kz_s, v_s, gate_s, u_s, sn_s, mix_s, inner_s, kv_s, sb_s):
    t = pl.program_id(1)
    tm = x_ref.shape[1]

    @pl.when(t == 0)
    def _():
        state_ref[...] = jnp.zeros_like(state_ref)

    _sample_state_update(pl.program_id(0) * pl.num_programs(1) + t, decay_ref, qs_ref, ks_ref, vs_ref,
                         sstate_ref, sstate_out_ref, os_ref)

    x = x_ref[0]
    h = _rms(x, gmix_ref[...]).astype(BF16)

    cos2 = cos_ref[...]
    sin2 = sin_ref[...]
    n_chunks = tm // CHUNK

    def proj(off, g):
        return _dot(h, win_ref[:, off + g * GD:off + (g + 1) * GD])

    for pair in range(Q_W // GD):
        qq = proj(0, pair)
        kk = proj(Q_W, pair)
        for e in range(GD // DK):
            j = pair * (GD // DK) + e
            ks = slice(j * DK, (j + 1) * DK)
            q_s[:, ks] = _rotary(qq[:, e * DK:(e + 1) * DK], cos2, sin2).astype(BF16)
            kj = _rotary(kk[:, e * DK:(e + 1) * DK], cos2, sin2) * (DK ** -0.5)
            k_s[:, ks] = kj.astype(BF16)
            for c in range(n_chunks):
                rs = slice(c * CHUNK, (c + 1) * CHUNK)
                kz_s[rs, ks] = (kj[rs, :] * zeta_ref[j]).astype(BF16)
    for g in range(GROUPS):
        gs = slice(g * GD, (g + 1) * GD)
        v_s[:, gs] = proj(OFF_V, g).astype(BF16)
        gate_s[:, gs] = _silu(proj(OFF_G, g))
        u_s[:, gs] = _gelu(proj(OFF_U, g))
        sn_s[:, gs] = _layernorm(_gelu(proj(OFF_S, g)), lng_ref[:, gs], lnb_ref[:, gs]).astype(BF16)

    row = lax.broadcasted_iota(jnp.int32, (CHUNK, CHUNK), 0)
    col = lax.broadcasted_iota(jnp.int32, (CHUNK, CHUNK), 1)
    tril = row >= col

    pairs = [(c, hh) for c in range(n_chunks) for hh in range(HEADS)]
    for c, hh in pairs:
        rs = slice(c * CHUNK, (c + 1) * CHUNK)
        ks = slice(hh * DK, (hh + 1) * DK)
        inner = _dot_nt(q_s[rs, ks], k_s[rs, ks]) * dmask_ref[hh]
        inner_s[c * HEADS + hh] = inner.astype(BF16)
    for c, hh in pairs:
        rs = slice(c * CHUNK, (c + 1) * CHUNK)
        kv_s[c * HEADS + hh] = _dot_tn(kz_s[rs, hh * DK:(hh + 1) * DK], v_s[rs, hh * DV:(hh + 1) * DV])
    for hh in range(HEADS):
        st = state_ref[0, 0, hh]
        for c in range(n_chunks):
            sb_s[c * HEADS + hh] = st.astype(BF16)
            st = cd_ref[hh] * st + kv_s[c * HEADS + hh]
        state_ref[0, 0, hh] = st
    for c, hh in pairs:
        rs = slice(c * CHUNK, (c + 1) * CHUNK)
        vs = slice(hh * DV, (hh + 1) * DV)
        o = _dot(inner_s[c * HEADS + hh], v_s[rs, vs])
        o = o + _dot(q_s[rs, hh * DK:(hh + 1) * DK], sb_s[c * HEADS + hh]) * xi_ref[hh]
        on = _groupnorm(o, gn_ref[:, vs])
        mix_s[rs, vs] = (gate_s[rs, vs] * on).astype(BF16)

    for c in range(n_chunks):
        rs = slice(c * CHUNK, (c + 1) * CHUNK)
        for g in range(GROUPS):
            gs = slice(g * GD, (g + 1) * GD)
            wg = jnp.where(tril, wsgu_ref[g], 0.0).astype(BF16)
            mixed = _dot(wg, sn_s[rs, gs]) + bsgu_ref[g]
            mix_s[rs, V_W + g * GD:V_W + (g + 1) * GD] = (u_s[rs, gs] * mixed).astype(BF16)

    x1_ref[0] = x + _dot(mix_s[...], wout_ref[...])


def _mix_prompt(x, gmix, win, wout, cos2, sin2, dmask, xi_b, zeta_b, cd, gn, lng, lnb, wsgu, bsgu_b,
                decay1, qs2, ks2, vs2, sstate):
    b, t, d = x.shape
    tm = TM_MIX
    nt = t // tm
    n_pairs = (tm // CHUNK) * HEADS
    grid = (b, nt)
    n_hp, ms, _ = qs2.shape
    assert b * nt == (ms // BB_RET) * n_hp
    pair_blk = lambda i, j: ((i * nt + j) % n_hp, (i * nt + j) // n_hp, 0)
    state_blk = lambda i, j: (0, (i * nt + j) // n_hp, (i * nt + j) % n_hp, 0, 0)
    in_specs = [
        _smem_spec(),
        _smem_spec(),
        pl.BlockSpec((1, tm, d), lambda i, j: (i, j, 0)),
        _const_spec((1, d)),
        _const_spec((d, D_IN)),
        _const_spec((V_W + U_W, d)),
        pl.BlockSpec((tm, DK), lambda i, j: (j, 0)),
        pl.BlockSpec((tm, DK), lambda i, j: (j, 0)),
        _const_spec((HEADS, CHUNK, CHUNK)),
        _const_spec((HEADS, CHUNK, DV)),
        _const_spec((HEADS, CHUNK, DK)),
        _const_spec((1, V_W)),
        _const_spec((1, U_W)),
        _const_spec((1, U_W)),
        _const_spec((GROUPS, CHUNK, CHUNK)),
        _const_spec((GROUPS, CHUNK, GD)),
        pl.BlockSpec((1, BB_RET, HEAD_PAIR * DK), pair_blk),
        pl.BlockSpec((1, BB_RET, HEAD_PAIR * DK), pair_blk),
        pl.BlockSpec((1, BB_RET, HEAD_PAIR * DV), pair_blk),
        pl.BlockSpec((1, BB_RET, HEAD_PAIR, DK, DV), state_blk),
    ]
    out_specs = [
        pl.BlockSpec((1, tm, d), lambda i, j: (i, j, 0)),
        pl.BlockSpec((1, 1, HEADS, DK, DV), lambda i, j: (0, i, 0, 0, 0)),
        pl.BlockSpec((1, BB_RET, HEAD_PAIR, DK, DV), state_blk),
        pl.BlockSpec((1, BB_RET, HEAD_PAIR * DV), pair_blk),
    ]
    out_shape = [
        jax.ShapeDtypeStruct((b, t, d), F32),
        jax.ShapeDtypeStruct((1, b, HEADS, DK, DV), F32),
        jax.ShapeDtypeStruct(sstate.shape, F32),
        jax.ShapeDtypeStruct((n_hp, ms, HEAD_PAIR * DV), F32),
    ]
    scratch = [
        pltpu.VMEM((tm, Q_W), BF16),
        pltpu.VMEM((tm, Q_W), BF16),
        pltpu.VMEM((tm, Q_W), BF16),
        pltpu.VMEM((tm, V_W), BF16),
        pltpu.VMEM((tm, V_W), F32),
        pltpu.VMEM((tm, U_W), F32),
        pltpu.VMEM((tm, U_W), BF16),
        pltpu.VMEM((tm, V_W + U_W), BF16),
        pltpu.VMEM((n_pairs, CHUNK, CHUNK), BF16),
        pltpu.VMEM((n_pairs, DK, DV), F32),
        pltpu.VMEM((n_pairs, DK, DV), BF16),
    ]
    x1, state_p, state_s, o2 = pl.pallas_call(
        _mix_prompt_kernel, grid=grid, in_specs=in_specs, out_specs=out_specs,
        out_shape=out_shape, scratch_shapes=scratch,
        compiler_params=_params("arbitrary", "arbitrary"), name="mix_prompt",
    )(cd, decay1, x, gmix, win, wout, cos2, sin2, dmask, xi_b, zeta_b, gn, lng, lnb, wsgu, bsgu_b,
      qs2, ks2, vs2, sstate)
    return x1, state_p, state_s, o2


def _ffn_tile(x, g, gfin, w1_s, w2_s, final, before_slice=None):
    h = _rms(x, g).astype(BF16)
    acc = x
    for j in range(FFN_SLICES):
        if before_slice is not None:
            before_slice(j)
        js = slice(j * FFN_HS, (j + 1) * FFN_HS)
        f = _dot(h, w1_s[:, js])
        a = jnp.square(jnp.maximum(f, 0.0)).astype(BF16)
        acc = acc + _dot(a, w2_s[js, :])
    if final:
        acc = _rms(acc, gfin)
    return acc


def _ffn_kernel(*refs, layer, final, n_tiles, n_casts):
    xp_ref, xs_ref, g_ref, gfin_ref, w1_hbm, w2_hbm = refs[:6]
    cast_in = refs[6:6 + n_casts]
    yp_ref, ys_ref = refs[6 + n_casts:8 + n_casts]
    cast_out = refs[8 + n_casts:8 + 2 * n_casts]
    w1_s, w2_s, st1, st2, sem = refs[8 + 2 * n_casts:]
    i = pl.program_id(0)

    if n_casts:
        @pl.when(jnp.logical_and(i >= 1, i <= CAST_STEPS))
        def _():
            for src, dst in zip(cast_in, cast_out):
                dst[...] = src[...].astype(BF16)

    def slice_copies(j):
        slot = j % 2
        js = pl.ds(j * FFN_HS, FFN_HS)
        return (pltpu.make_async_copy(w1_hbm.at[layer, :, js], st1.at[slot], sem.at[0, slot]),
                pltpu.make_async_copy(w2_hbm.at[layer, js, :], st2.at[slot], sem.at[1, slot]))

    def fetch_slice(j):
        js = slice(j * FFN_HS, (j + 1) * FFN_HS)
        for c in slice_copies(j):
            c.wait()
        w1_s[:, js] = st1[j % 2].astype(BF16)
        w2_s[js, :] = st2[j % 2].astype(BF16)
        if j + 2 < FFN_SLICES:
            for c in slice_copies(j + 2):
                c.start()

    @pl.when(i == 0)
    def _():
        for j in range(min(2, FFN_SLICES)):
            for c in slice_copies(j):
                c.start()
        yp_ref[...] = _ffn_tile(xp_ref[...], g_ref[...], gfin_ref[...], w1_s, w2_s, final, fetch_slice)

    @pl.when(jnp.logical_and(i > 0, i < n_tiles))
    def _():
        yp_ref[...] = _ffn_tile(xp_ref[...], g_ref[...], gfin_ref[...], w1_s, w2_s, final)

    @pl.when(i == n_tiles)
    def _():
        ys_ref[...] = _ffn_tile(xs_ref[...], g_ref[...], gfin_ref[...], w1_s, w2_s, final)


def _ffn(xp, xs, g, w1, w2, gfin, layer, final, casts=()):
    m, d = xp.shape
    ms = xs.shape[0]
    tm = TM_FFN
    n_tiles = m // tm
    assert n_tiles > CAST_STEPS
    tile = lambda i: (jnp.minimum(i, n_tiles - 1), 0)
    cast_blk = lambda i: (jnp.clip(i - 1, 0, CAST_STEPS - 1), 0)
    cast_specs = [pl.BlockSpec((c.shape[0] // CAST_STEPS, c.shape[1]), cast_blk) for c in casts]
    return pl.pallas_call(
        functools.partial(_ffn_kernel, layer=layer, final=final, n_tiles=n_tiles, n_casts=len(casts)),
        grid=(n_tiles + 1,),
        in_specs=[pl.BlockSpec((tm, d), tile), _const_spec((ms, d)),
                  _const_spec((1, d)), _const_spec((1, d)),
                  pl.BlockSpec(memory_space=pl.ANY), pl.BlockSpec(memory_space=pl.ANY)] + cast_specs,
        out_specs=[pl.BlockSpec((tm, d), tile), pl.BlockSpec((ms, d), lambda i: (0, 0))] + cast_specs,
        out_shape=[jax.ShapeDtypeStruct((m, d), F32), jax.ShapeDtypeStruct((ms, d), F32)]
        + [jax.ShapeDtypeStruct(c.shape, BF16) for c in casts],
        scratch_shapes=[pltpu.VMEM((d, D_FF), BF16), pltpu.VMEM((D_FF, d), BF16),
                        pltpu.VMEM((2, d, FFN_HS), F32), pltpu.VMEM((2, FFN_HS, d), F32),
                        pltpu.SemaphoreType.DMA((2, 2))],
        compiler_params=_params("arbitrary"), name="ffn_final" if final else "ffn",
    )(xp, xs, g, gfin, w1, w2, *casts)


def _glu_pieces(x, g, win_ref, bin_ref):
    h = _rms(x, g).astype(BF16)
    for c in range(D_MODEL // GD):
        ca = slice(c * GD, (c + 1) * GD)
        cg = slice(D_MODEL + c * GD, D_MODEL + (c + 1) * GD)
        a = _dot(h, win_ref[:, ca]) + bin_ref[:, ca]
        gt = _dot(h, win_ref[:, cg]) + bin_ref[:, cg]
        yield c, a * _sigmoid(gt)


def _glu_in(x, g, win_ref, bin_ref):
    return jnp.concatenate([p for _, p in _glu_pieces(x, g, win_ref, bin_ref)], axis=-1)


def _from_token_rows(buf, first_token, n):
    return jnp.concatenate(
        [buf[pl.ds(first_token * LANE_TILES + g, n, stride=LANE_TILES), :] for g in range(LANE_TILES)], axis=-1)


def _conv_block(r, full_s, conv_s, w3_ref, b3_ref):
    off0 = HIST_PAD - HIST
    acc = jnp.broadcast_to(b3_ref[...][None], (CONV_RB, LANE_TILES, LANES))
    for j in range(CONV_WIDTH):
        start = pl.multiple_of((r * CONV_RB + j + off0) * LANE_TILES, LANE_TILES)
        xj = full_s[pl.ds(start, CONV_RB * LANE_TILES), :].reshape(CONV_RB, LANE_TILES, LANES)
        acc = acc + xj * w3_ref[j][None]
    out0 = pl.multiple_of(r * CONV_RB * LANE_TILES, CONV_RB * LANE_TILES)
    conv_s[pl.ds(out0, CONV_RB * LANE_TILES), :] = acc.reshape(CONV_RB * LANE_TILES, LANES)


def _conv_prompt_kernel(x_ref, xprev_ref, gmix_ref, win_ref, bin_ref, dww_ref, dwb_ref, w3_ref, b3_ref,
                        lng_ref, lnb_ref, wout_ref, bout_ref, shist_ref, sxg_ref,
                        o_ref, hist_ref, snewhist_ref, sconv_ref,
                        full_s, conv_s, *, n_tiles, tiles_per_seq, sample_every):
    s = pl.program_id(0)
    tm = x_ref.shape[0]
    off0 = HIST_PAD - HIST
    hist_rows = HIST_PAD * LANE_TILES

    @pl.when(jnp.logical_and(s % sample_every == 0, s < n_tiles))
    def _():
        _conv_sample_state_update(shist_ref, sxg_ref, dww_ref, dwb_ref, snewhist_ref, sconv_ref)

    def front():
        for c, piece in _glu_pieces(x_ref[...], gmix_ref[...], win_ref, bin_ref):
            for e in range(GD // LANES):
                g = c * (GD // LANES) + e
                full_s[pl.ds(hist_rows + g, tm, stride=LANE_TILES), :] = piece[:, e * LANES:(e + 1) * LANES]

    def back():
        conv = _from_token_rows(conv_s, 0, tm)
        c = _silu(_layernorm(conv, lng_ref[...], lnb_ref[...])).astype(BF16)
        o_ref[...] = xprev_ref[...] + _dot(c, wout_ref[...]) + bout_ref[...]

    @pl.when(s % tiles_per_seq == 0)
    def _():
        full_s[0:hist_rows, :] = jnp.zeros((hist_rows, LANES), F32)

    @pl.when(s == 0)
    def _():
        front()

    @pl.when(jnp.logical_and(s > 0, s < n_tiles))
    def _():
        front()
        back()

    @pl.when(s == n_tiles)
    def _():
        back()

    @pl.when(s < n_tiles)
    def _():
        def body(r, carry):
            _conv_block(r, full_s, conv_s, w3_ref, b3_ref)
            return carry
        lax.fori_loop(0, tm // CONV_RB, body, 0)

        @pl.when(s % tiles_per_seq == tiles_per_seq - 1)
        def _():
            hist_ref[0, 0] = _from_token_rows(full_s, tm + off0, HIST)

        full_s[0:hist_rows, :] = full_s[tm * LANE_TILES:tm * LANE_TILES + hist_rows, :]


def _conv_prompt(x, gmix, win, bin_, dww, dwb, lng, lnb, wout, bout, shist_t, sxg):
    b, t, d = x.shape
    tm = TM_CONV
    n_tiles = (b * t) // tm
    tiles_per_seq = t // tm
    ms = sxg.shape[0]
    n_sblocks = ms // BB_CONV
    sample_every = n_tiles // n_sblocks
    assert sample_every * n_sblocks == n_tiles
    sblock = lambda s: jnp.minimum(s // sample_every, n_sblocks - 1)
    in_specs = [
        pl.BlockSpec((tm, d), lambda s: (jnp.minimum(s, n_tiles - 1), 0)),
        pl.BlockSpec((tm, d), lambda s: (jnp.maximum(s - 1, 0), 0)),
        _const_spec((1, d)), _const_spec((d, 2 * d)), _const_spec((1, 2 * d)),
        _const_spec((CONV_WIDTH, d)), _const_spec((1, d)),
        _const_spec((CONV_WIDTH, LANE_TILES, LANES)), _const_spec((LANE_TILES, LANES)),
        _const_spec((1, d)), _const_spec((1, d)),
        _const_spec((d, d)), _const_spec((1, d)),
        pl.BlockSpec((HIST, BB_CONV, d), lambda s: (0, sblock(s), 0)),
        pl.BlockSpec((BB_CONV, d), lambda s: (sblock(s), 0)),
    ]
    out_specs = [
        pl.BlockSpec((tm, d), lambda s: (jnp.maximum(s - 1, 0), 0)),
        pl.BlockSpec((1, 1, HIST, d), lambda s: (0, jnp.minimum(s, n_tiles - 1) // tiles_per_seq, 0, 0)),
        pl.BlockSpec((HIST, BB_CONV, d), lambda s: (0, sblock(s), 0)),
        pl.BlockSpec((BB_CONV, d), lambda s: (sblock(s), 0)),
    ]
    out_shape = [jax.ShapeDtypeStruct((b * t, d), F32),
                 jax.ShapeDtypeStruct((1, b, HIST, d), F32),
                 jax.ShapeDtypeStruct(shist_t.shape, F32),
                 jax.ShapeDtypeStruct((ms, d), F32)]
    scratch = [
        pltpu.VMEM(((tm + HIST_PAD) * LANE_TILES, LANES), F32),
        pltpu.VMEM((tm * LANE_TILES, LANES), F32),
    ]
    x2d = x.reshape(b * t, d)
    w3 = dww.reshape(CONV_WIDTH, LANE_TILES, LANES)
    b3 = dwb.reshape(LANE_TILES, LANES)
    return pl.pallas_call(
        functools.partial(_conv_prompt_kernel, n_tiles=n_tiles, tiles_per_seq=tiles_per_seq,
                          sample_every=sample_every),
        grid=(n_tiles + 1,), in_specs=in_specs, out_specs=out_specs,
        out_shape=out_shape, scratch_shapes=scratch,
        compiler_params=_params("arbitrary"), name="conv_prompt",
    )(x2d, x2d, gmix, win, bin_, dww, dwb, w3, b3, lng, lnb, wout, bout, shist_t, sxg)


def _mix_sample_in_kernel(sgu_ref, x_ref, gmix_ref, win_ref, cos_ref, sin_ref, lng_ref, lnb_ref,
                          q_ref, k_ref, v_ref, gate_ref, sgu_out_ref, rows_ref):
    h = _rms(x_ref[...], gmix_ref[...]).astype(BF16)
    qk = _dot(h, win_ref[:, 0:OFF_V])
    cos2 = cos_ref[...]
    sin2 = sin_ref[...]
    for j in range(HEADS):
        p, li = divmod(j, HEAD_PAIR)
        q_ref[p, :, li * DK:(li + 1) * DK] = _rotary(qk[:, j * DK:(j + 1) * DK], cos2, sin2)
        kj = qk[:, Q_W + j * DK:Q_W + (j + 1) * DK]
        k_ref[p, :, li * DK:(li + 1) * DK] = _rotary(kj, cos2, sin2) * (DK ** -0.5)
    v = _dot(h, win_ref[:, OFF_V:OFF_G])
    for p in range(HEADS // HEAD_PAIR):
        v_ref[p] = v[:, p * HEAD_PAIR * DV:(p + 1) * HEAD_PAIR * DV]
    gate_ref[...] = _silu(_dot(h, win_ref[:, OFF_G:OFF_U]))
    u = _gelu(_dot(h, win_ref[:, OFF_U:OFF_S]))
    s = _gelu(_dot(h, win_ref[:, OFF_S:D_IN]))
    for g in range(GROUPS):
        gs = slice(g * GD, (g + 1) * GD)
        sn = _layernorm(s[:, gs], lng_ref[:, gs], lnb_ref[:, gs])
        rows_ref[:, gs] = sn
        sgu_out_ref[:, gs] = u[:, gs] * (sgu_ref[0, g] * sn + sgu_ref[1, g])


def _mix_sample_in(x, gmix, win, cos2, sin2, lng, lnb, sgu_wb):
    m, d = x.shape
    shp = lambda n: jax.ShapeDtypeStruct((m, n), F32)
    n_hp = HEADS // HEAD_PAIR
    paired = lambda w: jax.ShapeDtypeStruct((n_hp, m, HEAD_PAIR * w), F32)
    vm = lambda: pl.BlockSpec(memory_space=pltpu.VMEM)
    return pl.pallas_call(
        _mix_sample_in_kernel,
        in_specs=[_smem_spec()] + [vm() for _ in range(7)],
        out_specs=[vm() for _ in range(6)],
        out_shape=[paired(DK), paired(DK), paired(DV), shp(V_W), shp(U_W), shp(U_W)],
        compiler_params=pltpu.CompilerParams(vmem_limit_bytes=VMEM_LIMIT), name="mix_sample_in",
    )(sgu_wb, x, gmix, win, cos2, sin2, lng, lnb)


def _mix_sample_out_kernel(x_ref, o_ref, gate_ref, sgu_ref, gn_ref, wout_ref, x1_ref):
    parts = []
    for hh in range(HEADS):
        p, li = divmod(hh, HEAD_PAIR)
        vs = slice(hh * DV, (hh + 1) * DV)
        o = o_ref[p, :, li * DV:(li + 1) * DV]
        parts.append((gate_ref[:, vs] * _groupnorm(o, gn_ref[:, vs])).astype(BF16))
    y = _dot(jnp.concatenate(parts, axis=-1), wout_ref[0:V_W, :])
    y = y + _dot(sgu_ref[...].astype(BF16), wout_ref[V_W:V_W + U_W, :])
    x1_ref[...] = x_ref[...] + y


def _mix_sample_out(x, o, gate, sgu_out, gn, wout):
    m, d = x.shape
    vm = lambda: pl.BlockSpec(memory_space=pltpu.VMEM)
    return pl.pallas_call(
        _mix_sample_out_kernel, in_specs=[vm() for _ in range(6)], out_specs=vm(),
        out_shape=jax.ShapeDtypeStruct((m, d), F32),
        compiler_params=pltpu.CompilerParams(vmem_limit_bytes=VMEM_LIMIT), name="mix_sample_out",
    )(x, o, gate, sgu_out, gn, wout)


def _conv_sample_in_kernel(x_ref, g_ref, win_ref, bin_ref, xg_ref):
    xg_ref[...] = _glu_in(x_ref[...], g_ref[...], win_ref, bin_ref)


def _conv_sample_in(x, g, win, bin_):
    m, d = x.shape
    vm = lambda: pl.BlockSpec(memory_space=pltpu.VMEM)
    return pl.pallas_call(
        _conv_sample_in_kernel, in_specs=[vm() for _ in range(4)], out_specs=vm(),
        out_shape=jax.ShapeDtypeStruct((m, d), F32),
        compiler_params=pltpu.CompilerParams(vmem_limit_bytes=VMEM_LIMIT), name="conv_sample_in",
    )(x, g, win, bin_)


def _conv_sample_state_update(hist_ref, xg_ref, dww_ref, dwb_ref, newhist_ref, conv_ref):
    xg = xg_ref[...]
    acc = dwb_ref[...] + xg * dww_ref[HIST:CONV_WIDTH, :]
    for j in range(HIST):
        acc = acc + hist_ref[j] * dww_ref[j:j + 1, :]
    conv_ref[...] = acc
    for j in range(HIST - 1):
        newhist_ref[j] = hist_ref[j + 1]
    newhist_ref[HIST - 1] = xg


def _conv_sample_out_kernel(x_ref, conv_ref, lng_ref, lnb_ref, wout_ref, bout_ref, o_ref):
    c = _silu(_layernorm(conv_ref[...], lng_ref[...], lnb_ref[...])).astype(BF16)
    o_ref[...] = x_ref[...] + _dot(c, wout_ref[...]) + bout_ref[...]


def _conv_sample_out(x, conv, lng, lnb, wout, bout):
    m, d = x.shape
    vm = lambda: pl.BlockSpec(memory_space=pltpu.VMEM)
    return pl.pallas_call(
        _conv_sample_out_kernel, in_specs=[vm() for _ in range(6)], out_specs=vm(),
        out_shape=jax.ShapeDtypeStruct((m, d), F32),
        compiler_params=pltpu.CompilerParams(vmem_limit_bytes=VMEM_LIMIT), name="conv_sample_out",
    )(x, conv, lng, lnb, wout, bout)


def _rope_tables(pos):
    half = DK // 2
    inv = ROPE_BASE ** (-jnp.arange(half, dtype=F32) / half)
    ang = pos.astype(F32)[:, None] * inv[None, :]
    cos = jnp.cos(ang)
    sin = jnp.sin(ang)
    return jnp.concatenate([cos, cos], axis=-1), jnp.concatenate([-sin, sin], axis=-1)


def _decay_tables(c):
    log_g = jnp.log1p(-(2.0 ** (-5.0 - jnp.arange(HEADS, dtype=F32))))
    idx = jnp.arange(c, dtype=F32)
    diff = idx[:, None] - idx[None, :]
    dmask = jnp.where(diff >= 0, jnp.exp(jnp.maximum(diff, 0.0)[None] * log_g[:, None, None]), 0.0)
    xi = jnp.exp((idx[None, :] + 1.0) * log_g[:, None])
    zeta = jnp.exp((c - 1.0 - idx[None, :]) * log_g[:, None])
    chunk_decay = jnp.exp(c * log_g)
    return dmask, xi, zeta, chunk_decay


def kernel(x_prompt, x_sample, state_ret, state_conv, norm_mix_g, norm_ffn_g, final_norm_g, ab_w_in, ab_w_out, ret_gn_g, sgu_ln_g, sgu_ln_b, sgu_w, sgu_b, conv_w_in, conv_b_in, conv_dw_w, conv_dw_b, conv_ln_g, conv_ln_b, conv_w_out, conv_b_out, ffn_w1, ffn_w2):
    b, t, d = x_prompt.shape
    m_s = x_sample.shape[0]
    row = lambda a: a.reshape(1, -1)

    win = ab_w_in[0].astype(BF16)
    wout = ab_w_out[0].astype(BF16)
    gmix = [row(norm_mix_g[l]) for l in range(2)]
    gffn = [row(norm_ffn_g[l]) for l in range(2)]
    gfin = row(final_norm_g)
    gn, lng, lnb = row(ret_gn_g[0]), row(sgu_ln_g[0]), row(sgu_ln_b[0])
    cbin, cdwb, clng, clnb, cbout = (row(conv_b_in[0]), row(conv_dw_b[0]), row(conv_ln_g[0]),
                                     row(conv_ln_b[0]), row(conv_b_out[0]))

    cos_p, sin_p = _rope_tables(jnp.arange(t))
    dmask, xi, zeta, cd = _decay_tables(CHUNK)
    xi_b = jnp.broadcast_to(xi[:, :, None], (HEADS, CHUNK, DV))
    zeta_b = jnp.broadcast_to(zeta[:, :, None], (HEADS, CHUNK, DK))
    bsgu_b = jnp.broadcast_to(sgu_b[0][:, :CHUNK, None], (GROUPS, CHUNK, GD))

    xs = x_sample.reshape(m_s, d)
    cos_s, sin_s = _rope_tables(PAST_LEN + jnp.arange(1))
    decay1 = xi[:, 0]
    sgu_wb = jnp.stack([sgu_w[0][:, 0, 0], sgu_b[0][:, 0]])
    q, k, v, gate, sgu_out, rows = _mix_sample_in(xs, gmix[0], win, cos_s, sin_s, lng, lnb, sgu_wb)
    x1, ret_state_p, ret_state_s, o = _mix_prompt(x_prompt, gmix[0], win, wout, cos_p, sin_p, dmask, xi_b, zeta_b,
                                                  cd, gn, lng, lnb, sgu_w[0], bsgu_b, decay1, q, k, v, state_ret)
    xs1 = _mix_sample_out(xs, o, gate, sgu_out, gn, wout)

    x2, xs2, cwin, cwout = _ffn(x1.reshape(b * t, d), xs1, gffn[0], ffn_w1, ffn_w2, gfin, 0, False,
                                casts=(conv_w_in[0], conv_w_out[0]))

    xg = _conv_sample_in(xs2, gmix[1], cwin, cbin)
    hist_t = jnp.transpose(state_conv[0], (1, 0, 2))
    x3, conv_state_p, newhist_t, conv = _conv_prompt(x2.reshape(b, t, d), gmix[1], cwin, cbin, conv_dw_w[0], cdwb,
                                                     clng, clnb, cwout, cbout, hist_t, xg)
    conv_state_s = jnp.transpose(newhist_t, (1, 0, 2))[None]
    xs3 = _conv_sample_out(xs2, conv, clng, clnb, cwout, cbout)

    y_prompt, y_sample = _ffn(x3, xs3, gffn[1], ffn_w1, ffn_w2, gfin, 1, True)

    return (y_prompt.reshape(b, t, d), y_sample.reshape(m_s, 1, d), ret_state_p, ret_state_s,
            rows.reshape(1, m_s, 1, U_W), conv_state_p, conv_state_s)
```

```python
import functools

import jax
import jax.numpy as jnp
from jax import lax
from jax.experimental import pallas as pl
from jax.experimental.pallas import tpu as pltpu

F32 = jnp.float32
BF16 = jnp.bfloat16

D_MODEL = 1024
HEADS = 4
DK = 128
DV = 256
GROUPS = 4
GD = 256
CHUNK = 128
RET_CHUNK = 256
ROPE_BASE = 10000.0
CONV_WIDTH = 31
HIST = CONV_WIDTH - 1
D_FF = 4 * D_MODEL
RMS_EPS = 1e-6
LN_EPS = 1e-5
PAST_LEN = 16384

Q_W = HEADS * DK
V_W = HEADS * DV
U_W = GROUPS * GD
OFF_V = 2 * Q_W
OFF_G = OFF_V + V_W
OFF_U = OFF_G + V_W
OFF_S = OFF_U + U_W
D_IN = OFF_S + U_W

SUBLANES = 8
LANES = 128
VMEM_LIMIT = 60 * 1024 * 1024

TM_MIX = 512
TM_FFN = 512
TM_CONV = 512
CONV_RB = 128
LANE_TILES = D_MODEL // LANES
assert LANE_TILES == SUBLANES
HIST_PAD = 32
FFN_SLICES = 4
FFN_HS = D_FF // FFN_SLICES
CAST_STEPS = 8
BB_RET = 8
HEAD_PAIR = 2
BB_CONV = 16


def _params(*sem):
    return pltpu.CompilerParams(dimension_semantics=sem, vmem_limit_bytes=VMEM_LIMIT)


def _const_spec(shape):
    n = len(shape)
    return pl.BlockSpec(shape, lambda *_: (0,) * n, pipeline_mode=pl.Buffered(1))


def _smem_spec():
    return pl.BlockSpec(memory_space=pltpu.SMEM)


def _rms(x, g):
    ms = jnp.mean(x * x, axis=-1, keepdims=True)
    return x * lax.rsqrt(ms + RMS_EPS) * g


def _layernorm(x, g, b):
    mu = jnp.mean(x, axis=-1, keepdims=True)
    xc = x - mu
    var = jnp.mean(xc * xc, axis=-1, keepdims=True)
    return xc * lax.rsqrt(var + LN_EPS) * g + b


def _groupnorm(x, g):
    mu = jnp.mean(x, axis=-1, keepdims=True)
    xc = x - mu
    var = jnp.mean(xc * xc, axis=-1, keepdims=True)
    return xc * lax.rsqrt(var + LN_EPS) * g


def _sigmoid(x):
    return 1.0 / (1.0 + jnp.exp(-x))


def _silu(x):
    return x * _sigmoid(x)


def _gelu(x):
    return 0.5 * x * (1.0 + lax.erf(x * 0.7071067811865476))


def _dot(a, b):
    return jnp.dot(a, b, preferred_element_type=F32)


def _dot_nt(a, b):
    return lax.dot_general(a, b, (((1,), (1,)), ((), ())), preferred_element_type=F32)


def _dot_tn(a, b):
    return lax.dot_general(a, b, (((0,), (0,)), ((), ())), preferred_element_type=F32)


def _rotary(blk, cos2, sin2):
    return blk * cos2 + pltpu.roll(blk, DK // 2, 1) * sin2


def _sample_state_update(step, decay_ref, qs_ref, ks_ref, vs_ref, sin_ref, sout_ref, os_ref):
    pair = step % (HEADS // HEAD_PAIR)
    for li in range(HEAD_PAIR):
        ks = slice(li * DK, (li + 1) * DK)
        vs = slice(li * DV, (li + 1) * DV)
        qt = qs_ref[0, :, ks].T
        kt = ks_ref[0, :, ks].T
        gam = decay_ref[pair * HEAD_PAIR + li]
        for b in range(BB_RET):
            st = sin_ref[0, b, li]
            sn = gam * st + kt[:, b:b + 1] * vs_ref[0, b:b + 1, vs]
            sout_ref[0, b, li] = sn
            os_ref[0, b:b + 1, vs] = jnp.sum(qt[:, b:b + 1] * sn, axis=0, keepdims=True)


def _mix_prompt_kernel(cd_ref, decay_ref, x_ref, gmix_ref, win_ref, wout_ref, cos_ref, sin_ref,
                       dmask_ref, xi_ref, zeta_ref, gn_ref, lng_ref, lnb_ref,
                       wsgu_ref, bsgu_ref, qs_ref, ks_ref, vs_ref, sstate_ref,
                       x1_ref, state_ref, sstate_out_ref, os_ref,
                       q_s, k_s, kz_s, v_s, gate_s, u_s, sn_s, mix_s, inner_s, kv_s, sb_s):
    t = pl.program_id(1)
    tm = x_ref.shape[1]

    @pl.when(t == 0)
    def _():
        state_ref[...] = jnp.zeros_like(state_ref)

    _sample_state_update(pl.program_id(0) * pl.num_programs(1) + t, decay_ref, qs_ref, ks_ref, vs_ref,
                         sstate_ref, sstate_out_ref, os_ref)

    x = x_ref[0]
    h = _rms(x, gmix_ref[...]).astype(BF16)

    cos2 = cos_ref[...]
    sin2 = sin_ref[...]
    n_chunks = tm // CHUNK
    n_rc = tm // RET_CHUNK

    def proj(off, g):
        return _dot(h, win_ref[:, off + g * GD:off + (g + 1) * GD])

    for pair in range(Q_W // GD):
        qq = proj(0, pair)
        kk = proj(Q_W, pair)
        for e in range(GD // DK):
            j = pair * (GD // DK) + e
            ks = slice(j * DK, (j + 1) * DK)
            q_s[:, ks] = _rotary(qq[:, e * DK:(e + 1) * DK], cos2, sin2).astype(BF16)
            kj = _rotary(kk[:, e * DK:(e + 1) * DK], cos2, sin2) * (DK ** -0.5)
            k_s[:, ks] = kj.astype(BF16)
            for c in range(n_rc):
                rs = slice(c * RET_CHUNK, (c + 1) * RET_CHUNK)
                kz_s[rs, ks] = (kj[rs, :] * zeta_ref[j]).astype(BF16)
    for g in range(GROUPS):
        gs = slice(g * GD, (g + 1) * GD)
        v_s[:, gs] = proj(OFF_V, g).astype(BF16)
        gate_s[:, gs] = _silu(proj(OFF_G, g))
        u_s[:, gs] = _gelu(proj(OFF_U, g))
        sn_s[:, gs] = _layernorm(_gelu(proj(OFF_S, g)), lng_ref[:, gs], lnb_ref[:, gs]).astype(BF16)

    row = lax.broadcasted_iota(jnp.int32, (CHUNK, CHUNK), 0)
    col = lax.broadcasted_iota(jnp.int32, (CHUNK, CHUNK), 1)
    tril = row >= col

    pairs = [(c, hh) for c in range(n_rc) for hh in range(HEADS)]
    for c, hh in pairs:
        rs = slice(c * RET_CHUNK, (c + 1) * RET_CHUNK)
        ks = slice(hh * DK, (hh + 1) * DK)
        inner = _dot_nt(q_s[rs, ks], k_s[rs, ks]) * dmask_ref[hh]
        inner_s[c * HEADS + hh] = inner.astype(BF16)
    for c, hh in pairs:
        rs = slice(c * RET_CHUNK, (c + 1) * RET_CHUNK)
        kv_s[c * HEADS + hh] = _dot_tn(kz_s[rs, hh * DK:(hh + 1) * DK], v_s[rs, hh * DV:(hh + 1) * DV])
    for hh in range(HEADS):
        st = state_ref[0, 0, hh]
        for c in range(n_rc):
            sb_s[c * HEADS + hh] = st.astype(BF16)
            st = cd_ref[hh] * st + kv_s[c * HEADS + hh]
        state_ref[0, 0, hh] = st
    for c, hh in pairs:
        rs = slice(c * RET_CHUNK, (c + 1) * RET_CHUNK)
        vs = slice(hh * DV, (hh + 1) * DV)
        o = _dot(inner_s[c * HEADS + hh], v_s[rs, vs])
        o = o + _dot(q_s[rs, hh * DK:(hh + 1) * DK], sb_s[c * HEADS + hh]) * xi_ref[hh]
        on = _groupnorm(o, gn_ref[:, vs])
        mix_s[rs, vs] = (gate_s[rs, vs] * on).astype(BF16)

    for c in range(n_chunks):
        rs = slice(c * CHUNK, (c + 1) * CHUNK)
        for g in range(GROUPS):
            gs = slice(g * GD, (g + 1) * GD)
            wg = jnp.where(tril, wsgu_ref[g], 0.0).astype(BF16)
            mixed = _dot(wg, sn_s[rs, gs]) + bsgu_ref[g]
            mix_s[rs, V_W + g * GD:V_W + (g + 1) * GD] = (u_s[rs, gs] * mixed).astype(BF16)

    x1_ref[0] = x + _dot(mix_s[...], wout_ref[...])


def _mix_prompt(x, gmix, win, wout, cos2, sin2, dmask, xi_b, zeta_b, cd, gn, lng, lnb, wsgu, bsgu_b,
                decay1, qs2, ks2, vs2, sstate):
    b, t, d = x.shape
    tm = TM_MIX
    nt = t // tm
    n_pairs = (tm // RET_CHUNK) * HEADS
    grid = (b, nt)
    n_hp, ms, _ = qs2.shape
    assert b * nt == (ms // BB_RET) * n_hp
    pair_blk = lambda i, j: ((i * nt + j) % n_hp, (i * nt + j) // n_hp, 0)
    state_blk = lambda i, j: (0, (i * nt + j) // n_hp, (i * nt + j) % n_hp, 0, 0)
    in_specs = [
        _smem_spec(),
        _smem_spec(),
        pl.BlockSpec((1, tm, d), lambda i, j: (i, j, 0)),
        _const_spec((1, d)),
        _const_spec((d, D_IN)),
        _const_spec((V_W + U_W, d)),
        pl.BlockSpec((tm, DK), lambda i, j: (j, 0)),
        pl.BlockSpec((tm, DK), lambda i, j: (j, 0)),
        _const_spec((HEADS, RET_CHUNK, RET_CHUNK)),
        _const_spec((HEADS, RET_CHUNK, DV)),
        _const_spec((HEADS, RET_CHUNK, DK)),
        _const_spec((1, V_W)),
        _const_spec((1, U_W)),
        _const_spec((1, U_W)),
        _const_spec((GROUPS, CHUNK, CHUNK)),
        _const_spec((GROUPS, CHUNK, GD)),
        pl.BlockSpec((1, BB_RET, HEAD_PAIR * DK), pair_blk),
        pl.BlockSpec((1, BB_RET, HEAD_PAIR * DK), pair_blk),
        pl.BlockSpec((1, BB_RET, HEAD_PAIR * DV), pair_blk),
        pl.BlockSpec((1, BB_RET, HEAD_PAIR, DK, DV), state_blk),
    ]
    out_specs = [
        pl.BlockSpec((1, tm, d), lambda i, j: (i, j, 0)),
        pl.BlockSpec((1, 1, HEADS, DK, DV), lambda i, j: (0, i, 0, 0, 0)),
        pl.BlockSpec((1, BB_RET, HEAD_PAIR, DK, DV), state_blk),
        pl.BlockSpec((1, BB_RET, HEAD_PAIR * DV), pair_blk),
    ]
    out_shape = [
        jax.ShapeDtypeStruct((b, t, d), F32),
        jax.ShapeDtypeStruct((1, b, HEADS, DK, DV), F32),
        jax.ShapeDtypeStruct(sstate.shape, F32),
        jax.ShapeDtypeStruct((n_hp, ms, HEAD_PAIR * DV), F32),
    ]
    scratch = [
        pltpu.VMEM((tm, Q_W), BF16),
        pltpu.VMEM((tm, Q_W), BF16),
        pltpu.VMEM((tm, Q_W), BF16),
        pltpu.VMEM((tm, V_W), BF16),
        pltpu.VMEM((tm, V_W), F32),
        pltpu.VMEM((tm, U_W), F32),
        pltpu.VMEM((tm, U_W), BF16),
        pltpu.VMEM((tm, V_W + U_W), BF16),
        pltpu.VMEM((n_pairs, RET_CHUNK, RET_CHUNK), BF16),
        pltpu.VMEM((n_pairs, DK, DV), F32),
        pltpu.VMEM((n_pairs, DK, DV), BF16),
    ]
    x1, state_p, state_s, o2 = pl.pallas_call(
        _mix_prompt_kernel, grid=grid, in_specs=in_specs, out_specs=out_specs,
        out_shape=out_shape, scratch_shapes=scratch,
        compiler_params=_params("arbitrary", "arbitrary"), name="mix_prompt",
    )(cd, decay1, x, gmix, win, wout, cos2, sin2, dmask, xi_b, zeta_b, gn, lng, lnb, wsgu, bsgu_b,
      qs2, ks2, vs2, sstate)
    return x1, state_p, state_s, o2


def _ffn_tile(x, g, gfin, w1_s, w2_s, final, before_slice=None):
    h = _rms(x, g).astype(BF16)
    acc = x
    for j in range(FFN_SLICES):
        if before_slice is not None:
            before_slice(j)
        js = slice(j * FFN_HS, (j + 1) * FFN_HS)
        f = _dot(h, w1_s[:, js])
        a = jnp.square(jnp.maximum(f, 0.0)).astype(BF16)
        acc = acc + _dot(a, w2_s[js, :])
    if final:
        acc = _rms(acc, gfin)
    return acc


def _ffn_kernel(*refs, layer, final, n_tiles, n_casts, n_pro, sample_prologue):
    xp_ref, xs_ref, g_ref, gfin_ref, w1_hbm, w2_hbm = refs[:6]
    cast_in = refs[6:6 + n_casts]
    pro_refs = refs[6 + n_casts:6 + n_casts + n_pro]
    n_in = 6 + n_casts + n_pro
    yp_ref, ys_ref = refs[n_in:n_in + 2]
    cast_out = refs[n_in + 2:n_in + 2 + n_casts]
    w1_s, w2_s, st1, st2, sem = refs[n_in + 2 + n_casts:]
    i = pl.program_id(0)

    if n_casts:
        @pl.when(jnp.logical_and(i >= 1, i <= CAST_STEPS))
        def _():
            for src, dst in zip(cast_in, cast_out):
                dst[...] = src[...].astype(BF16)

    def slice_copies(j):
        slot = j % 2
        js = pl.ds(j * FFN_HS, FFN_HS)
        return (pltpu.make_async_copy(w1_hbm.at[layer, :, js], st1.at[slot], sem.at[0, slot]),
                pltpu.make_async_copy(w2_hbm.at[layer, js, :], st2.at[slot], sem.at[1, slot]))

    def fetch_slice(j):
        js = slice(j * FFN_HS, (j + 1) * FFN_HS)
        for c in slice_copies(j):
            c.wait()
        w1_s[:, js] = st1[j % 2].astype(BF16)
        w2_s[js, :] = st2[j % 2].astype(BF16)
        if j + 2 < FFN_SLICES:
            for c in slice_copies(j + 2):
                c.start()

    @pl.when(i == 0)
    def _():
        for j in range(min(2, FFN_SLICES)):
            for c in slice_copies(j):
                c.start()
        yp_ref[...] = _ffn_tile(xp_ref[...], g_ref[...], gfin_ref[...], w1_s, w2_s, final, fetch_slice)

    @pl.when(jnp.logical_and(i > 0, i < n_tiles))
    def _():
        yp_ref[...] = _ffn_tile(xp_ref[...], g_ref[...], gfin_ref[...], w1_s, w2_s, final)

    @pl.when(i == n_tiles)
    def _():
        xs = sample_prologue(xs_ref[...], *pro_refs)
        ys_ref[...] = _ffn_tile(xs, g_ref[...], gfin_ref[...], w1_s, w2_s, final)


def _ffn(xp, xs, g, w1, w2, gfin, layer, final, sample_prologue, pro_args, casts=()):
    m, d = xp.shape
    ms = xs.shape[0]
    tm = TM_FFN
    n_tiles = m // tm
    assert n_tiles > CAST_STEPS
    tile = lambda i: (jnp.minimum(i, n_tiles - 1), 0)
    cast_blk = lambda i: (jnp.clip(i - 1, 0, CAST_STEPS - 1), 0)
    cast_specs = [pl.BlockSpec((c.shape[0] // CAST_STEPS, c.shape[1]), cast_blk) for c in casts]
    return pl.pallas_call(
        functools.partial(_ffn_kernel, layer=layer, final=final, n_tiles=n_tiles, n_casts=len(casts),
                          n_pro=len(pro_args), sample_prologue=sample_prologue),
        grid=(n_tiles + 1,),
        in_specs=[pl.BlockSpec((tm, d), tile), _const_spec((ms, d)),
                  _const_spec((1, d)), _const_spec((1, d)),
                  pl.BlockSpec(memory_space=pl.ANY), pl.BlockSpec(memory_space=pl.ANY)] + cast_specs
        + [_const_spec(a.shape) for a in pro_args],
        out_specs=[pl.BlockSpec((tm, d), tile), pl.BlockSpec((ms, d), lambda i: (0, 0))] + cast_specs,
        out_shape=[jax.ShapeDtypeStruct((m, d), F32), jax.ShapeDtypeStruct((ms, d), F32)]
        + [jax.ShapeDtypeStruct(c.shape, BF16) for c in casts],
        scratch_shapes=[pltpu.VMEM((d, D_FF), BF16), pltpu.VMEM((D_FF, d), BF16),
                        pltpu.VMEM((2, d, FFN_HS), F32), pltpu.VMEM((2, FFN_HS, d), F32),
                        pltpu.SemaphoreType.DMA((2, 2))],
        compiler_params=_params("arbitrary"), name="ffn_final" if final else "ffn",
    )(xp, xs, g, gfin, w1, w2, *casts, *pro_args)


def _glu_pieces(x, g, win_ref, bin_ref):
    h = _rms(x, g).astype(BF16)
    for c in range(D_MODEL // GD):
        ca = slice(c * GD, (c + 1) * GD)
        cg = slice(D_MODEL + c * GD, D_MODEL + (c + 1) * GD)
        a = _dot(h, win_ref[:, ca]) + bin_ref[:, ca]
        gt = _dot(h, win_ref[:, cg]) + bin_ref[:, cg]
        yield c, a * _sigmoid(gt)


def _glu_in(x, g, win_ref, bin_ref):
    return jnp.concatenate([p for _, p in _glu_pieces(x, g, win_ref, bin_ref)], axis=-1)


def _from_token_rows(buf, first_token, n):
    return jnp.concatenate(
        [buf[pl.ds(first_token * LANE_TILES + g, n, stride=LANE_TILES), :] for g in range(LANE_TILES)], axis=-1)


def _conv_block(r, full_s, conv_s, w3_ref, b3_ref):
    off0 = HIST_PAD - HIST
    acc = jnp.broadcast_to(b3_ref[...][None], (CONV_RB, LANE_TILES, LANES))
    for j in range(CONV_WIDTH):
        start = pl.multiple_of((r * CONV_RB + j + off0) * LANE_TILES, LANE_TILES)
        xj = full_s[pl.ds(start, CONV_RB * LANE_TILES), :].reshape(CONV_RB, LANE_TILES, LANES)
        acc = acc + xj * w3_ref[j][None]
    out0 = pl.multiple_of(r * CONV_RB * LANE_TILES, CONV_RB * LANE_TILES)
    conv_s[pl.ds(out0, CONV_RB * LANE_TILES), :] = acc.reshape(CONV_RB * LANE_TILES, LANES)


def _conv_prompt_kernel(x_ref, xprev_ref, gmix_ref, win_ref, bin_ref, dww_ref, dwb_ref, w3_ref, b3_ref,
                        lng_ref, lnb_ref, wout_ref, bout_ref, shist_ref, sxg_ref,
                        o_ref, hist_ref, snewhist_ref, sconv_ref,
                        full_s, conv_s, *, n_tiles, tiles_per_seq, sample_every):
    s = pl.program_id(0)
    tm = x_ref.shape[0]
    off0 = HIST_PAD - HIST
    hist_rows = HIST_PAD * LANE_TILES

    @pl.when(jnp.logical_and(s % sample_every == 0, s < n_tiles))
    def _():
        _conv_sample_state_update(shist_ref, sxg_ref, dww_ref, dwb_ref, snewhist_ref, sconv_ref)

    def front():
        for c, piece in _glu_pieces(x_ref[...], gmix_ref[...], win_ref, bin_ref):
            for e in range(GD // LANES):
                g = c * (GD // LANES) + e
                full_s[pl.ds(hist_rows + g, tm, stride=LANE_TILES), :] = piece[:, e * LANES:(e + 1) * LANES]

    def back():
        conv = _from_token_rows(conv_s, 0, tm)
        c = _silu(_layernorm(conv, lng_ref[...], lnb_ref[...])).astype(BF16)
        o_ref[...] = xprev_ref[...] + _dot(c, wout_ref[...]) + bout_ref[...]

    @pl.when(s % tiles_per_seq == 0)
    def _():
        full_s[0:hist_rows, :] = jnp.zeros((hist_rows, LANES), F32)

    @pl.when(s == 0)
    def _():
        front()

    @pl.when(jnp.logical_and(s > 0, s < n_tiles))
    def _():
        front()
        back()

    @pl.when(s == n_tiles)
    def _():
        back()

    @pl.when(s < n_tiles)
    def _():
        def body(r, carry):
            _conv_block(r, full_s, conv_s, w3_ref, b3_ref)
            return carry
        lax.fori_loop(0, tm // CONV_RB, body, 0)

        @pl.when(s % tiles_per_seq == tiles_per_seq - 1)
        def _():
            hist_ref[0, 0] = _from_token_rows(full_s, tm + off0, HIST)

        full_s[0:hist_rows, :] = full_s[tm * LANE_TILES:tm * LANE_TILES + hist_rows, :]


def _conv_prompt(x, gmix, win, bin_, dww, dwb, lng, lnb, wout, bout, shist_t, sxg):
    b, t, d = x.shape
    tm = TM_CONV
    n_tiles = (b * t) // tm
    tiles_per_seq = t // tm
    ms = sxg.shape[0]
    n_sblocks = ms // BB_CONV
    sample_every = n_tiles // n_sblocks
    assert sample_every * n_sblocks == n_tiles
    sblock = lambda s: jnp.minimum(s // sample_every, n_sblocks - 1)
    in_specs = [
        pl.BlockSpec((tm, d), lambda s: (jnp.minimum(s, n_tiles - 1), 0)),
        pl.BlockSpec((tm, d), lambda s: (jnp.maximum(s - 1, 0), 0)),
        _const_spec((1, d)), _const_spec((d, 2 * d)), _const_spec((1, 2 * d)),
        _const_spec((CONV_WIDTH, d)), _const_spec((1, d)),
        _const_spec((CONV_WIDTH, LANE_TILES, LANES)), _const_spec((LANE_TILES, LANES)),
        _const_spec((1, d)), _const_spec((1, d)),
        _const_spec((d, d)), _const_spec((1, d)),
        pl.BlockSpec((HIST, BB_CONV, d), lambda s: (0, sblock(s), 0)),
        pl.BlockSpec((BB_CONV, d), lambda s: (sblock(s), 0)),
    ]
    out_specs = [
        pl.BlockSpec((tm, d), lambda s: (jnp.maximum(s - 1, 0), 0)),
        pl.BlockSpec((1, 1, HIST, d), lambda s: (0, jnp.minimum(s, n_tiles - 1) // tiles_per_seq, 0, 0)),
        pl.BlockSpec((HIST, BB_CONV, d), lambda s: (0, sblock(s), 0)),
        pl.BlockSpec((BB_CONV, d), lambda s: (sblock(s), 0)),
    ]
    out_shape = [jax.ShapeDtypeStruct((b * t, d), F32),
                 jax.ShapeDtypeStruct((1, b, HIST, d), F32),
                 jax.ShapeDtypeStruct(shist_t.shape, F32),
                 jax.ShapeDtypeStruct((ms, d), F32)]
    scratch = [
        pltpu.VMEM(((tm + HIST_PAD) * LANE_TILES, LANES), F32),
        pltpu.VMEM((tm * LANE_TILES, LANES), F32),
    ]
    x2d = x.reshape(b * t, d)
    w3 = dww.reshape(CONV_WIDTH, LANE_TILES, LANES)
    b3 = dwb.reshape(LANE_TILES, LANES)
    return pl.pallas_call(
        functools.partial(_conv_prompt_kernel, n_tiles=n_tiles, tiles_per_seq=tiles_per_seq,
                          sample_every=sample_every),
        grid=(n_tiles + 1,), in_specs=in_specs, out_specs=out_specs,
        out_shape=out_shape, scratch_shapes=scratch,
        compiler_params=_params("arbitrary"), name="conv_prompt",
    )(x2d, x2d, gmix, win, bin_, dww, dwb, w3, b3, lng, lnb, wout, bout, shist_t, sxg)


def _mix_sample_in_kernel(sgu_ref, x_ref, gmix_ref, win_ref, cos_ref, sin_ref, lng_ref, lnb_ref,
                          q_ref, k_ref, v_ref, gate_ref, sgu_out_ref, rows_ref):
    h = _rms(x_ref[...], gmix_ref[...]).astype(BF16)
    qk = _dot(h, win_ref[:, 0:OFF_V])
    cos2 = cos_ref[...]
    sin2 = sin_ref[...]
    for j in range(HEADS):
        p, li = divmod(j, HEAD_PAIR)
        q_ref[p, :, li * DK:(li + 1) * DK] = _rotary(qk[:, j * DK:(j + 1) * DK], cos2, sin2)
        kj = qk[:, Q_W + j * DK:Q_W + (j + 1) * DK]
        k_ref[p, :, li * DK:(li + 1) * DK] = _rotary(kj, cos2, sin2) * (DK ** -0.5)
    v = _dot(h, win_ref[:, OFF_V:OFF_G])
    for p in range(HEADS // HEAD_PAIR):
        v_ref[p] = v[:, p * HEAD_PAIR * DV:(p + 1) * HEAD_PAIR * DV]
    gate_ref[...] = _silu(_dot(h, win_ref[:, OFF_G:OFF_U]))
    u = _gelu(_dot(h, win_ref[:, OFF_U:OFF_S]))
    s = _gelu(_dot(h, win_ref[:, OFF_S:D_IN]))
    for g in range(GROUPS):
        gs = slice(g * GD, (g + 1) * GD)
        sn = _layernorm(s[:, gs], lng_ref[:, gs], lnb_ref[:, gs])
        rows_ref[:, gs] = sn
        sgu_out_ref[:, gs] = u[:, gs] * (sgu_ref[0, g] * sn + sgu_ref[1, g])


def _mix_sample_in(x, gmix, win, cos2, sin2, lng, lnb, sgu_wb):
    m, d = x.shape
    shp = lambda n: jax.ShapeDtypeStruct((m, n), F32)
    n_hp = HEADS // HEAD_PAIR
    paired = lambda w: jax.ShapeDtypeStruct((n_hp, m, HEAD_PAIR * w), F32)
    vm = lambda: pl.BlockSpec(memory_space=pltpu.VMEM)
    return pl.pallas_call(
        _mix_sample_in_kernel,
        in_specs=[_smem_spec()] + [vm() for _ in range(7)],
        out_specs=[vm() for _ in range(6)],
        out_shape=[paired(DK), paired(DK), paired(DV), shp(V_W), shp(U_W), shp(U_W)],
        compiler_params=pltpu.CompilerParams(vmem_limit_bytes=VMEM_LIMIT), name="mix_sample_in",
    )(sgu_wb, x, gmix, win, cos2, sin2, lng, lnb)


def _mix_out_rows(x, o_ref, gate_ref, sgu_ref, gn_ref, wout_ref):
    parts = []
    for hh in range(HEADS):
        p, li = divmod(hh, HEAD_PAIR)
        vs = slice(hh * DV, (hh + 1) * DV)
        o = o_ref[p, :, li * DV:(li + 1) * DV]
        parts.append((gate_ref[:, vs] * _groupnorm(o, gn_ref[:, vs])).astype(BF16))
    y = _dot(jnp.concatenate(parts, axis=-1), wout_ref[0:V_W, :])
    y = y + _dot(sgu_ref[...].astype(BF16), wout_ref[V_W:V_W + U_W, :])
    return x + y


def _conv_sample_in_kernel(x_ref, g_ref, win_ref, bin_ref, xg_ref):
    xg_ref[...] = _glu_in(x_ref[...], g_ref[...], win_ref, bin_ref)


def _conv_sample_in(x, g, win, bin_):
    m, d = x.shape
    vm = lambda: pl.BlockSpec(memory_space=pltpu.VMEM)
    return pl.pallas_call(
        _conv_sample_in_kernel, in_specs=[vm() for _ in range(4)], out_specs=vm(),
        out_shape=jax.ShapeDtypeStruct((m, d), F32),
        compiler_params=pltpu.CompilerParams(vmem_limit_bytes=VMEM_LIMIT), name="conv_sample_in",
    )(x, g, win, bin_)


def _conv_sample_state_update(hist_ref, xg_ref, dww_ref, dwb_ref, newhist_ref, conv_ref):
    xg = xg_ref[...]
    acc = dwb_ref[...] + xg * dww_ref[HIST:CONV_WIDTH, :]
    for j in range(HIST):
        acc = acc + hist_ref[j] * dww_ref[j:j + 1, :]
    conv_ref[...] = acc
    for j in range(HIST - 1):
        newhist_ref[j] = hist_ref[j + 1]
    newhist_ref[HIST - 1] = xg


def _conv_out_rows(x, conv_ref, lng_ref, lnb_ref, wout_ref, bout_ref):
    c = _silu(_layernorm(conv_ref[...], lng_ref[...], lnb_ref[...])).astype(BF16)
    return x + _dot(c, wout_ref[...]) + bout_ref[...]


def _rope_tables(pos):
    half = DK // 2
    inv = ROPE_BASE ** (-jnp.arange(half, dtype=F32) / half)
    ang = pos.astype(F32)[:, None] * inv[None, :]
    cos = jnp.cos(ang)
    sin = jnp.sin(ang)
    return jnp.concatenate([cos, cos], axis=-1), jnp.concatenate([-sin, sin], axis=-1)


def _decay_tables(c):
    log_g = jnp.log1p(-(2.0 ** (-5.0 - jnp.arange(HEADS, dtype=F32))))
    idx = jnp.arange(c, dtype=F32)
    diff = idx[:, None] - idx[None, :]
    dmask = jnp.where(diff >= 0, jnp.exp(jnp.maximum(diff, 0.0)[None] * log_g[:, None, None]), 0.0)
    xi = jnp.exp((idx[None, :] + 1.0) * log_g[:, None])
    zeta = jnp.exp((c - 1.0 - idx[None, :]) * log_g[:, None])
    chunk_decay = jnp.exp(c * log_g)
    return dmask, xi, zeta, chunk_decay


def kernel(x_prompt, x_sample, state_ret, state_conv, norm_mix_g, norm_ffn_g, final_norm_g, ab_w_in, ab_w_out, ret_gn_g, sgu_ln_g, sgu_ln_b, sgu_w, sgu_b, conv_w_in, conv_b_in, conv_dw_w, conv_dw_b, conv_ln_g, conv_ln_b, conv_w_out, conv_b_out, ffn_w1, ffn_w2):
    b, t, d = x_prompt.shape
    m_s = x_sample.shape[0]
    row = lambda a: a.reshape(1, -1)

    win = ab_w_in[0].astype(BF16)
    wout = ab_w_out[0].astype(BF16)
    gmix = [row(norm_mix_g[l]) for l in range(2)]
    gffn = [row(norm_ffn_g[l]) for l in range(2)]
    gfin = row(final_norm_g)
    gn, lng, lnb = row(ret_gn_g[0]), row(sgu_ln_g[0]), row(sgu_ln_b[0])
    cbin, cdwb, clng, clnb, cbout = (row(conv_b_in[0]), row(conv_dw_b[0]), row(conv_ln_g[0]),
                                     row(conv_ln_b[0]), row(conv_b_out[0]))

    cos_p, sin_p = _rope_tables(jnp.arange(t))
    dmask, xi, zeta, cd = _decay_tables(RET_CHUNK)
    xi_b = jnp.broadcast_to(xi[:, :, None], (HEADS, RET_CHUNK, DV))
    zeta_b = jnp.broadcast_to(zeta[:, :, None], (HEADS, RET_CHUNK, DK))
    bsgu_b = jnp.broadcast_to(sgu_b[0][:, :CHUNK, None], (GROUPS, CHUNK, GD))

    xs = x_sample.reshape(m_s, d)
    cos_s, sin_s = _rope_tables(PAST_LEN + jnp.arange(1))
    decay1 = xi[:, 0]
    sgu_wb = jnp.stack([sgu_w[0][:, 0, 0], sgu_b[0][:, 0]])
    q, k, v, gate, sgu_out, rows = _mix_sample_in(xs, gmix[0], win, cos_s, sin_s, lng, lnb, sgu_wb)
    x1, ret_state_p, ret_state_s, o = _mix_prompt(x_prompt, gmix[0], win, wout, cos_p, sin_p, dmask, xi_b, zeta_b,
                                                  cd, gn, lng, lnb, sgu_w[0], bsgu_b, decay1, q, k, v, state_ret)

    x2, xs2, cwin, cwout = _ffn(x1.reshape(b * t, d), xs, gffn[0], ffn_w1, ffn_w2, gfin, 0, False,
                                _mix_out_rows, (o, gate, sgu_out, gn, wout),
                                casts=(conv_w_in[0], conv_w_out[0]))

    xg = _conv_sample_in(xs2, gmix[1], cwin, cbin)
    hist_t = jnp.transpose(state_conv[0], (1, 0, 2))
    x3, conv_state_p, newhist_t, conv = _conv_prompt(x2.reshape(b, t, d), gmix[1], cwin, cbin, conv_dw_w[0], cdwb,
                                                     clng, clnb, cwout, cbout, hist_t, xg)
    conv_state_s = jnp.transpose(newhist_t, (1, 0, 2))[None]

    y_prompt, y_sample = _ffn(x3, xs2, gffn[1], ffn_w1, ffn_w2, gfin, 1, True,
                              _conv_out_rows, (conv, clng, clnb, cwout, cbout))

    return (y_prompt.reshape(b, t, d), y_sample.reshape(m_s, 1, d), ret_state_p, ret_state_s,
            rows.reshape(1, m_s, 1, U_W), conv_state_p, conv_state_s)
```
